```python
import math
import jax, jax.numpy as jnp
from jax import lax
import numpy as np

D_MODEL = 1024
BATCH = 8
SEQ = 4096
DEPTH = 2

CTX_LEN = 256
GRID_W = 64

FT_GROUPS = 4
FT_GROUP_DIM = 64
FT_WIDTH = FT_GROUPS * FT_GROUP_DIM
NA_HEADS = 6
NA_HEAD_DIM = 64
NA_WIDTH = NA_HEADS * NA_HEAD_DIM
NA_WIN_H = 8
NA_WIN_W = 16
DF_HEADS = 6
DF_QK_DIM = 32
DF_V_DIM = 2 * DF_QK_DIM
DF_QK_WIDTH = DF_HEADS * 2 * DF_QK_DIM
DF_WIDTH = DF_HEADS * DF_V_DIM
N_BRANCHES = 3
N_EXPERTS = 16
EC_CAPACITY_FACTOR = 2
EXPERT_FF = 2 * D_MODEL

ROPE_BASE = 10000.0
QUERY_BLOCK = 128
EPS = 1e-6

OFF_Q = FT_WIDTH
OFF_KV = OFF_Q + NA_WIDTH + DF_QK_WIDTH
KV_WIDTH = 2 * NA_WIDTH + DF_QK_WIDTH + DF_WIDTH
OFF_GATE = OFF_KV + KV_WIDTH
PROJ_WIDTH = OFF_GATE + N_BRANCHES * D_MODEL

kernel_name = "hybrid_fourier_natten_diffattn_ec_moe_dit"


def rms_norm(x, g):
    xf = x.astype(jnp.float32)
    y = xf * lax.rsqrt(jnp.mean(xf * xf, axis=-1, keepdims=True) + EPS)
    return (y * g.astype(jnp.float32)).astype(x.dtype)


def modulate(h, shift, scale):
    return h * (1 + scale) + shift


def split6(m):
    return [p[..., None, :] for p in jnp.split(m, 6, axis=-1)]


def axial_rope_tables(n, dim):
    t = jnp.arange(n)
    row = (t // GRID_W).astype(jnp.float32)
    col = (t % GRID_W).astype(jnp.float32)
    ax = dim // 2
    inv = ROPE_BASE ** (-jnp.arange(0, ax, 2, dtype=jnp.float32) / ax)
    ar = row[:, None] * inv
    ac = col[:, None] * inv
    return jnp.cos(ar), jnp.sin(ar), jnp.cos(ac), jnp.sin(ac)


def apply_axial_rope(x, tabs):
    cr, sr, cc, sc = tabs
    n = x.shape[1]
    half = x.shape[-1] // 2

    def bc(a):
        return a.reshape((1, n) + (1,) * (x.ndim - 3) + (a.shape[-1],))

    def rot(v, cos, sin):
        v1, v2 = jnp.split(v, 2, axis=-1)
        return jnp.concatenate([v1 * cos - v2 * sin, v2 * cos + v1 * sin], axis=-1)

    xf = x.astype(jnp.float32)
    out = jnp.concatenate([rot(xf[..., :half], bc(cr), bc(sr)),
                           rot(xf[..., half:], bc(cc), bc(sc))], axis=-1)
    return out.astype(x.dtype)


def q_heads(pq, na_qn_g, df_qn_g):
    b, n, _ = pq.shape
    nq, dq = jnp.split(pq, [NA_WIDTH], axis=-1)
    nq = rms_norm(nq.reshape(b, n, NA_HEADS, NA_HEAD_DIM), na_qn_g)
    dq = rms_norm(dq.reshape(b, n, DF_HEADS, 2, DF_QK_DIM), df_qn_g)
    return nq, dq


def kv_heads(pkv, na_kn_g, df_kn_g):
    b, n, _ = pkv.shape
    nk, nv, dk, dv = jnp.split(pkv, [NA_WIDTH, 2 * NA_WIDTH, 2 * NA_WIDTH + DF_QK_WIDTH], axis=-1)
    nk = rms_norm(nk.reshape(b, n, NA_HEADS, NA_HEAD_DIM), na_kn_g)
    nv = nv.reshape(b, n, NA_HEADS, NA_HEAD_DIM)
    dk = rms_norm(dk.reshape(b, n, DF_HEADS, 2, DF_QK_DIM), df_kn_g)
    dv = dv.reshape(b, n, DF_HEADS, DF_V_DIM)
    return nk, nv, dk, dv


def fourier_mix(u):
    b, n, _ = u.shape
    ug = u.astype(jnp.float32).reshape(b, n, FT_GROUPS, FT_GROUP_DIM)
    f = jnp.fft.fft2(ug, axes=(1, 3), norm="ortho")
    return jnp.real(f).reshape(b, n, FT_WIDTH).astype(u.dtype)


def neighbourhood_attention(q, k, v, kc, vc, rpb):
    b, n, h, d = q.shape
    rows = n // GRID_W
    kh = min(NA_WIN_H, rows)
    scale = d ** -0.5
    qg = q.reshape(b, rows, GRID_W, h, d)
    kg = k.reshape(b, rows, GRID_W, h, d)
    vg = v.reshape(b, rows, GRID_W, h, d)
    cols = jnp.arange(GRID_W)
    col_start = jnp.clip(cols - NA_WIN_W // 2, 0, GRID_W - NA_WIN_W)
    col_idx = col_start[:, None] + jnp.arange(NA_WIN_W)[None, :]
    col_off = col_idx - cols[:, None]
    rpb_c = rpb[:, :, col_off + NA_WIN_W - 1]
    row_starts = jnp.clip(jnp.arange(rows) - kh // 2, 0, rows - kh)

    def one_row(r):
        rs = row_starts[r]
        kb = lax.dynamic_slice_in_dim(kg, rs, kh, axis=1)
        vb = lax.dynamic_slice_in_dim(vg, rs, kh, axis=1)
        kw = kb[:, :, col_idx]
        vw = vb[:, :, col_idx]
        qr = lax.dynamic_index_in_dim(qg, r, axis=1, keepdims=False)
        row_off = rs + jnp.arange(kh) - r
        bias = rpb_c[:, row_off + NA_WIN_H - 1].transpose(0, 2, 1, 3)
        s_loc = jnp.einsum('bwhd,bawkhd->bhwak', qr, kw).astype(jnp.float32) * scale + bias[None].astype(jnp.float32)
        s_ctx = jnp.einsum('bwhd,blhd->bhwl', qr, kc).astype(jnp.float32) * scale
        s = jnp.concatenate([s_loc.reshape(b, h, GRID_W, kh * NA_WIN_W), s_ctx], axis=-1)
        p = jax.nn.softmax(s, axis=-1).astype(v.dtype)
        p_loc = p[..., :kh * NA_WIN_W].reshape(b, h, GRID_W, kh, NA_WIN_W)
        p_ctx = p[..., kh * NA_WIN_W:]
        return (jnp.einsum('bhwak,bawkhd->bwhd', p_loc, vw)
                + jnp.einsum('bhwl,blhd->bwhd', p_ctx, vc))

    out = lax.map(one_row, jnp.arange(rows))
    return out.transpose(1, 0, 2, 3, 4).reshape(b, n, h, d)


def context_attention(q, k, v):
    s = jnp.einsum('bqhd,bkhd->bhqk', q, k).astype(jnp.float32) * q.shape[-1] ** -0.5
    p = jax.nn.softmax(s, axis=-1).astype(v.dtype)
    return jnp.einsum('bhqk,bkhd->bqhd', p, v)


def diff_lambda(lp, lam_init):
    lp = lp.astype(jnp.float32)
    return jnp.exp(jnp.sum(lp[0] * lp[1])) - jnp.exp(jnp.sum(lp[2] * lp[3])) + lam_init


def diff_attention(q, k, v, lam):
    b, n, h, _, d = q.shape
    scale = d ** -0.5
    qb = q.reshape(b, n // QUERY_BLOCK, QUERY_BLOCK, h, 2, d).transpose(1, 0, 2, 3, 4, 5)

    def block(qi):
        s = jnp.einsum('bqhcd,bkhcd->bchqk', qi, k).astype(jnp.float32) * scale
        p = jax.nn.softmax(s, axis=-1)
        a = (p[:, 0] - lam * p[:, 1]).astype(v.dtype)
        return jnp.einsum('bhqk,bkhe->bqhe', a, v)

    o = lax.map(block, qb)
    return o.transpose(1, 0, 2, 3, 4).reshape(b, n, h, v.shape[-1])


def merge_branches(u_ft, o_na, o_df, gates, w_ft, w_na_o, w_df_o, w_out):
    b, n, _ = u_ft.shape
    y_ft = fourier_mix(u_ft) @ w_ft
    y_na = o_na.reshape(b, n, NA_WIDTH) @ w_na_o
    y_df = o_df.reshape(b, n, DF_WIDTH) @ w_df_o
    g = jax.nn.sigmoid(gates.astype(jnp.float32)).astype(u_ft.dtype).reshape(b, n, N_BRANCHES, D_MODEL)
    m = g[:, :, 0] * y_ft + g[:, :, 1] * y_na + g[:, :, 2] * y_df
    return m @ w_out


def expert_choice_ffn(h, w_router, w_gate, w_up, w_down):
    b, n, d = h.shape
    cap = EC_CAPACITY_FACTOR * n // N_EXPERTS
    aff = jax.nn.softmax(jnp.einsum('bnd,de->ben', h, w_router).astype(jnp.float32), axis=1)
    g, idx = lax.top_k(aff, cap)
    xe = jax.vmap(lambda hb, ib: hb[ib])(h, idx)

    def expert(args):
        xi, wg, wu, wd = args
        return (jax.nn.silu(xi @ wg) * (xi @ wu)) @ wd

    ye = lax.map(expert, (xe.transpose(1, 0, 2, 3), w_gate, w_up, w_down))
    ye = ye.transpose(1, 0, 2, 3) * g[..., None].astype(h.dtype)
    return jax.vmap(lambda yb, ib: jnp.zeros((n, d), yb.dtype).at[ib.reshape(-1)].add(yb.reshape(-1, d)))(ye, idx)


def setup_inputs(seed: int = 0) -> dict:
    key = jax.random.key(seed)
    ks = jax.random.split(key, 24)
    D, L, F, E = D_MODEL, DEPTH, EXPERT_FF, N_EXPERTS
    nrm = jax.random.normal
    f32 = jnp.float32
    return {
        "x": nrm(ks[0], (BATCH, SEQ, D), f32),
        "c": nrm(ks[1], (BATCH, D), f32),
        "ctx": nrm(ks[2], (BATCH, CTX_LEN, D), f32),
        "c_ctx": nrm(ks[3], (D,), f32),
        "norm1_g": 1.0 + 0.02 * nrm(ks[4], (L, D), f32),
        "norm2_g": 1.0 + 0.02 * nrm(ks[5], (L, D), f32),
        "w_ada": 0.5 * D ** -0.5 * nrm(ks[6], (L, D, 6 * D), f32),
        "b_ada": 0.01 * nrm(ks[7], (L, 6 * D), f32),
        "w_in": D ** -0.5 * nrm(ks[8], (L, D, PROJ_WIDTH), f32),
        "na_qn_g": 1.0 + 0.02 * nrm(ks[9], (L, NA_HEAD_DIM), f32),
        "na_kn_g": 1.0 + 0.02 * nrm(ks[10], (L, NA_HEAD_DIM), f32),
        "na_rpb": 0.1 * nrm(ks[11], (L, NA_HEADS, 2 * NA_WIN_H - 1, 2 * NA_WIN_W - 1), f32),
        "df_qn_g": 1.0 + 0.02 * nrm(ks[12], (L, DF_QK_DIM), f32),
        "df_kn_g": 1.0 + 0.02 * nrm(ks[13], (L, DF_QK_DIM), f32),
        "df_lambda": 0.1 * nrm(ks[14], (L, 4, DF_QK_DIM), f32),
        "df_subln_g": 1.0 + 0.02 * nrm(ks[15], (L, DF_V_DIM), f32),
        "w_ft": FT_WIDTH ** -0.5 * nrm(ks[16], (L, FT_WIDTH, D), f32),
        "w_na_o": NA_WIDTH ** -0.5 * nrm(ks[17], (L, NA_WIDTH, D), f32),
        "w_df_o": DF_WIDTH ** -0.5 * nrm(ks[18], (L, DF_WIDTH, D), f32),
        "w_out": D ** -0.5 * nrm(ks[19], (L, D, D), f32),
        "w_router": D ** -0.5 * nrm(ks[20], (L, D, E), f32),
        "w_gate": D ** -0.5 * nrm(ks[21], (L, E, D, F), f32),
        "w_up": D ** -0.5 * nrm(ks[22], (L, E, D, F), f32),
        "w_down": F ** -0.5 * nrm(ks[23], (L, E, F, D), f32),
    }


def reference(x, c, ctx, c_ctx, norm1_g, norm2_g, w_ada, b_ada, w_in, na_qn_g, na_kn_g, na_rpb,
              df_qn_g, df_kn_g, df_lambda, df_subln_g, w_ft, w_na_o, w_df_o, w_out,
              w_router, w_gate, w_up, w_down):
    n = x.shape[1]
    rope = axial_rope_tables(n, DF_QK_DIM)
    xc = ctx
    for i in range(DEPTH):
        last = i == DEPTH - 1
        lam_init = 0.8 - 0.6 * math.exp(-0.3 * i)
        sh1, sc1, g1, sh2, sc2, g2 = split6(jax.nn.silu(c) @ w_ada[i] + b_ada[i])
        csh1, csc1, cg1, csh2, csc2, cg2 = split6(jax.nn.silu(c_ctx) @ w_ada[i] + b_ada[i])
        lam = diff_lambda(df_lambda[i], lam_init)

        hc = modulate(rms_norm(xc, norm1_g[i]), csh1, csc1)
        if last:
            pc_kv = hc @ w_in[i][:, OFF_KV:OFF_GATE]
        else:
            pc = hc @ w_in[i]
            pc_kv = pc[..., OFF_KV:OFF_GATE]
        nkc, nvc, dkc, dvc = kv_heads(pc_kv, na_kn_g[i], df_kn_g[i])

        h = modulate(rms_norm(x, norm1_g[i]), sh1, sc1)
        p = h @ w_in[i]
        nq, dq = q_heads(p[..., OFF_Q:OFF_KV], na_qn_g[i], df_qn_g[i])
        nk, nv, dk, dv = kv_heads(p[..., OFF_KV:OFF_GATE], na_kn_g[i], df_kn_g[i])
        dq = apply_axial_rope(dq, rope)
        dk = apply_axial_rope(dk, rope)
        o_na = neighbourhood_attention(nq, nk, nv, nkc, nvc, na_rpb[i])
        o_df = diff_attention(dq, jnp.concatenate([dk, dkc], axis=1),
                              jnp.concatenate([dv, dvc], axis=1), lam)
        o_df = rms_norm(o_df, df_subln_g[i]) * (1 - lam_init)
        y = merge_branches(p[..., :FT_WIDTH], o_na, o_df, p[..., OFF_GATE:],
                           w_ft[i], w_na_o[i], w_df_o[i], w_out[i])
        x = x + g1 * y
        h2 = modulate(rms_norm(x, norm2_g[i]), sh2, sc2)
        x = x + g2 * expert_choice_ffn(h2, w_router[i], w_gate[i], w_up[i], w_down[i])

        if not last:
            ncq, dcq = q_heads(pc[..., OFF_Q:OFF_KV], na_qn_g[i], df_qn_g[i])
            o_nac = context_attention(ncq, nkc, nvc)
            o_dfc = rms_norm(diff_attention(dcq, dkc, dvc, lam), df_subln_g[i]) * (1 - lam_init)
            yc = merge_branches(pc[..., :FT_WIDTH], o_nac, o_dfc, pc[..., OFF_GATE:],
                                w_ft[i], w_na_o[i], w_df_o[i], w_out[i])
            xc = xc + cg1 * yc
            hc2 = modulate(rms_norm(xc, norm2_g[i]), csh2, csc2)
            xc = xc + cg2 * expert_choice_ffn(hc2, w_router[i], w_gate[i], w_up[i], w_down[i])
    return x
```

```python
import functools
import math

import jax
import jax.numpy as jnp
from jax import lax
from jax.experimental import pallas as pl
from jax.experimental.pallas import tpu as pltpu

F32 = jnp.float32
BF16 = jnp.bfloat16

GRID_W = 64
FT_GROUPS = 4
FT_GROUP_DIM = 64
FT_WIDTH = FT_GROUPS * FT_GROUP_DIM
NA_HEADS = 6
NA_HEAD_DIM = 64
NA_WIDTH = NA_HEADS * NA_HEAD_DIM
NA_WIN_H = 8
NA_WIN_W = 16
DF_HEADS = 6
DF_QK_DIM = 32
DF_V_DIM = 2 * DF_QK_DIM
DF_QK_WIDTH = DF_HEADS * 2 * DF_QK_DIM
DF_WIDTH = DF_HEADS * DF_V_DIM
N_BRANCHES = 3
N_EXPERTS = 16
EC_CAPACITY_FACTOR = 2
ROPE_BASE = 10000.0
EPS = 1e-6
MASK_VALUE = -1e30

LANES = 128
VMEM_LIMIT_BYTES = 56 * 1024 * 1024

HEAD_W = 384
N_PAIRS = HEAD_W // LANES
MOD_ROWS = 16
ROUTER_PAD = LANES


def _cparams(sem):
    return pltpu.CompilerParams(dimension_semantics=sem, vmem_limit_bytes=VMEM_LIMIT_BYTES)


def _dot(a, b):
    return jnp.dot(a, b, preferred_element_type=F32)


def _dot_nt(a, b):
    return lax.dot_general(a, b, (((1,), (1,)), ((), ())), preferred_element_type=F32)


def _split_bf16(v):
    hi = v.astype(BF16)
    lo = (v - hi.astype(F32)).astype(BF16)
    return hi, lo


def _ada_kernel(c_ref, w_ref, b_ref, o_ref):
    c = c_ref[...]
    a = c * jax.nn.sigmoid(c)
    a_hi, a_lo = _split_bf16(a)
    w_hi, w_lo = _split_bf16(w_ref[0])
    acc = _dot(a_hi, w_hi) + _dot(a_lo, w_hi) + _dot(a_hi, w_lo)
    o_ref[0] = acc + b_ref[0]


def _ada(cc, w_ada, b_ada):
    depth, d, d6 = w_ada.shape
    tn = 512
    return pl.pallas_call(
        _ada_kernel,
        grid=(depth, d6 // tn),
        in_specs=[
            pl.BlockSpec((MOD_ROWS, d), lambda l, j: (0, 0)),
            pl.BlockSpec((1, d, tn), lambda l, j: (l, 0, j)),
            pl.BlockSpec((1, 1, tn), lambda l, j: (l, 0, j)),
        ],
        out_specs=pl.BlockSpec((1, MOD_ROWS, tn), lambda l, j: (l, 0, j)),
        out_shape=jax.ShapeDtypeStruct((depth, MOD_ROWS, d6), F32),
        compiler_params=_cparams(("arbitrary", "arbitrary")),
        name="ada",
    )(cc, w_ada, b_ada.reshape(depth, 1, d6))


OFF_Q = FT_WIDTH
OFF_DQ = OFF_Q + NA_WIDTH
OFF_KV = OFF_DQ + DF_QK_WIDTH
OFF_NV = OFF_KV + NA_WIDTH
OFF_DK = OFF_NV + NA_WIDTH
OFF_DV = OFF_DK + DF_QK_WIDTH
OFF_GATE = OFF_DV + DF_WIDTH


def _group_rms(v, gmat_ref, gain_ref):
    hi, lo = _split_bf16(v * v)
    ms = _dot(hi, gmat_ref[...]) + _dot(lo, gmat_ref[...])
    return v * lax.rsqrt(ms + EPS) * gain_ref[...]


def _rope_chunk(vj, cos, sin_signed, first_half):
    fwd = pltpu.roll(vj, LANES - 8, 1)
    bwd = pltpu.roll(vj, 8, 1)
    partner = jnp.where(first_half, fwd, bwd)
    return vj * cos + partner * sin_signed


def _inproj_kernel(x_ref, sh_ref, sc_ref, ng_ref, w_ref, cd_ref, g64_ref, g32_ref,
                   naq_g_ref, nak_g_ref, dfq_g_ref, dfk_g_ref, cos_ref, sin_ref,
                   *out_refs, rope, kv_only):
    x = x_ref[0]
    ms = jnp.mean(x * x, axis=-1, keepdims=True)
    y = x * lax.rsqrt(ms + EPS) * ng_ref[...]
    h = (y * (1.0 + sc_ref[0]) + sh_ref[0]).astype(BF16)

    def proj(c0, c1):
        return _dot(h, w_ref[:, c0:c1])

    if rope:
        lane = lax.broadcasted_iota(jnp.int32, (x.shape[0], LANES), 1)
        first_half = (lane & 8) == 0
        cos = cos_ref[...]
        sin_signed = sin_ref[...]

    def df_qk(v, gain_ref, scale, o_ref):
        v = _group_rms(v, g32_ref, gain_ref)
        for j in range(N_PAIRS):
            vj = v[:, j * LANES:(j + 1) * LANES]
            if rope:
                vj = _rope_chunk(vj, cos, sin_signed, first_half)
            o_ref[0, :, j * LANES:(j + 1) * LANES] = (vj * scale).astype(BF16)

    if kv_only:
        nk_ref, nv_ref, dk_ref, dv_ref = out_refs
    else:
        fab_ref, nq_ref, dq_ref, nk_ref, nv_ref, dk_ref, dv_ref, gate_ref = out_refs
        u = proj(0, FT_WIDTH)
        fab_ref[0] = _dot(u.astype(BF16), cd_ref[...]).astype(BF16)
        nq = _group_rms(proj(OFF_Q, OFF_DQ), g64_ref, naq_g_ref)
        nq_ref[0] = (nq * (NA_HEAD_DIM ** -0.5)).astype(BF16)
        df_qk(proj(OFF_DQ, OFF_KV), dfq_g_ref, DF_QK_DIM ** -0.5, dq_ref)
        d = x.shape[1]
        for j in range(N_BRANCHES):
            z = proj(OFF_GATE + j * d, OFF_GATE + (j + 1) * d)
            gate_ref[0, :, j * d:(j + 1) * d] = jax.nn.sigmoid(z).astype(BF16)

    nk_ref[0] = _group_rms(proj(OFF_KV, OFF_NV), g64_ref, nak_g_ref).astype(BF16)
    nv_ref[0] = proj(OFF_NV, OFF_DK).astype(BF16)
    df_qk(proj(OFF_DK, OFF_DV), dfk_g_ref, 1.0, dk_ref)
    dv_ref[0] = proj(OFF_DV, OFF_GATE).astype(BF16)


def _inproj(x, sh, sc, norm_g, w_in, consts, gains, rope_tabs, *, rope, kv_only):
    b, n, d = x.shape
    tm = min(256, n)
    pw = w_in.shape[1]
    tok = lambda w: pl.BlockSpec((1, tm, w), lambda bi, i: (bi, i, 0))
    full = lambda a: pl.BlockSpec(a.shape, lambda bi, i: (0,) * a.ndim)
    mod = pl.BlockSpec((1, 1, d), lambda bi, i: (bi, 0, 0))
    cos_t, sin_t = rope_tabs
    tab = pl.BlockSpec((tm, LANES), lambda bi, i: (i, 0))
    slab = jax.ShapeDtypeStruct((b, n, HEAD_W), BF16)
    if kv_only:
        out_shape = [slab] * 4
        out_specs = [tok(HEAD_W)] * 4
    else:
        out_shape = [jax.ShapeDtypeStruct((b, n, 2 * FT_WIDTH), BF16)] + [slab] * 6 + [
            jax.ShapeDtypeStruct((b, n, N_BRANCHES * d), BF16)]
        out_specs = [tok(2 * FT_WIDTH)] + [tok(HEAD_W)] * 6 + [tok(N_BRANCHES * d)]
    args = [x, sh, sc, norm_g, w_in, consts["chan_dft"], consts["g64"], consts["g32"],
            gains["naq"], gains["nak"], gains["dfq"], gains["dfk"], cos_t, sin_t]
    in_specs = [tok(d), mod, mod, full(norm_g), full(w_in), full(consts["chan_dft"]),
                full(consts["g64"]), full(consts["g32"]), full(gains["naq"]), full(gains["nak"]),
                full(gains["dfq"]), full(gains["dfk"]), tab, tab]
    del pw
    return pl.pallas_call(
        functools.partial(_inproj_kernel, rope=rope, kv_only=kv_only),
        grid=(b, n // tm),
        in_specs=in_specs,
        out_specs=out_specs,
        out_shape=out_shape,
        compiler_params=_cparams(("parallel", "parallel")),
        name="inproj_kv" if kv_only else "inproj",
    )(*args)


def _fourier_kernel(c_ref, s_ref, ab_ref, o_ref, acc_ref, *, scale):
    k = pl.program_id(1)

    @pl.when(k == 0)
    def _():
        acc_ref[...] = jnp.zeros_like(acc_ref)

    cm = c_ref[...]
    sm = s_ref[...]
    for bi in range(ab_ref.shape[0]):
        acc_ref[bi] += _dot(cm, ab_ref[bi, :, :FT_WIDTH]) + _dot(sm, ab_ref[bi, :, FT_WIDTH:])

    @pl.when(k == pl.num_programs(1) - 1)
    def _():
        o_ref[...] = (acc_ref[...] * scale).astype(o_ref.dtype)


def _fourier(fab, cos_m, nsin_m):
    b, n, _ = fab.shape
    tn = min(1024, n)
    tk = min(512, n)
    scale = 1.0 / math.sqrt(n * FT_GROUP_DIM)
    return pl.pallas_call(
        functools.partial(_fourier_kernel, scale=scale),
        grid=(n // tn, n // tk),
        in_specs=[
            pl.BlockSpec((tn, tk), lambda i, k: (i, k)),
            pl.BlockSpec((tn, tk), lambda i, k: (i, k)),
            pl.BlockSpec((b, tk, 2 * FT_WIDTH), lambda i, k: (0, k, 0)),
        ],
        out_specs=pl.BlockSpec((b, tn, FT_WIDTH), lambda i, k: (0, i, 0)),
        out_shape=jax.ShapeDtypeStruct((b, n, FT_WIDTH), BF16),
        scratch_shapes=[pltpu.VMEM((b, tn, FT_WIDTH), F32)],
        compiler_params=_cparams(("parallel", "arbitrary")),
        name="fourier",
    )(cos_m, nsin_m, fab)


def _na_kernel(q_ref, k_ref, v_ref, kc_ref, vc_ref, bias_ref, o_ref, *, rows, kh):
    r = pl.program_id(1)
    rs = jnp.clip(r - kh // 2, 0, rows - kh)
    base = r - rs
    start = pl.multiple_of(rs * GRID_W, GRID_W)
    lane = lax.broadcasted_iota(jnp.int32, (GRID_W, LANES), 1)
    for pr in range(N_PAIRS):
        cols = slice(pr * LANES, (pr + 1) * LANES)
        q2 = q_ref[0, :, cols]
        kb = k_ref[0, pl.ds(start, kh * GRID_W), cols]
        vb = v_ref[0, pl.ds(start, kh * GRID_W), cols]
        kcb = kc_ref[0, :, cols]
        vcb = vc_ref[0, :, cols]
        acc = jnp.zeros((GRID_W, LANES), F32)
        for hh in range(2):
            in_head = (lane >= hh * NA_HEAD_DIM) & (lane < (hh + 1) * NA_HEAD_DIM)
            qm = jnp.where(in_head, q2, jnp.zeros_like(q2))
            s1 = _dot_nt(qm, kb) + bias_ref[base, pr * 2 + hh]
            s2 = _dot_nt(qm, kcb)
            m = jnp.maximum(jnp.max(s1, axis=-1, keepdims=True), jnp.max(s2, axis=-1, keepdims=True))
            p1 = jnp.exp(s1 - m)
            p2 = jnp.exp(s2 - m)
            l = jnp.sum(p1, axis=-1, keepdims=True) + jnp.sum(p2, axis=-1, keepdims=True)
            o = _dot(p1.astype(BF16), vb) + _dot(p2.astype(BF16), vcb)
            acc = jnp.where(in_head, o / l, acc)
        o_ref[0, :, cols] = acc.astype(BF16)


def _na_attention(nq, nk, nv, nkc, nvc, bias):
    b, n, _ = nq.shape
    lc = nkc.shape[1]
    rows = n // GRID_W
    kh = min(NA_WIN_H, rows)
    return pl.pallas_call(
        functools.partial(_na_kernel, rows=rows, kh=kh),
        grid=(b, rows),
        in_specs=[
            pl.BlockSpec((1, GRID_W, HEAD_W), lambda bi, r: (bi, r, 0)),
            pl.BlockSpec((1, n, HEAD_W), lambda bi, r: (bi, 0, 0)),
            pl.BlockSpec((1, n, HEAD_W), lambda bi, r: (bi, 0, 0)),
            pl.BlockSpec((1, lc, HEAD_W), lambda bi, r: (bi, 0, 0)),
            pl.BlockSpec((1, lc, HEAD_W), lambda bi, r: (bi, 0, 0)),
            pl.BlockSpec(bias.shape, lambda bi, r: (0, 0, 0, 0)),
        ],
        out_specs=pl.BlockSpec((1, GRID_W, HEAD_W), lambda bi, r: (bi, r, 0)),
        out_shape=jax.ShapeDtypeStruct((b, n, HEAD_W), BF16),
        compiler_params=_cparams(("parallel", "arbitrary")),
        name="na_attention",
    )(nq, nk, nv, nkc, nvc, bias)


def _na_bias_table(rpb, rows):
    kh = min(NA_WIN_H, rows)
    cols = jnp.arange(GRID_W)
    col_start = jnp.clip(cols - NA_WIN_W // 2, 0, GRID_W - NA_WIN_W)
    kc = jnp.arange(GRID_W)
    in_win = (kc[None, :] >= col_start[:, None]) & (kc[None, :] < col_start[:, None] + NA_WIN_W)
    col_off = jnp.clip(kc[None, :] - cols[:, None] + NA_WIN_W - 1, 0, 2 * NA_WIN_W - 2)
    base = jnp.arange(NA_WIN_H)
    a = jnp.arange(kh)
    row_off = jnp.clip(a[None, :] - base[:, None] + NA_WIN_H - 1, 0, 2 * NA_WIN_H - 2)
    t = rpb[:, row_off]
    t = t[:, :, :, col_off]
    t = jnp.where(in_win[None, None, None], t, MASK_VALUE)
    t = t.transpose(1, 0, 3, 2, 4)
    return t.reshape(NA_WIN_H, NA_HEADS, GRID_W, kh * GRID_W).astype(F32)


def _ctx_na_kernel(q_ref, k_ref, v_ref, o_ref):
    tq = q_ref.shape[1]
    lane = lax.broadcasted_iota(jnp.int32, (tq, LANES), 1)
    for pr in range(N_PAIRS):
        cols = slice(pr * LANES, (pr + 1) * LANES)
        q2 = q_ref[0, :, cols]
        kb = k_ref[0, :, cols]
        vb = v_ref[0, :, cols]
        acc = jnp.zeros((tq, LANES), F32)
        for hh in range(2):
            in_head = (lane >= hh * NA_HEAD_DIM) & (lane < (hh + 1) * NA_HEAD_DIM)
            qm = jnp.where(in_head, q2, jnp.zeros_like(q2))
            s = _dot_nt(qm, kb)
            p = jnp.exp(s - jnp.max(s, axis=-1, keepdims=True))
            l = jnp.sum(p, axis=-1, keepdims=True)
            acc = jnp.where(in_head, _dot(p.astype(BF16), vb) / l, acc)
        o_ref[0, :, cols] = acc.astype(BF16)


def _ctx_na_attention(q, k, v):
    b, n, _ = q.shape
    spec = pl.BlockSpec((1, n, HEAD_W), lambda bi: (bi, 0, 0))
    return pl.pallas_call(
        _ctx_na_kernel,
        grid=(b,),
        in_specs=[spec, spec, spec],
        out_specs=spec,
        out_shape=jax.ShapeDtypeStruct((b, n, HEAD_W), BF16),
        compiler_params=_cparams(("parallel",)),
        name="ctx_na_attention",
    )(q, k, v)


def _diff_kernel(q_ref, k_ref, v_ref, lam_ref, g_ref, o_ref, *, lam_init):
    tq = q_ref.shape[1]
    lp = lam_ref[...]
    lam = (jnp.exp(jnp.sum(lp[0:1] * lp[1:2], axis=-1, keepdims=True))
           - jnp.exp(jnp.sum(lp[2:3] * lp[3:4], axis=-1, keepdims=True)) + lam_init)
    lane = lax.broadcasted_iota(jnp.int32, (tq, LANES), 1)
    for pr in range(N_PAIRS):
        q2 = q_ref[0, :, pr * LANES:(pr + 1) * LANES]
        kk = k_ref[0, :, pr * LANES:(pr + 1) * LANES]
        va = v_ref[0, :, pr * 2 * LANES:(pr + 1) * 2 * LANES]
        outp = jnp.zeros((tq, LANES), F32)
        for hh in range(2):
            comp = []
            for c in range(2):
                lo = hh * DF_V_DIM + c * DF_QK_DIM
                qm = jnp.where((lane >= lo) & (lane < lo + DF_QK_DIM), q2, jnp.zeros_like(q2))
                s = _dot_nt(qm, kk)
                p = jnp.exp(s - jnp.max(s, axis=-1, keepdims=True)).astype(BF16)
                res = _dot(p, va)
                comp.append(res[:, :LANES] / res[:, LANES:LANES + 1])
            in_head = (lane >= hh * DF_V_DIM) & (lane < (hh + 1) * DF_V_DIM)
            oh = jnp.where(in_head, comp[0] - lam * comp[1], 0.0)
            ms = jnp.sum(oh * oh, axis=-1, keepdims=True) * (1.0 / DF_V_DIM)
            outp = outp + oh * lax.rsqrt(ms + EPS) * g_ref[...] * (1.0 - lam_init)
        o_ref[0, :, pr * LANES:(pr + 1) * LANES] = outp.astype(BF16)


def _diff_attention(dq, dk, dv_aug, lam_p, subln_g, lam_init):
    b, nq, _ = dq.shape
    nk = dk.shape[1]
    tq = min(256, nq)
    return pl.pallas_call(
        functools.partial(_diff_kernel, lam_init=lam_init),
        grid=(b, nq // tq),
        in_specs=[
            pl.BlockSpec((1, tq, HEAD_W), lambda bi, i: (bi, i, 0)),
            pl.BlockSpec((1, nk, HEAD_W), lambda bi, i: (bi, 0, 0)),
            pl.BlockSpec((1, nk, 2 * HEAD_W), lambda bi, i: (bi, 0, 0)),
            pl.BlockSpec(lam_p.shape, lambda bi, i: (0, 0)),
            pl.BlockSpec(subln_g.shape, lambda bi, i: (0, 0)),
        ],
        out_specs=pl.BlockSpec((1, tq, HEAD_W), lambda bi, i: (bi, i, 0)),
        out_shape=jax.ShapeDtypeStruct((b, nq, HEAD_W), BF16),
        compiler_params=_cparams(("parallel", "arbitrary")),
        name="diff_attention",
    )(dq, dk, dv_aug, lam_p, subln_g)


def _augment_v(dv):
    b, n, _ = dv.shape
    v3 = dv.reshape(b, n, N_PAIRS, LANES)
    ones = jnp.ones_like(v3)
    return jnp.concatenate([v3, ones], axis=-1).reshape(b, n, 2 * HEAD_W)


def _merge_kernel(x_ref, f_ref, na_ref, df_ref, gate_ref, g1_ref, wft_ref, wna_ref, wdf_ref, wout_ref,
                  ng_ref, sh_ref, sc_ref, wrh_ref, wrl_ref, xo_ref, h2_ref, lg_ref):
    d = x_ref.shape[2]
    y_ft = _dot(f_ref[0], wft_ref[...])
    y_na = _dot(na_ref[0], wna_ref[...])
    y_df = _dot(df_ref[0], wdf_ref[...])
    m = (gate_ref[0, :, 0:d].astype(F32) * y_ft + gate_ref[0, :, d:2 * d].astype(F32) * y_na
         + gate_ref[0, :, 2 * d:3 * d].astype(F32) * y_df)
    y = _dot(m.astype(BF16), wout_ref[...])
    xn = x_ref[0] + g1_ref[0] * y
    xo_ref[0] = xn
    ms = jnp.mean(xn * xn, axis=-1, keepdims=True)
    h2 = (xn * lax.rsqrt(ms + EPS) * ng_ref[...]) * (1.0 + sc_ref[0]) + sh_ref[0]
    h_hi, h_lo = _split_bf16(h2)
    h2_ref[0] = h_hi
    lg_ref[0] = _dot(h_hi, wrh_ref[...]) + _dot(h_lo, wrh_ref[...]) + _dot(h_hi, wrl_ref[...])


def _merge(x, f, o_na, o_df, gates, g1, w_ft, w_na_o, w_df_o, w_out, norm_g, sh2, sc2, wr_hi, wr_lo):
    b, n, d = x.shape
    tm = min(256, n)
    tok = lambda w: pl.BlockSpec((1, tm, w), lambda bi, i: (bi, i, 0))
    full = lambda a: pl.BlockSpec(a.shape, lambda bi, i: (0,) * a.ndim)
    mod = pl.BlockSpec((1, 1, d), lambda bi, i: (bi, 0, 0))
    return pl.pallas_call(
        _merge_kernel,
        grid=(b, n // tm),
        in_specs=[tok(d), tok(FT_WIDTH), tok(HEAD_W), tok(HEAD_W), tok(N_BRANCHES * d), mod,
                  full(w_ft), full(w_na_o), full(w_df_o), full(w_out), full(norm_g), mod, mod,
                  full(wr_hi), full(wr_lo)],
        out_specs=[tok(d), tok(d), tok(ROUTER_PAD)],
        out_shape=[jax.ShapeDtypeStruct((b, n, d), F32), jax.ShapeDtypeStruct((b, n, d), BF16),
                   jax.ShapeDtypeStruct((b, n, ROUTER_PAD), F32)],
        compiler_params=_cparams(("parallel", "parallel")),
        name="merge",
    )(x, f, o_na, o_df, gates, g1, w_ft, w_na_o, w_df_o, w_out, norm_g, sh2, sc2, wr_hi, wr_lo)


def _expert_kernel(x_ref, wg_ref, wu_ref, wd_ref, g_ref, o_ref, *, f_chunk):
    x = x_ref[0, 0]
    ff = wg_ref.shape[2]
    acc = jnp.zeros(o_ref.shape[2:], F32)
    for f0 in range(0, ff, f_chunk):
        a = _dot(x, wg_ref[0, :, f0:f0 + f_chunk])
        u = _dot(x, wu_ref[0, :, f0:f0 + f_chunk])
        hm = (a * jax.nn.sigmoid(a) * u).astype(BF16)
        acc = acc + _dot(hm, wd_ref[0, f0:f0 + f_chunk, :])
    o_ref[0, 0] = acc * g_ref[0, 0]


def _experts(xe, w_gate, w_up, w_down, g):
    b, e, cap, d = xe.shape
    ff = w_gate.shape[2]
    return pl.pallas_call(
        functools.partial(_expert_kernel, f_chunk=min(512, ff)),
        grid=(e, b),
        in_specs=[
            pl.BlockSpec((1, 1, cap, d), lambda ei, bi: (bi, ei, 0, 0)),
            pl.BlockSpec((1, d, ff), lambda ei, bi: (ei, 0, 0)),
            pl.BlockSpec((1, d, ff), lambda ei, bi: (ei, 0, 0)),
            pl.BlockSpec((1, ff, d), lambda ei, bi: (ei, 0, 0)),
            pl.BlockSpec((1, 1, cap, 1), lambda ei, bi: (bi, ei, 0, 0)),
        ],
        out_specs=pl.BlockSpec((1, 1, cap, d), lambda ei, bi: (bi, ei, 0, 0)),
        out_shape=jax.ShapeDtypeStruct((b, e, cap, d), F32),
        compiler_params=_cparams(("arbitrary", "arbitrary")),
        name="experts",
    )(xe, w_gate, w_up, w_down, g[..., None])


def _expert_choice_ffn(h2, logits, w_gate, w_up, w_down):
    b, n, d = h2.shape
    cap = EC_CAPACITY_FACTOR * n // N_EXPERTS
    aff = jax.nn.softmax(logits[..., :N_EXPERTS], axis=-1).transpose(0, 2, 1)
    g, idx = lax.top_k(aff, cap)
    xe = jax.vmap(lambda hb, ib: hb[ib])(h2, idx)
    ye = _experts(xe, w_gate, w_up, w_down, g)
    return jax.vmap(lambda yb, ib: jnp.zeros((n, d), yb.dtype).at[ib.reshape(-1)].add(yb.reshape(-1, d)))(ye, idx)


def _dft_mats(n):
    k = jnp.arange(n, dtype=jnp.int32)
    ang = ((k[:, None] * k[None, :]) % n).astype(F32) * (2.0 * math.pi / n)
    return jnp.cos(ang), jnp.sin(ang)


def _chan_dft():
    c, s = _dft_mats(FT_GROUP_DIM)
    eye = jnp.eye(FT_GROUPS, dtype=F32)
    return jnp.concatenate([jnp.kron(eye, c), jnp.kron(eye, s)], axis=1).astype(BF16)


def _group_mean_mat(group):
    gid = jnp.arange(HEAD_W) // group
    return jnp.where(gid[:, None] == gid[None, :], 1.0 / group, 0.0).astype(BF16)


def _rope_tables(n):
    t = jnp.arange(n)
    row = (t // GRID_W).astype(F32)
    col = (t % GRID_W).astype(F32)
    ax = DF_QK_DIM // 2
    inv = ROPE_BASE ** (-jnp.arange(0, ax, 2, dtype=F32) / ax)
    lane = jnp.arange(LANES)
    freq = inv[lane % (ax // 2)]
    pos = jnp.where(((lane % DF_QK_DIM) < ax)[None, :], row[:, None], col[:, None])
    ang = pos * freq[None, :]
    sign = jnp.where((lane % ax) < ax // 2, -1.0, 1.0)
    return jnp.cos(ang), jnp.sin(ang) * sign[None, :]


def kernel(x, c, ctx, c_ctx, norm1_g, norm2_g, w_ada, b_ada, w_in, na_qn_g, na_kn_g, na_rpb, df_qn_g, df_kn_g,
           df_lambda, df_subln_g, w_ft, w_na_o, w_df_o, w_out, w_router, w_gate, w_up, w_down):
    b, n, d = x.shape
    lc = ctx.shape[1]
    depth = w_ada.shape[0]
    assert b + 1 <= MOD_ROWS and n % GRID_W == 0

    cc = jnp.zeros((MOD_ROWS, d), F32).at[:b].set(c).at[b].set(c_ctx)
    mods = _ada(cc, w_ada, b_ada)

    consts = {"chan_dft": _chan_dft(), "g64": _group_mean_mat(NA_HEAD_DIM), "g32": _group_mean_mat(DF_QK_DIM)}
    rope_lat = _rope_tables(n)
    rope_ctx = (jnp.zeros((lc, LANES), F32), jnp.zeros((lc, LANES), F32))
    cos_n, sin_n = _dft_mats(n)
    dft_lat = (cos_n.astype(BF16), (-sin_n).astype(BF16))
    cos_c, sin_c = _dft_mats(lc)
    dft_ctx = (cos_c.astype(BF16), (-sin_c).astype(BF16))
    rows = n // GRID_W

    xc = ctx
    for i in range(depth):
        last = i == depth - 1
        lam_init = 0.8 - 0.6 * math.exp(-0.3 * i)
        m_lat = mods[i, :b].reshape(b, 6, 1, d)
        m_ctx = jnp.broadcast_to(mods[i, b].reshape(1, 6, 1, d), (b, 6, 1, d))
        sh1, sc1, g1, sh2, sc2, g2 = [m_lat[:, j] for j in range(6)]
        csh1, csc1, cg1, csh2, csc2, cg2 = [m_ctx[:, j] for j in range(6)]

        w_in_b = w_in[i].astype(BF16)
        gains = {
            "naq": jnp.tile(na_qn_g[i], NA_HEADS).reshape(1, HEAD_W),
            "nak": jnp.tile(na_kn_g[i], NA_HEADS).reshape(1, HEAD_W),
            "dfq": jnp.tile(df_qn_g[i], 2 * DF_HEADS).reshape(1, HEAD_W),
            "dfk": jnp.tile(df_kn_g[i], 2 * DF_HEADS).reshape(1, HEAD_W),
        }
        n1g = norm1_g[i].reshape(1, d)
        n2g = norm2_g[i].reshape(1, d)
        subln = jnp.tile(df_subln_g[i], 2).reshape(1, LANES)
        w_ft_b, w_na_b, w_df_b, w_out_b = (w.astype(BF16) for w in (w_ft[i], w_na_o[i], w_df_o[i], w_out[i]))
        wr = jnp.zeros((d, ROUTER_PAD), F32).at[:, :N_EXPERTS].set(w_router[i])
        wr_hi, wr_lo = _split_bf16(wr)
        wg_b, wu_b, wd_b = w_gate[i].astype(BF16), w_up[i].astype(BF16), w_down[i].astype(BF16)

        if last:
            nkc, nvc, dkc, dvc = _inproj(xc, csh1, csc1, n1g, w_in_b, consts, gains, rope_ctx,
                                         rope=False, kv_only=True)
        else:
            fabc, nqc, dqc, nkc, nvc, dkc, dvc, gatec = _inproj(xc, csh1, csc1, n1g, w_in_b, consts, gains,
                                                                 rope_ctx, rope=False, kv_only=False)
        dvc_aug = _augment_v(dvc)

        fab, nq, dq, nk, nv, dk, dv, gate = _inproj(x, sh1, sc1, n1g, w_in_b, consts, gains, rope_lat,
                                                     rope=True, kv_only=False)
        f = _fourier(fab, *dft_lat)
        o_na = _na_attention(nq, nk, nv, nkc, nvc, _na_bias_table(na_rpb[i], rows))
        o_df = _diff_attention(dq, jnp.concatenate([dk, dkc], axis=1),
                               jnp.concatenate([_augment_v(dv), dvc_aug], axis=1),
                               df_lambda[i], subln, lam_init)
        x, h2, logits = _merge(x, f, o_na, o_df, gate, g1, w_ft_b, w_na_b, w_df_b, w_out_b, n2g, sh2, sc2,
                               wr_hi, wr_lo)
        x = x + g2 * _expert_choice_ffn(h2, logits, wg_b, wu_b, wd_b)

        if not last:
            fc = _fourier(fabc, *dft_ctx)
            o_nac = _ctx_na_attention(nqc, nkc, nvc)
            o_dfc = _diff_attention(dqc, dkc, dvc_aug, df_lambda[i], subln, lam_init)
            xc, hc2, logits_c = _merge(xc, fc, o_nac, o_dfc, gatec, cg1, w_ft_b, w_na_b, w_df_b, w_out_b, n2g,
                                       csh2, csc2, wr_hi, wr_lo)
            xc = xc + cg2 * _expert_choice_ffn(hc2, logits_c, wg_b, wu_b, wd_b)
    return x
```

```python
import functools
import math

import jax
import jax.numpy as jnp
from jax import lax
from jax.experimental import pallas as pl
from jax.experimental.pallas import tpu as pltpu

F32 = jnp.float32
BF16 = jnp.bfloat16

GRID_W = 64
FT_GROUPS = 4
FT_GROUP_DIM = 64
FT_WIDTH = FT_GROUPS * FT_GROUP_DIM
NA_HEADS = 6
NA_HEAD_DIM = 64
NA_WIDTH = NA_HEADS * NA_HEAD_DIM
NA_WIN_H = 8
NA_WIN_W = 16
DF_HEADS = 6
DF_QK_DIM = 32
DF_V_DIM = 2 * DF_QK_DIM
DF_QK_WIDTH = DF_HEADS * 2 * DF_QK_DIM
DF_WIDTH = DF_HEADS * DF_V_DIM
N_BRANCHES = 3
N_EXPERTS = 16
EC_CAPACITY_FACTOR = 2
ROPE_BASE = 10000.0
EPS = 1e-6
MASK_VALUE = -1e30

LANES = 128
VMEM_LIMIT_BYTES = 56 * 1024 * 1024

HEAD_W = 384
N_PAIRS = HEAD_W // LANES
MOD_ROWS = 16
ROUTER_PAD = LANES


def _cparams(sem):
    return pltpu.CompilerParams(dimension_semantics=sem, vmem_limit_bytes=VMEM_LIMIT_BYTES)


def _dot(a, b):
    return jnp.dot(a, b, preferred_element_type=F32)


def _dot_nt(a, b):
    return lax.dot_general(a, b, (((1,), (1,)), ((), ())), preferred_element_type=F32)


def _split_bf16(v):
    hi = v.astype(BF16)
    lo = (v - hi.astype(F32)).astype(BF16)
    return hi, lo


def _ada_kernel(c_ref, w_ref, b_ref, o_ref):
    c = c_ref[...]
    a = c * jax.nn.sigmoid(c)
    a_hi, a_lo = _split_bf16(a)
    w_hi, w_lo = _split_bf16(w_ref[0])
    acc = _dot(a_hi, w_hi) + _dot(a_lo, w_hi) + _dot(a_hi, w_lo)
    o_ref[0] = acc + b_ref[0]


def _ada(cc, w_ada, b_ada):
    depth, d, d6 = w_ada.shape
    tn = 512
    return pl.pallas_call(
        _ada_kernel,
        grid=(depth, d6 // tn),
        in_specs=[
            pl.BlockSpec((MOD_ROWS, d), lambda l, j: (0, 0)),
            pl.BlockSpec((1, d, tn), lambda l, j: (l, 0, j)),
            pl.BlockSpec((1, 1, tn), lambda l, j: (l, 0, j)),
        ],
        out_specs=pl.BlockSpec((1, MOD_ROWS, tn), lambda l, j: (l, 0, j)),
        out_shape=jax.ShapeDtypeStruct((depth, MOD_ROWS, d6), F32),
        compiler_params=_cparams(("arbitrary", "arbitrary")),
        name="ada",
    )(cc, w_ada, b_ada.reshape(depth, 1, d6))


OFF_Q = FT_WIDTH
OFF_DQ = OFF_Q + NA_WIDTH
OFF_KV = OFF_DQ + DF_QK_WIDTH
OFF_NV = OFF_KV + NA_WIDTH
OFF_DK = OFF_NV + NA_WIDTH
OFF_DV = OFF_DK + DF_QK_WIDTH
OFF_GATE = OFF_DV + DF_WIDTH


def _group_rms(v, gmat_ref, gain_ref):
    hi, lo = _split_bf16(v * v)
    ms = _dot(hi, gmat_ref[...]) + _dot(lo, gmat_ref[...])
    return v * lax.rsqrt(ms + EPS) * gain_ref[...]


def _rope_chunk(vj, cos, sin_signed, first_half):
    fwd = pltpu.roll(vj, LANES - 8, 1)
    bwd = pltpu.roll(vj, 8, 1)
    partner = jnp.where(first_half, fwd, bwd)
    return vj * cos + partner * sin_signed


def _inproj_kernel(x_ref, sh_ref, sc_ref, ng_ref, w_ref, cd_ref, g64_ref, g32_ref,
                   naq_g_ref, nak_g_ref, dfq_g_ref, dfk_g_ref, cos_ref, sin_ref,
                   *out_refs, rope, kv_only):
    x = x_ref[0]
    ms = jnp.mean(x * x, axis=-1, keepdims=True)
    y = x * lax.rsqrt(ms + EPS) * ng_ref[...]
    h = (y * (1.0 + sc_ref[0]) + sh_ref[0]).astype(BF16)

    def proj(c0, c1):
        return _dot(h, w_ref[:, c0:c1])

    if rope:
        lane = lax.broadcasted_iota(jnp.int32, (x.shape[0], LANES), 1)
        first_half = (lane & 8) == 0
        cos = cos_ref[...]
        sin_signed = sin_ref[...]

    def df_qk(v, gain_ref, scale, o_ref):
        v = _group_rms(v, g32_ref, gain_ref)
        for j in range(N_PAIRS):
            vj = v[:, j * LANES:(j + 1) * LANES]
            if rope:
                vj = _rope_chunk(vj, cos, sin_signed, first_half)
            o_ref[0, :, j * LANES:(j + 1) * LANES] = (vj * scale).astype(BF16)

    if kv_only:
        nk_ref, nv_ref, dk_ref, dv_ref = out_refs
    else:
        fab_ref, nq_ref, dq_ref, nk_ref, nv_ref, dk_ref, dv_ref, gate_ref = out_refs
        u = proj(0, FT_WIDTH)
        fab_ref[0] = _dot(u.astype(BF16), cd_ref[...]).astype(BF16)
        nq = _group_rms(proj(OFF_Q, OFF_DQ), g64_ref, naq_g_ref)
        nq_ref[0] = (nq * (NA_HEAD_DIM ** -0.5)).astype(BF16)
        df_qk(proj(OFF_DQ, OFF_KV), dfq_g_ref, DF_QK_DIM ** -0.5 * math.log2(math.e), dq_ref)
        d = x.shape[1]
        for j in range(N_BRANCHES):
            z = proj(OFF_GATE + j * d, OFF_GATE + (j + 1) * d)
            gate_ref[0, :, j * d:(j + 1) * d] = jax.nn.sigmoid(z).astype(BF16)

    nk_ref[0] = _group_rms(proj(OFF_KV, OFF_NV), g64_ref, nak_g_ref).astype(BF16)
    nv_ref[0] = proj(OFF_NV, OFF_DK).astype(BF16)
    df_qk(proj(OFF_DK, OFF_DV), dfk_g_ref, 1.0, dk_ref)
    dv_ref[0] = proj(OFF_DV, OFF_GATE).astype(BF16)


def _inproj(x, sh, sc, norm_g, w_in, consts, gains, rope_tabs, *, rope, kv_only):
    b, n, d = x.shape
    tm = min(512, n)
    tok = lambda w: pl.BlockSpec((1, tm, w), lambda bi, i: (bi, i, 0))
    full = lambda a: pl.BlockSpec(a.shape, lambda bi, i: (0,) * a.ndim)
    mod = pl.BlockSpec((1, 1, d), lambda bi, i: (bi, 0, 0))
    cos_t, sin_t = rope_tabs
    tab = pl.BlockSpec((tm, LANES), lambda bi, i: (i, 0))
    slab = jax.ShapeDtypeStruct((b, n, HEAD_W), BF16)
    if kv_only:
        out_shape = [slab] * 4
        out_specs = [tok(HEAD_W)] * 4
    else:
        out_shape = [jax.ShapeDtypeStruct((b, n, 2 * FT_WIDTH), BF16)] + [slab] * 6 + [
            jax.ShapeDtypeStruct((b, n, N_BRANCHES * d), BF16)]
        out_specs = [tok(2 * FT_WIDTH)] + [tok(HEAD_W)] * 6 + [tok(N_BRANCHES * d)]
    args = [x, sh, sc, norm_g, w_in, consts["chan_dft"], consts["g64"], consts["g32"],
            gains["naq"], gains["nak"], gains["dfq"], gains["dfk"], cos_t, sin_t]
    in_specs = [tok(d), mod, mod, full(norm_g), full(w_in), full(consts["chan_dft"]),
                full(consts["g64"]), full(consts["g32"]), full(gains["naq"]), full(gains["nak"]),
                full(gains["dfq"]), full(gains["dfk"]), tab, tab]
    return pl.pallas_call(
        functools.partial(_inproj_kernel, rope=rope, kv_only=kv_only),
        grid=(b, n // tm),
        in_specs=in_specs,
        out_specs=out_specs,
        out_shape=out_shape,
        compiler_params=_cparams(("parallel", "parallel")),
        name="inproj_kv" if kv_only else "inproj",
    )(*args)


def _fourier_kernel(c_ref, s_ref, ab_ref, o_ref, acc_ref, *, scale):
    k = pl.program_id(1)

    @pl.when(k == 0)
    def _():
        acc_ref[...] = jnp.zeros_like(acc_ref)

    cm = c_ref[...]
    sm = s_ref[...]
    for bi in range(ab_ref.shape[0]):
        acc_ref[bi] += _dot(cm, ab_ref[bi, :, :FT_WIDTH]) + _dot(sm, ab_ref[bi, :, FT_WIDTH:])

    @pl.when(k == pl.num_programs(1) - 1)
    def _():
        o_ref[...] = (acc_ref[...] * scale).astype(o_ref.dtype)


def _fourier(fab, cos_m, nsin_m):
    b, n, _ = fab.shape
    tn = min(1024, n)
    tk = min(512, n)
    scale = 1.0 / math.sqrt(n * FT_GROUP_DIM)
    return pl.pallas_call(
        functools.partial(_fourier_kernel, scale=scale),
        grid=(n // tn, n // tk),
        in_specs=[
            pl.BlockSpec((tn, tk), lambda i, k: (i, k)),
            pl.BlockSpec((tn, tk), lambda i, k: (i, k)),
            pl.BlockSpec((b, tk, 2 * FT_WIDTH), lambda i, k: (0, k, 0)),
        ],
        out_specs=pl.BlockSpec((b, tn, FT_WIDTH), lambda i, k: (0, i, 0)),
        out_shape=jax.ShapeDtypeStruct((b, n, FT_WIDTH), BF16),
        scratch_shapes=[pltpu.VMEM((b, tn, FT_WIDTH), F32)],
        compiler_params=_cparams(("parallel", "arbitrary")),
        name="fourier",
    )(cos_m, nsin_m, fab)


NA_ROW_GROUP = 4


def _na_kernel(pid_ref, q_ref, k_ref, v_ref, kc_ref, vc_ref, bias_ref, o_ref, *, rows, kh):
    del pid_ref
    g = pl.program_id(1)
    rq = NA_ROW_GROUP * GRID_W
    key_rows = NA_ROW_GROUP + kh - 1
    u = jnp.clip(NA_ROW_GROUP * g - kh // 2, 0, rows - key_rows)
    start = pl.multiple_of(u * GRID_W, GRID_W)
    lane = lax.broadcasted_iota(jnp.int32, (rq, LANES), 1)
    first = lane < NA_HEAD_DIM
    for pr in range(N_PAIRS):
        cols = slice(pr * LANES, (pr + 1) * LANES)
        q2 = q_ref[0, :, cols]
        zero = jnp.zeros_like(q2)
        qcat = jnp.concatenate([jnp.where(first, q2, zero), jnp.where(first, zero, q2)], axis=0)
        kb = k_ref[0, pl.ds(start, key_rows * GRID_W), cols]
        vb = v_ref[0, pl.ds(start, key_rows * GRID_W), cols]
        s1 = _dot_nt(qcat, kb) + bias_ref[0, pr]
        s2 = _dot_nt(qcat, kc_ref[0, :, cols])
        m = jnp.maximum(jnp.max(s1, axis=-1, keepdims=True), jnp.max(s2, axis=-1, keepdims=True))
        p1 = jnp.exp(s1 - m)
        p2 = jnp.exp(s2 - m)
        l = jnp.sum(p1, axis=-1, keepdims=True) + jnp.sum(p2, axis=-1, keepdims=True)
        o = (_dot(p1.astype(BF16), vb) + _dot(p2.astype(BF16), vc_ref[0, :, cols])) / l
        o_ref[0, :, cols] = jnp.where(first, o[:rq], o[rq:]).astype(BF16)


def _na_attention(nq, nk, nv, nkc, nvc, bias, pattern_ids):
    b, n, _ = nq.shape
    lc = nkc.shape[1]
    rows = n // GRID_W
    kh = min(NA_WIN_H, rows)
    rq = NA_ROW_GROUP * GRID_W
    grid_spec = pltpu.PrefetchScalarGridSpec(
        num_scalar_prefetch=1,
        grid=(b, rows // NA_ROW_GROUP),
        in_specs=[
            pl.BlockSpec((1, rq, HEAD_W), lambda bi, g, pid: (bi, g, 0)),
            pl.BlockSpec((1, n, HEAD_W), lambda bi, g, pid: (bi, 0, 0)),
            pl.BlockSpec((1, n, HEAD_W), lambda bi, g, pid: (bi, 0, 0)),
            pl.BlockSpec((1, lc, HEAD_W), lambda bi, g, pid: (bi, 0, 0)),
            pl.BlockSpec((1, lc, HEAD_W), lambda bi, g, pid: (bi, 0, 0)),
            pl.BlockSpec((1,) + bias.shape[1:], lambda bi, g, pid: (pid[g], 0, 0, 0)),
        ],
        out_specs=pl.BlockSpec((1, rq, HEAD_W), lambda bi, g, pid: (bi, g, 0)),
    )
    return pl.pallas_call(
        functools.partial(_na_kernel, rows=rows, kh=kh),
        grid_spec=grid_spec,
        out_shape=jax.ShapeDtypeStruct((b, n, HEAD_W), BF16),
        compiler_params=_cparams(("parallel", "arbitrary")),
        name="na_attention",
    )(pattern_ids, nq, nk, nv, nkc, nvc, bias)


def _na_patterns(rows):
    kh = min(NA_WIN_H, rows)
    key_rows = NA_ROW_GROUP + kh - 1
    assert rows % NA_ROW_GROUP == 0 and rows >= key_rows
    patterns, ids = [], []
    for g in range(rows // NA_ROW_GROUP):
        u = min(max(NA_ROW_GROUP * g - kh // 2, 0), rows - key_rows)
        geo = []
        for r in range(NA_ROW_GROUP * g, NA_ROW_GROUP * (g + 1)):
            rs = min(max(r - kh // 2, 0), rows - kh)
            geo.append((r - u, rs - u))
        geo = tuple(geo)
        if geo not in patterns:
            patterns.append(geo)
        ids.append(patterns.index(geo))
    return patterns, ids


def _na_bias_table(rpb, rows):
    kh = min(NA_WIN_H, rows)
    key_rows = NA_ROW_GROUP + kh - 1
    patterns, ids = _na_patterns(rows)
    geo = jnp.asarray(patterns, dtype=jnp.int32)
    rq_off, win_off = geo[..., 0], geo[..., 1]
    a = jnp.arange(key_rows)
    row_ok = (a >= win_off[..., None]) & (a < win_off[..., None] + kh)
    row_idx = jnp.clip(a - rq_off[..., None] + NA_WIN_H - 1, 0, 2 * NA_WIN_H - 2)
    cols = jnp.arange(GRID_W)
    col_start = jnp.clip(cols - NA_WIN_W // 2, 0, GRID_W - NA_WIN_W)
    kc = jnp.arange(GRID_W)
    col_ok = (kc[None, :] >= col_start[:, None]) & (kc[None, :] < col_start[:, None] + NA_WIN_W)
    col_idx = jnp.clip(kc[None, :] - cols[:, None] + NA_WIN_W - 1, 0, 2 * NA_WIN_W - 2)
    t = rpb[:, row_idx]
    t = t[..., col_idx]
    ok = row_ok[None, :, :, :, None, None] & col_ok[None, None, None, None]
    t = jnp.where(ok, t, MASK_VALUE)
    t = t.transpose(1, 0, 2, 4, 3, 5)
    n_pat = len(patterns)
    t = t.reshape(n_pat, N_PAIRS, 2 * NA_ROW_GROUP * GRID_W, key_rows * GRID_W)
    return t.astype(F32), jnp.asarray(ids, dtype=jnp.int32)


def _ctx_na_kernel(q_ref, k_ref, v_ref, o_ref):
    tq = q_ref.shape[1]
    lane = lax.broadcasted_iota(jnp.int32, (tq, LANES), 1)
    for pr in range(N_PAIRS):
        cols = slice(pr * LANES, (pr + 1) * LANES)
        q2 = q_ref[0, :, cols]
        kb = k_ref[0, :, cols]
        vb = v_ref[0, :, cols]
        acc = jnp.zeros((tq, LANES), F32)
        for hh in range(2):
            in_head = (lane >= hh * NA_HEAD_DIM) & (lane < (hh + 1) * NA_HEAD_DIM)
            qm = jnp.where(in_head, q2, jnp.zeros_like(q2))
            s = _dot_nt(qm, kb)
            p = jnp.exp(s - jnp.max(s, axis=-1, keepdims=True))
            l = jnp.sum(p, axis=-1, keepdims=True)
            acc = jnp.where(in_head, _dot(p.astype(BF16), vb) / l, acc)
        o_ref[0, :, cols] = acc.astype(BF16)


def _ctx_na_attention(q, k, v):
    b, n, _ = q.shape
    spec = pl.BlockSpec((1, n, HEAD_W), lambda bi: (bi, 0, 0))
    return pl.pallas_call(
        _ctx_na_kernel,
        grid=(b,),
        in_specs=[spec, spec, spec],
        out_specs=spec,
        out_shape=jax.ShapeDtypeStruct((b, n, HEAD_W), BF16),
        compiler_params=_cparams(("parallel",)),
        name="ctx_na_attention",
    )(q, k, v)


def _diff_kernel(q_ref, k_ref, v_ref, lam_ref, g_ref, o_ref, *, lam_init):
    tq = q_ref.shape[1]
    lp = lam_ref[...]
    lam = (jnp.exp(jnp.sum(lp[0:1] * lp[1:2], axis=-1, keepdims=True))
           - jnp.exp(jnp.sum(lp[2:3] * lp[3:4], axis=-1, keepdims=True)) + lam_init)
    lane = lax.broadcasted_iota(jnp.int32, (tq, LANES), 1)
    for pr in range(N_PAIRS):
        q2 = q_ref[0, :, pr * LANES:(pr + 1) * LANES]
        kk = k_ref[0, :, pr * LANES:(pr + 1) * LANES]
        va = v_ref[0, :, pr * 2 * LANES:(pr + 1) * 2 * LANES]
        zero = jnp.zeros_like(q2)
        qcat = jnp.concatenate(
            [jnp.where((lane >= j * DF_QK_DIM) & (lane < (j + 1) * DF_QK_DIM), q2, zero) for j in range(4)], axis=0)
        s = _dot_nt(qcat, kk)
        p = jnp.exp2(s - jnp.max(s, axis=-1, keepdims=True)).astype(BF16)
        res = _dot(p, va)
        o = res[:, :LANES] / res[:, LANES:LANES + 1]
        outp = jnp.zeros((tq, LANES), F32)
        for hh in range(2):
            in_head = (lane >= hh * DF_V_DIM) & (lane < (hh + 1) * DF_V_DIM)
            oh = o[(2 * hh) * tq:(2 * hh + 1) * tq] - lam * o[(2 * hh + 1) * tq:(2 * hh + 2) * tq]
            oh = jnp.where(in_head, oh, 0.0)
            ms = jnp.sum(oh * oh, axis=-1, keepdims=True) * (1.0 / DF_V_DIM)
            outp = outp + oh * lax.rsqrt(ms + EPS) * g_ref[...] * (1.0 - lam_init)
        o_ref[0, :, pr * LANES:(pr + 1) * LANES] = outp.astype(BF16)


def _diff_attention(dq, dk, dv_aug, lam_p, subln_g, lam_init):
    b, nq, _ = dq.shape
    nk = dk.shape[1]
    tq = min(128, nq)
    return pl.pallas_call(
        functools.partial(_diff_kernel, lam_init=lam_init),
        grid=(b, nq // tq),
        in_specs=[
            pl.BlockSpec((1, tq, HEAD_W), lambda bi, i: (bi, i, 0)),
            pl.BlockSpec((1, nk, HEAD_W), lambda bi, i: (bi, 0, 0)),
            pl.BlockSpec((1, nk, 2 * HEAD_W), lambda bi, i: (bi, 0, 0)),
            pl.BlockSpec(lam_p.shape, lambda bi, i: (0, 0)),
            pl.BlockSpec(subln_g.shape, lambda bi, i: (0, 0)),
        ],
        out_specs=pl.BlockSpec((1, tq, HEAD_W), lambda bi, i: (bi, i, 0)),
        out_shape=jax.ShapeDtypeStruct((b, nq, HEAD_W), BF16),
        compiler_params=_cparams(("parallel", "arbitrary")),
        name="diff_attention",
    )(dq, dk, dv_aug, lam_p, subln_g)


def _augment_v(dv):
    b, n, _ = dv.shape
    v3 = dv.reshape(b, n, N_PAIRS, LANES)
    ones = jnp.ones_like(v3)
    return jnp.concatenate([v3, ones], axis=-1).reshape(b, n, 2 * HEAD_W)


def _merge_kernel(x_ref, f_ref, na_ref, df_ref, gate_ref, g1_ref, wft_ref, wna_ref, wdf_ref, wout_ref,
                  ng_ref, sh_ref, sc_ref, wrh_ref, wrl_ref, xo_ref, h2_ref, lg_ref):
    d = x_ref.shape[2]
    y_ft = _dot(f_ref[0], wft_ref[...])
    y_na = _dot(na_ref[0], wna_ref[...])
    y_df = _dot(df_ref[0], wdf_ref[...])
    m = (gate_ref[0, :, 0:d].astype(F32) * y_ft + gate_ref[0, :, d:2 * d].astype(F32) * y_na
         + gate_ref[0, :, 2 * d:3 * d].astype(F32) * y_df)
    y = _dot(m.astype(BF16), wout_ref[...])
    xn = x_ref[0] + g1_ref[0] * y
    xo_ref[0] = xn
    ms = jnp.mean(xn * xn, axis=-1, keepdims=True)
    h2 = (xn * lax.rsqrt(ms + EPS) * ng_ref[...]) * (1.0 + sc_ref[0]) + sh_ref[0]
    h_hi, h_lo = _split_bf16(h2)
    h2_ref[0] = h_hi
    lg_ref[0] = _dot(h_hi, wrh_ref[...]) + _dot(h_lo, wrh_ref[...]) + _dot(h_hi, wrl_ref[...])


def _merge(x, f, o_na, o_df, gates, g1, w_ft, w_na_o, w_df_o, w_out, norm_g, sh2, sc2, wr_hi, wr_lo):
    b, n, d = x.shape
    tm = min(512, n)
    tok = lambda w: pl.BlockSpec((1, tm, w), lambda bi, i: (bi, i, 0))
    full = lambda a: pl.BlockSpec(a.shape, lambda bi, i: (0,) * a.ndim)
    mod = pl.BlockSpec((1, 1, d), lambda bi, i: (bi, 0, 0))
    return pl.pallas_call(
        _merge_kernel,
        grid=(b, n // tm),
        in_specs=[tok(d), tok(FT_WIDTH), tok(HEAD_W), tok(HEAD_W), tok(N_BRANCHES * d), mod,
                  full(w_ft), full(w_na_o), full(w_df_o), full(w_out), full(norm_g), mod, mod,
                  full(wr_hi), full(wr_lo)],
        out_specs=[tok(d), tok(d), tok(ROUTER_PAD)],
        out_shape=[jax.ShapeDtypeStruct((b, n, d), F32), jax.ShapeDtypeStruct((b, n, d), BF16),
                   jax.ShapeDtypeStruct((b, n, ROUTER_PAD), F32)],
        compiler_params=_cparams(("parallel", "parallel")),
        name="merge",
    )(x, f, o_na, o_df, gates, g1, w_ft, w_na_o, w_df_o, w_out, norm_g, sh2, sc2, wr_hi, wr_lo)


ROUTE_TILE = LANES
SLOT_ALIGN = 16
ONE_BITS = 0x3F800000


def _slot_window(cap):
    return min(ROUTE_TILE + SLOT_ALIGN, cap)


def _route_kernel(lg_ref, rel_ref, aff_ref, lo_ref, bits_ref, sel_ref, *, cap, win):
    n = lg_ref.shape[1]
    nb = n // ROUTE_TILE
    lane = lax.broadcasted_iota(jnp.int32, (n, LANES), 1)
    z = jnp.where(lane < N_EXPERTS, lg_ref[0], MASK_VALUE)
    p = jnp.exp(z - jnp.max(z, axis=-1, keepdims=True))
    aff = p / jnp.sum(p, axis=-1, keepdims=True)
    aff_ref[0] = aff
    bits_ref[...] = pltpu.bitcast(aff, jnp.int32)

    def bisect(_, carry):
        lo, hi = carry
        mid = (lo + hi) >> 1
        cnt = jnp.sum(jnp.where(bits_ref[...] >= mid, 1.0, 0.0), axis=0, keepdims=True)
        ge = cnt >= cap
        return jnp.where(ge, mid, lo), jnp.where(ge, hi, mid)

    lo0 = jnp.zeros((1, LANES), jnp.int32)
    hi0 = jnp.full((1, LANES), ONE_BITS + 1, jnp.int32)
    thr, _ = lax.fori_loop(0, 31, bisect, (lo0, hi0))
    n_above = jnp.sum(jnp.where(bits_ref[...] > thr, 1.0, 0.0), axis=0, keepdims=True)
    need = cap - n_above

    row = lax.broadcasted_iota(jnp.int32, (ROUTE_TILE, ROUTE_TILE), 0)
    col = lax.broadcasted_iota(jnp.int32, (ROUTE_TILE, ROUTE_TILE), 1)
    tri = jnp.where(row >= col, 1.0, 0.0).astype(BF16)

    carry = jnp.zeros((1, LANES), F32)
    for blk in range(nb):
        rows = slice(blk * ROUTE_TILE, (blk + 1) * ROUTE_TILE)
        bb = bits_ref[rows]
        eq = jnp.where(bb == thr, 1.0, 0.0)
        incl = _dot(tri, eq.astype(BF16))
        before = incl - eq + carry
        take = jnp.where(before < need, eq, 0.0)
        sel_ref[rows] = jnp.where(bb > thr, 1.0, take)
        carry = carry + incl[ROUTE_TILE - 1:ROUTE_TILE]

    carry = jnp.zeros((1, LANES), F32)
    for blk in range(nb):
        rows = slice(blk * ROUTE_TILE, (blk + 1) * ROUTE_TILE)
        sel = sel_ref[rows]
        incl = _dot(tri, sel.astype(BF16))
        lo_al = jnp.minimum(jnp.floor(carry * (1.0 / SLOT_ALIGN)) * SLOT_ALIGN, float(cap - win))
        rel_ref[0, rows] = jnp.where(sel > 0.0, carry - lo_al + incl - sel, -1.0)
        lo_ref[0, blk:blk + 1] = lo_al.astype(jnp.int32)
        carry = carry + incl[ROUTE_TILE - 1:ROUTE_TILE]


def _route(logits, cap):
    b, n, _ = logits.shape
    nt = n // ROUTE_TILE
    win = _slot_window(cap)
    tok = pl.BlockSpec((1, n, LANES), lambda bi: (bi, 0, 0))
    return pl.pallas_call(
        functools.partial(_route_kernel, cap=cap, win=win),
        grid=(b,),
        in_specs=[tok],
        out_specs=[tok, tok, pl.BlockSpec((1, nt, LANES), lambda bi: (bi, 0, 0))],
        out_shape=[jax.ShapeDtypeStruct((b, n, LANES), F32), jax.ShapeDtypeStruct((b, n, LANES), F32),
                   jax.ShapeDtypeStruct((b, nt, LANES), jnp.int32)],
        scratch_shapes=[pltpu.VMEM((n, LANES), jnp.int32), pltpu.VMEM((n, LANES), F32)],
        compiler_params=_cparams(("parallel",)),
        name="route",
    )(logits)


def _expert_kernel(lo_ref, h_ref, rel_ref, wg_ref, wu_ref, wd_ref, o_ref, xs_ref, *, f_chunk, win):
    e = pl.program_id(0)
    b = pl.program_id(1)
    nt = rel_ref.shape[2]
    xs_ref[...] = jnp.zeros_like(xs_ref)
    slot = lax.broadcasted_iota(jnp.int32, (win, ROUTE_TILE), 0).astype(F32)

    def gather_tile(t, carry):
        lo = pl.multiple_of(lo_ref[(b * nt + t) * N_EXPERTS + e], SLOT_ALIGN)
        onehot = jnp.where(rel_ref[0, 0, pl.ds(t, 1), :] == slot, 1.0, 0.0).astype(BF16)
        tok = h_ref[0, pl.ds(pl.multiple_of(t * ROUTE_TILE, ROUTE_TILE), ROUTE_TILE), :]
        xs_ref[pl.ds(lo, win), :] += _dot(onehot, tok)
        return carry

    lax.fori_loop(0, nt, gather_tile, 0)
    x = xs_ref[...].astype(BF16)
    ff = wg_ref.shape[2]
    acc = jnp.zeros(xs_ref.shape, F32)
    for f0 in range(0, ff, f_chunk):
        a = _dot(x, wg_ref[0, :, f0:f0 + f_chunk])
        u = _dot(x, wu_ref[0, :, f0:f0 + f_chunk])
        hm = (a * jax.nn.sigmoid(a) * u).astype(BF16)
        acc = acc + _dot(hm, wd_ref[0, f0:f0 + f_chunk, :])
    o_ref[0, 0] = acc.astype(BF16)


def _experts(lo_flat, h2, rel_t, w_gate, w_up, w_down, cap):
    b, n, d = h2.shape
    e = w_gate.shape[0]
    ff = w_gate.shape[2]
    nt = n // ROUTE_TILE
    grid_spec = pltpu.PrefetchScalarGridSpec(
        num_scalar_prefetch=1,
        grid=(e, b),
        in_specs=[
            pl.BlockSpec((1, n, d), lambda ei, bi, lo: (bi, 0, 0)),
            pl.BlockSpec((1, 1, nt, ROUTE_TILE), lambda ei, bi, lo: (bi, ei, 0, 0)),
            pl.BlockSpec((1, d, ff), lambda ei, bi, lo: (ei, 0, 0)),
            pl.BlockSpec((1, d, ff), lambda ei, bi, lo: (ei, 0, 0)),
            pl.BlockSpec((1, ff, d), lambda ei, bi, lo: (ei, 0, 0)),
        ],
        out_specs=pl.BlockSpec((1, 1, cap, d), lambda ei, bi, lo: (bi, ei, 0, 0)),
        scratch_shapes=[pltpu.VMEM((cap, d), F32)],
    )
    return pl.pallas_call(
        functools.partial(_expert_kernel, f_chunk=min(512, ff), win=_slot_window(cap)),
        grid_spec=grid_spec,
        out_shape=jax.ShapeDtypeStruct((b, e, cap, d), BF16),
        compiler_params=_cparams(("arbitrary", "arbitrary")),
        name="experts",
    )(lo_flat, h2, rel_t, w_gate, w_up, w_down)


def _combine_kernel(lo_ref, x_ref, y_ref, rel_ref, aff_ref, g_ref, o_ref, *, win):
    b = pl.program_id(0)
    t = pl.program_id(1)
    nt = pl.num_programs(1)
    tile = x_ref.shape[1]
    slot = lax.broadcasted_iota(jnp.int32, (tile, win), 1).astype(F32)
    rel = rel_ref[0]
    aff = aff_ref[0]
    acc = jnp.zeros(x_ref.shape[1:], F32)
    for e in range(N_EXPERTS):
        lo = pl.multiple_of(lo_ref[(b * nt + t) * N_EXPERTS + e], SLOT_ALIGN)
        onehot = jnp.where(rel[:, e:e + 1] == slot, 1.0, 0.0).astype(BF16)
        acc = acc + aff[:, e:e + 1] * _dot(onehot, y_ref[0, e, pl.ds(lo, win), :])
    o_ref[0] = x_ref[0] + g_ref[0] * acc


def _combine(lo_flat, x, ye, rel, aff, g2, cap):
    b, n, d = x.shape
    e = ye.shape[1]
    nt = n // ROUTE_TILE
    tok = lambda w: pl.BlockSpec((1, ROUTE_TILE, w), lambda bi, t, lo: (bi, t, 0))
    grid_spec = pltpu.PrefetchScalarGridSpec(
        num_scalar_prefetch=1,
        grid=(b, nt),
        in_specs=[
            tok(d),
            pl.BlockSpec((1, e, cap, d), lambda bi, t, lo: (bi, 0, 0, 0)),
            tok(LANES),
            tok(LANES),
            pl.BlockSpec((1, 1, d), lambda bi, t, lo: (bi, 0, 0)),
        ],
        out_specs=tok(d),
    )
    return pl.pallas_call(
        functools.partial(_combine_kernel, win=_slot_window(cap)),
        grid_spec=grid_spec,
        out_shape=jax.ShapeDtypeStruct((b, n, d), F32),
        compiler_params=_cparams(("arbitrary", "arbitrary")),
        name="combine",
    )(lo_flat, x, ye, rel, aff, g2)


def _expert_choice_residual(x, h2, logits, g2, w_gate, w_up, w_down):
    b, n, d = h2.shape
    cap = EC_CAPACITY_FACTOR * n // N_EXPERTS
    nt = n // ROUTE_TILE
    assert n % ROUTE_TILE == 0 and cap % SLOT_ALIGN == 0
    rel, aff, lo = _route(logits, cap)
    lo_flat = lo[:, :, :N_EXPERTS].reshape(-1)
    rel_t = rel[:, :, :N_EXPERTS].transpose(0, 2, 1).reshape(b, N_EXPERTS, nt, ROUTE_TILE)
    ye = _experts(lo_flat, h2, rel_t, w_gate, w_up, w_down, cap)
    return _combine(lo_flat, x, ye, rel, aff, g2, cap)


def _dft_mats(n):
    k = jnp.arange(n, dtype=jnp.int32)
    ang = ((k[:, None] * k[None, :]) % n).astype(F32) * (2.0 * math.pi / n)
    return jnp.cos(ang), jnp.sin(ang)


def _chan_dft():
    c, s = _dft_mats(FT_GROUP_DIM)
    eye = jnp.eye(FT_GROUPS, dtype=F32)
    return jnp.concatenate([jnp.kron(eye, c), jnp.kron(eye, s)], axis=1).astype(BF16)


def _group_mean_mat(group):
    gid = jnp.arange(HEAD_W) // group
    return jnp.where(gid[:, None] == gid[None, :], 1.0 / group, 0.0).astype(BF16)


def _rope_tables(n):
    t = jnp.arange(n)
    row = (t // GRID_W).astype(F32)
    col = (t % GRID_W).astype(F32)
    ax = DF_QK_DIM // 2
    inv = ROPE_BASE ** (-jnp.arange(0, ax, 2, dtype=F32) / ax)
    lane = jnp.arange(LANES)
    freq = inv[lane % (ax // 2)]
    pos = jnp.where(((lane % DF_QK_DIM) < ax)[None, :], row[:, None], col[:, None])
    ang = pos * freq[None, :]
    sign = jnp.where((lane % ax) < ax // 2, -1.0, 1.0)
    return jnp.cos(ang), jnp.sin(ang) * sign[None, :]


def kernel(x, c, ctx, c_ctx, norm1_g, norm2_g, w_ada, b_ada, w_in, na_qn_g, na_kn_g, na_rpb, df_qn_g, df_kn_g,
           df_lambda, df_subln_g, w_ft, w_na_o, w_df_o, w_out, w_router, w_gate, w_up, w_down):
    b, n, d = x.shape
    lc = ctx.shape[1]
    depth = w_ada.shape[0]
    assert b + 1 <= MOD_ROWS and n % GRID_W == 0

    cc = jnp.zeros((MOD_ROWS, d), F32).at[:b].set(c).at[b].set(c_ctx)
    mods = _ada(cc, w_ada, b_ada)

    consts = {"chan_dft": _chan_dft(), "g64": _group_mean_mat(NA_HEAD_DIM), "g32": _group_mean_mat(DF_QK_DIM)}
    rope_lat = _rope_tables(n)
    rope_ctx = (jnp.zeros((lc, LANES), F32), jnp.zeros((lc, LANES), F32))
    cos_n, sin_n = _dft_mats(n)
    dft_lat = (cos_n.astype(BF16), (-sin_n).astype(BF16))
    cos_c, sin_c = _dft_mats(lc)
    dft_ctx = (cos_c.astype(BF16), (-sin_c).astype(BF16))
    rows = n // GRID_W

    xc = ctx
    for i in range(depth):
        last = i == depth - 1
        lam_init = 0.8 - 0.6 * math.exp(-0.3 * i)
        m_lat = mods[i, :b].reshape(b, 6, 1, d)
        m_ctx = jnp.broadcast_to(mods[i, b].reshape(1, 6, 1, d), (b, 6, 1, d))
        sh1, sc1, g1, sh2, sc2, g2 = [m_lat[:, j] for j in range(6)]
        csh1, csc1, cg1, csh2, csc2, cg2 = [m_ctx[:, j] for j in range(6)]

        w_in_b = w_in[i].astype(BF16)
        gains = {
            "naq": jnp.tile(na_qn_g[i], NA_HEADS).reshape(1, HEAD_W),
            "nak": jnp.tile(na_kn_g[i], NA_HEADS).reshape(1, HEAD_W),
            "dfq": jnp.tile(df_qn_g[i], 2 * DF_HEADS).reshape(1, HEAD_W),
            "dfk": jnp.tile(df_kn_g[i], 2 * DF_HEADS).reshape(1, HEAD_W),
        }
        n1g = norm1_g[i].reshape(1, d)
        n2g = norm2_g[i].reshape(1, d)
        subln = jnp.tile(df_subln_g[i], 2).reshape(1, LANES)
        w_ft_b, w_na_b, w_df_b, w_out_b = (w.astype(BF16) for w in (w_ft[i], w_na_o[i], w_df_o[i], w_out[i]))
        wr = jnp.zeros((d, ROUTER_PAD), F32).at[:, :N_EXPERTS].set(w_router[i])
        wr_hi, wr_lo = _split_bf16(wr)
        wg_b, wu_b, wd_b = w_gate[i].astype(BF16), w_up[i].astype(BF16), w_down[i].astype(BF16)

        if last:
            nkc, nvc, dkc, dvc = _inproj(xc, csh1, csc1, n1g, w_in_b, consts, gains, rope_ctx,
                                         rope=False, kv_only=True)
        else:
            fabc, nqc, dqc, nkc, nvc, dkc, dvc, gatec = _inproj(xc, csh1, csc1, n1g, w_in_b, consts, gains,
                                                                 rope_ctx, rope=False, kv_only=False)
        dvc_aug = _augment_v(dvc)

        fab, nq, dq, nk, nv, dk, dv, gate = _inproj(x, sh1, sc1, n1g, w_in_b, consts, gains, rope_lat,
                                                     rope=True, kv_only=False)
        f = _fourier(fab, *dft_lat)
        o_na = _na_attention(nq, nk, nv, nkc, nvc, *_na_bias_table(na_rpb[i], rows))
        o_df = _diff_attention(dq, jnp.concatenate([dk, dkc], axis=1),
                               jnp.concatenate([_augment_v(dv), dvc_aug], axis=1),
                               df_lambda[i], subln, lam_init)
        x, h2, logits = _merge(x, f, o_na, o_df, gate, g1, w_ft_b, w_na_b, w_df_b, w_out_b, n2g, sh2, sc2,
                               wr_hi, wr_lo)
        x = _expert_choice_residual(x, h2, logits, g2, wg_b, wu_b, wd_b)

        if not last:
            fc = _fourier(fabc, *dft_ctx)
            o_nac = _ctx_na_attention(nqc, nkc, nvc)
            o_dfc = _diff_attention(dqc, dkc, dvc_aug, df_lambda[i], subln, lam_init)
            xc, hc2, logits_c = _merge(xc, fc, o_nac, o_dfc, gatec, cg1, w_ft_b, w_na_b, w_df_b, w_out_b, n2g,
                                       csh2, csc2, wr_hi, wr_lo)
            xc = _expert_choice_residual(xc, hc2, logits_c, cg2, wg_b, wu_b, wd_b)
    return x
```

```python
import functools
import math

import jax
import jax.numpy as jnp
from jax import lax
from jax.experimental import pallas as pl
from jax.experimental.pallas import tpu as pltpu

F32 = jnp.float32
BF16 = jnp.bfloat16

GRID_W = 64
FT_GROUPS = 4
FT_GROUP_DIM = 64
FT_WIDTH = FT_GROUPS * FT_GROUP_DIM
NA_HEADS = 6
NA_HEAD_DIM = 64
NA_WIDTH = NA_HEADS * NA_HEAD_DIM
NA_WIN_H = 8
NA_WIN_W = 16
DF_HEADS = 6
DF_QK_DIM = 32
DF_V_DIM = 2 * DF_QK_DIM
DF_QK_WIDTH = DF_HEADS * 2 * DF_QK_DIM
DF_WIDTH = DF_HEADS * DF_V_DIM
N_BRANCHES = 3
N_EXPERTS = 16
EC_CAPACITY_FACTOR = 2
ROPE_BASE = 10000.0
EPS = 1e-6
MASK_VALUE = -1e30

LANES = 128
VMEM_LIMIT_BYTES = 56 * 1024 * 1024

HEAD_W = 384
N_PAIRS = HEAD_W // LANES
MOD_ROWS = 16
ROUTER_PAD = LANES


def _cparams(sem):
    return pltpu.CompilerParams(dimension_semantics=sem, vmem_limit_bytes=VMEM_LIMIT_BYTES)


def _dot(a, b):
    return jnp.dot(a, b, preferred_element_type=F32)


def _dot_nt(a, b):
    return lax.dot_general(a, b, (((1,), (1,)), ((), ())), preferred_element_type=F32)


def _split_bf16(v):
    hi = v.astype(BF16)
    lo = (v - hi.astype(F32)).astype(BF16)
    return hi, lo


def _ada_kernel(c_ref, w_ref, b_ref, o_ref):
    c = c_ref[...]
    a = c * jax.nn.sigmoid(c)
    a_hi, a_lo = _split_bf16(a)
    w_hi, w_lo = _split_bf16(w_ref[0])
    acc = _dot(a_hi, w_hi) + _dot(a_lo, w_hi) + _dot(a_hi, w_lo)
    o_ref[0] = acc + b_ref[0]


def _ada(cc, w_ada, b_ada):
    depth, d, d6 = w_ada.shape
    tn = 512
    return pl.pallas_call(
        _ada_kernel,
        grid=(depth, d6 // tn),
        in_specs=[
            pl.BlockSpec((MOD_ROWS, d), lambda l, j: (0, 0)),
            pl.BlockSpec((1, d, tn), lambda l, j: (l, 0, j)),
            pl.BlockSpec((1, 1, tn), lambda l, j: (l, 0, j)),
        ],
        out_specs=pl.BlockSpec((1, MOD_ROWS, tn), lambda l, j: (l, 0, j)),
        out_shape=jax.ShapeDtypeStruct((depth, MOD_ROWS, d6), F32),
        compiler_params=_cparams(("arbitrary", "arbitrary")),
        name="ada",
    )(cc, w_ada, b_ada.reshape(depth, 1, d6))


OFF_Q = FT_WIDTH
OFF_DQ = OFF_Q + NA_WIDTH
OFF_KV = OFF_DQ + DF_QK_WIDTH
OFF_NV = OFF_KV + NA_WIDTH
OFF_DK = OFF_NV + NA_WIDTH
OFF_DV = OFF_DK + DF_QK_WIDTH
OFF_GATE = OFF_DV + DF_WIDTH


def _group_rms(v, gmat_ref, gain_ref):
    hi, lo = _split_bf16(v * v)
    ms = _dot(hi, gmat_ref[...]) + _dot(lo, gmat_ref[...])
    return v * lax.rsqrt(ms + EPS) * gain_ref[...]


def _rope_chunk(vj, cos, sin_signed, first_half):
    fwd = pltpu.roll(vj, LANES - 8, 1)
    bwd = pltpu.roll(vj, 8, 1)
    partner = jnp.where(first_half, fwd, bwd)
    return vj * cos + partner * sin_signed


def _inproj_kernel(x_ref, sh_ref, sc_ref, ng_ref, w_ref, cd_ref, g64_ref, g32_ref,
                   naq_g_ref, nak_g_ref, dfq_g_ref, dfk_g_ref, cos_ref, sin_ref,
                   *out_refs, rope, kv_only):
    x = x_ref[0]
    ms = jnp.mean(x * x, axis=-1, keepdims=True)
    y = x * lax.rsqrt(ms + EPS) * ng_ref[...]
    h = (y * (1.0 + sc_ref[0]) + sh_ref[0]).astype(BF16)

    def proj(c0, c1):
        return _dot(h, w_ref[:, c0:c1])

    if rope:
        lane = lax.broadcasted_iota(jnp.int32, (x.shape[0], LANES), 1)
        first_half = (lane & 8) == 0
        cos = cos_ref[...]
        sin_signed = sin_ref[...]

    def df_qk(v, gain_ref, scale, o_ref):
        v = _group_rms(v, g32_ref, gain_ref)
        for j in range(N_PAIRS):
            vj = v[:, j * LANES:(j + 1) * LANES]
            if rope:
                vj = _rope_chunk(vj, cos, sin_signed, first_half)
            o_ref[0, :, j * LANES:(j + 1) * LANES] = (vj * scale).astype(BF16)

    if kv_only:
        nk_ref, nv_ref, dk_ref, dv_ref = out_refs
    else:
        fab_ref, nq_ref, dq_ref, nk_ref, nv_ref, dk_ref, dv_ref, gate_ref = out_refs
        u = proj(0, FT_WIDTH)
        fab_ref[0] = _dot(u.astype(BF16), cd_ref[...]).astype(BF16)
        nq = _group_rms(proj(OFF_Q, OFF_DQ), g64_ref, naq_g_ref)
        nq_ref[0] = (nq * (NA_HEAD_DIM ** -0.5)).astype(BF16)
        df_qk(proj(OFF_DQ, OFF_KV), dfq_g_ref, DF_QK_DIM ** -0.5 * math.log2(math.e), dq_ref)
        d = x.shape[1]
        for j in range(N_BRANCHES):
            z = proj(OFF_GATE + j * d, OFF_GATE + (j + 1) * d)
            gate_ref[0, :, j * d:(j + 1) * d] = jax.nn.sigmoid(z).astype(BF16)

    nk_ref[0] = _group_rms(proj(OFF_KV, OFF_NV), g64_ref, nak_g_ref).astype(BF16)
    nv_ref[0] = proj(OFF_NV, OFF_DK).astype(BF16)
    df_qk(proj(OFF_DK, OFF_DV), dfk_g_ref, 1.0, dk_ref)
    dv_ref[0] = proj(OFF_DV, OFF_GATE).astype(BF16)


def _inproj(x, sh, sc, norm_g, w_in, consts, gains, rope_tabs, *, rope, kv_only):
    b, n, d = x.shape
    tm = min(512, n)
    tok = lambda w: pl.BlockSpec((1, tm, w), lambda bi, i: (bi, i, 0))
    full = lambda a: pl.BlockSpec(a.shape, lambda bi, i: (0,) * a.ndim)
    mod = pl.BlockSpec((1, 1, d), lambda bi, i: (bi, 0, 0))
    cos_t, sin_t = rope_tabs
    tab = pl.BlockSpec((tm, LANES), lambda bi, i: (i, 0))
    slab = jax.ShapeDtypeStruct((b, n, HEAD_W), BF16)
    if kv_only:
        out_shape = [slab] * 4
        out_specs = [tok(HEAD_W)] * 4
    else:
        out_shape = [jax.ShapeDtypeStruct((b, n, 2 * FT_WIDTH), BF16)] + [slab] * 6 + [
            jax.ShapeDtypeStruct((b, n, N_BRANCHES * d), BF16)]
        out_specs = [tok(2 * FT_WIDTH)] + [tok(HEAD_W)] * 6 + [tok(N_BRANCHES * d)]
    args = [x, sh, sc, norm_g, w_in, consts["chan_dft"], consts["g64"], consts["g32"],
            gains["naq"], gains["nak"], gains["dfq"], gains["dfk"], cos_t, sin_t]
    in_specs = [tok(d), mod, mod, full(norm_g), full(w_in), full(consts["chan_dft"]),
                full(consts["g64"]), full(consts["g32"]), full(gains["naq"]), full(gains["nak"]),
                full(gains["dfq"]), full(gains["dfk"]), tab, tab]
    return pl.pallas_call(
        functools.partial(_inproj_kernel, rope=rope, kv_only=kv_only),
        grid=(b, n // tm),
        in_specs=in_specs,
        out_specs=out_specs,
        out_shape=out_shape,
        compiler_params=_cparams(("parallel", "parallel")),
        name="inproj_kv" if kv_only else "inproj",
    )(*args)


def _fourier_kernel(c_ref, s_ref, ab_ref, o_ref, acc_ref, *, scale):
    k = pl.program_id(1)

    @pl.when(k == 0)
    def _():
        acc_ref[...] = jnp.zeros_like(acc_ref)

    cm = c_ref[...]
    sm = s_ref[...]
    for bi in range(ab_ref.shape[0]):
        acc_ref[bi] += _dot(cm, ab_ref[bi, :, :FT_WIDTH]) + _dot(sm, ab_ref[bi, :, FT_WIDTH:])

    @pl.when(k == pl.num_programs(1) - 1)
    def _():
        o_ref[...] = (acc_ref[...] * scale).astype(o_ref.dtype)


def _fourier(fab, cos_m, nsin_m):
    b, n, _ = fab.shape
    tn = min(1024, n)
    tk = min(512, n)
    scale = 1.0 / math.sqrt(n * FT_GROUP_DIM)
    return pl.pallas_call(
        functools.partial(_fourier_kernel, scale=scale),
        grid=(n // tn, n // tk),
        in_specs=[
            pl.BlockSpec((tn, tk), lambda i, k: (i, k)),
            pl.BlockSpec((tn, tk), lambda i, k: (i, k)),
            pl.BlockSpec((b, tk, 2 * FT_WIDTH), lambda i, k: (0, k, 0)),
        ],
        out_specs=pl.BlockSpec((b, tn, FT_WIDTH), lambda i, k: (0, i, 0)),
        out_shape=jax.ShapeDtypeStruct((b, n, FT_WIDTH), BF16),
        scratch_shapes=[pltpu.VMEM((b, tn, FT_WIDTH), F32)],
        compiler_params=_cparams(("parallel", "arbitrary")),
        name="fourier",
    )(cos_m, nsin_m, fab)


NA_ROW_GROUP = 4


def _na_kernel(pid_ref, q_ref, k_ref, v_ref, kc_ref, vc_ref, bias_ref, o_ref, *, rows, kh):
    del pid_ref
    g = pl.program_id(1)
    rq = NA_ROW_GROUP * GRID_W
    key_rows = NA_ROW_GROUP + kh - 1
    u = jnp.clip(NA_ROW_GROUP * g - kh // 2, 0, rows - key_rows)
    start = pl.multiple_of(u * GRID_W, GRID_W)
    lane = lax.broadcasted_iota(jnp.int32, (rq, LANES), 1)
    first = lane < NA_HEAD_DIM
    for pr in range(N_PAIRS):
        cols = slice(pr * LANES, (pr + 1) * LANES)
        q2 = q_ref[0, :, cols]
        zero = jnp.zeros_like(q2)
        qcat = jnp.concatenate([jnp.where(first, q2, zero), jnp.where(first, zero, q2)], axis=0)
        kb = k_ref[0, pl.ds(start, key_rows * GRID_W), cols]
        vb = v_ref[0, pl.ds(start, key_rows * GRID_W), cols]
        s1 = _dot_nt(qcat, kb) + bias_ref[0, pr]
        s2 = _dot_nt(qcat, kc_ref[0, :, cols])
        m = jnp.maximum(jnp.max(s1, axis=-1, keepdims=True), jnp.max(s2, axis=-1, keepdims=True))
        p1 = jnp.exp(s1 - m)
        p2 = jnp.exp(s2 - m)
        l = jnp.sum(p1, axis=-1, keepdims=True) + jnp.sum(p2, axis=-1, keepdims=True)
        o = (_dot(p1.astype(BF16), vb) + _dot(p2.astype(BF16), vc_ref[0, :, cols])) / l
        o_ref[0, :, cols] = jnp.where(first, o[:rq], o[rq:]).astype(BF16)


def _na_attention(nq, nk, nv, nkc, nvc, bias, pattern_ids):
    b, n, _ = nq.shape
    lc = nkc.shape[1]
    rows = n // GRID_W
    kh = min(NA_WIN_H, rows)
    rq = NA_ROW_GROUP * GRID_W
    grid_spec = pltpu.PrefetchScalarGridSpec(
        num_scalar_prefetch=1,
        grid=(b, rows // NA_ROW_GROUP),
        in_specs=[
            pl.BlockSpec((1, rq, HEAD_W), lambda bi, g, pid: (bi, g, 0)),
            pl.BlockSpec((1, n, HEAD_W), lambda bi, g, pid: (bi, 0, 0)),
            pl.BlockSpec((1, n, HEAD_W), lambda bi, g, pid: (bi, 0, 0)),
            pl.BlockSpec((1, lc, HEAD_W), lambda bi, g, pid: (bi, 0, 0)),
            pl.BlockSpec((1, lc, HEAD_W), lambda bi, g, pid: (bi, 0, 0)),
            pl.BlockSpec((1,) + bias.shape[1:], lambda bi, g, pid: (pid[g], 0, 0, 0)),
        ],
        out_specs=pl.BlockSpec((1, rq, HEAD_W), lambda bi, g, pid: (bi, g, 0)),
    )
    return pl.pallas_call(
        functools.partial(_na_kernel, rows=rows, kh=kh),
        grid_spec=grid_spec,
        out_shape=jax.ShapeDtypeStruct((b, n, HEAD_W), BF16),
        compiler_params=_cparams(("parallel", "arbitrary")),
        name="na_attention",
    )(pattern_ids, nq, nk, nv, nkc, nvc, bias)


def _na_patterns(rows):
    kh = min(NA_WIN_H, rows)
    key_rows = NA_ROW_GROUP + kh - 1
    assert rows % NA_ROW_GROUP == 0 and rows >= key_rows
    patterns, ids = [], []
    for g in range(rows // NA_ROW_GROUP):
        u = min(max(NA_ROW_GROUP * g - kh // 2, 0), rows - key_rows)
        geo = []
        for r in range(NA_ROW_GROUP * g, NA_ROW_GROUP * (g + 1)):
            rs = min(max(r - kh // 2, 0), rows - kh)
            geo.append((r - u, rs - u))
        geo = tuple(geo)
        if geo not in patterns:
            patterns.append(geo)
        ids.append(patterns.index(geo))
    return patterns, ids


def _na_bias_table(rpb, rows):
    kh = min(NA_WIN_H, rows)
    key_rows = NA_ROW_GROUP + kh - 1
    patterns, ids = _na_patterns(rows)
    geo = jnp.asarray(patterns, dtype=jnp.int32)
    rq_off, win_off = geo[..., 0], geo[..., 1]
    a = jnp.arange(key_rows)
    row_ok = (a >= win_off[..., None]) & (a < win_off[..., None] + kh)
    row_idx = jnp.clip(a - rq_off[..., None] + NA_WIN_H - 1, 0, 2 * NA_WIN_H - 2)
    cols = jnp.arange(GRID_W)
    col_start = jnp.clip(cols - NA_WIN_W // 2, 0, GRID_W - NA_WIN_W)
    kc = jnp.arange(GRID_W)
    col_ok = (kc[None, :] >= col_start[:, None]) & (kc[None, :] < col_start[:, None] + NA_WIN_W)
    col_idx = jnp.clip(kc[None, :] - cols[:, None] + NA_WIN_W - 1, 0, 2 * NA_WIN_W - 2)
    t = rpb[:, row_idx]
    pick = (jnp.arange(2 * NA_WIN_W - 1)[:, None, None] == col_idx[None]).astype(F32)
    t = jnp.einsum('hpraj,jck->hprack', t, pick, precision=lax.Precision.HIGHEST)
    ok = row_ok[None, :, :, :, None, None] & col_ok[None, None, None, None]
    t = jnp.where(ok, t, MASK_VALUE)
    t = t.transpose(1, 0, 2, 4, 3, 5)
    n_pat = len(patterns)
    t = t.reshape(n_pat, N_PAIRS, 2 * NA_ROW_GROUP * GRID_W, key_rows * GRID_W)
    return t.astype(F32), jnp.asarray(ids, dtype=jnp.int32)


def _ctx_na_kernel(q_ref, k_ref, v_ref, o_ref):
    tq = q_ref.shape[1]
    lane = lax.broadcasted_iota(jnp.int32, (tq, LANES), 1)
    for pr in range(N_PAIRS):
        cols = slice(pr * LANES, (pr + 1) * LANES)
        q2 = q_ref[0, :, cols]
        kb = k_ref[0, :, cols]
        vb = v_ref[0, :, cols]
        acc = jnp.zeros((tq, LANES), F32)
        for hh in range(2):
            in_head = (lane >= hh * NA_HEAD_DIM) & (lane < (hh + 1) * NA_HEAD_DIM)
            qm = jnp.where(in_head, q2, jnp.zeros_like(q2))
            s = _dot_nt(qm, kb)
            p = jnp.exp(s - jnp.max(s, axis=-1, keepdims=True))
            l = jnp.sum(p, axis=-1, keepdims=True)
            acc = jnp.where(in_head, _dot(p.astype(BF16), vb) / l, acc)
        o_ref[0, :, cols] = acc.astype(BF16)


def _ctx_na_attention(q, k, v):
    b, n, _ = q.shape
    spec = pl.BlockSpec((1, n, HEAD_W), lambda bi: (bi, 0, 0))
    return pl.pallas_call(
        _ctx_na_kernel,
        grid=(b,),
        in_specs=[spec, spec, spec],
        out_specs=spec,
        out_shape=jax.ShapeDtypeStruct((b, n, HEAD_W), BF16),
        compiler_params=_cparams(("parallel",)),
        name="ctx_na_attention",
    )(q, k, v)


def _diff_kernel(q_ref, lam_ref, g_ref, *refs, lam_init):
    kv_refs, o_ref = refs[:-1], refs[-1]
    tq = q_ref.shape[1]
    lp = lam_ref[...]
    lam = (jnp.exp(jnp.sum(lp[0:1] * lp[1:2], axis=-1, keepdims=True))
           - jnp.exp(jnp.sum(lp[2:3] * lp[3:4], axis=-1, keepdims=True)) + lam_init)
    lane = lax.broadcasted_iota(jnp.int32, (tq, LANES), 1)
    kv_refs = [(kv_refs[2 * j], kv_refs[2 * j + 1]) for j in range(len(kv_refs) // 2)]
    for pr in range(N_PAIRS):
        cols = slice(pr * LANES, (pr + 1) * LANES)
        q2 = q_ref[0, :, cols]
        zero = jnp.zeros_like(q2)
        outp = jnp.zeros((tq, LANES), F32)
        for hh in range(2):
            comp = []
            for c in range(2):
                lo = hh * DF_V_DIM + c * DF_QK_DIM
                qm = jnp.where((lane >= lo) & (lane < lo + DF_QK_DIM), q2, zero)
                ss = [_dot_nt(qm, k_ref[0, :, cols]) for k_ref, _ in kv_refs]
                m = functools.reduce(jnp.maximum, [jnp.max(s, axis=-1, keepdims=True) for s in ss])
                res = functools.reduce(jnp.add, [
                    _dot(jnp.exp2(s - m).astype(BF16), v_ref[0, :, 2 * pr * LANES:2 * (pr + 1) * LANES])
                    for s, (_, v_ref) in zip(ss, kv_refs)])
                comp.append(res[:, :LANES] / res[:, LANES:LANES + 1])
            in_head = (lane >= hh * DF_V_DIM) & (lane < (hh + 1) * DF_V_DIM)
            oh = jnp.where(in_head, comp[0] - lam * comp[1], 0.0)
            ms = jnp.sum(oh * oh, axis=-1, keepdims=True) * (1.0 / DF_V_DIM)
            outp = outp + oh * lax.rsqrt(ms + EPS) * g_ref[...] * (1.0 - lam_init)
        o_ref[0, :, cols] = outp.astype(BF16)


def _augment_v(dv):
    b, n, _ = dv.shape
    v3 = dv.reshape(b, n, N_PAIRS, LANES)
    return jnp.concatenate([v3, jnp.ones_like(v3)], axis=-1).reshape(b, n, 2 * HEAD_W)


def _diff_attention(dq, kvs, lam_p, subln_g, lam_init):
    b, nq, _ = dq.shape
    tq = min(256, nq)
    kv_specs, kv_args = [], []
    for k, v in kvs:
        kv_specs += [pl.BlockSpec((1, k.shape[1], HEAD_W), lambda bi, i: (bi, 0, 0)),
                     pl.BlockSpec((1, k.shape[1], 2 * HEAD_W), lambda bi, i: (bi, 0, 0))]
        kv_args += [k, _augment_v(v)]
    return pl.pallas_call(
        functools.partial(_diff_kernel, lam_init=lam_init),
        grid=(b, nq // tq),
        in_specs=[
            pl.BlockSpec((1, tq, HEAD_W), lambda bi, i: (bi, i, 0)),
            pl.BlockSpec(lam_p.shape, lambda bi, i: (0, 0)),
            pl.BlockSpec(subln_g.shape, lambda bi, i: (0, 0)),
        ] + kv_specs,
        out_specs=pl.BlockSpec((1, tq, HEAD_W), lambda bi, i: (bi, i, 0)),
        out_shape=jax.ShapeDtypeStruct((b, nq, HEAD_W), BF16),
        compiler_params=_cparams(("parallel", "arbitrary")),
        name="diff_attention",
    )(dq, lam_p, subln_g, *kv_args)


def _merge_kernel(x_ref, f_ref, na_ref, df_ref, gate_ref, g1_ref, wft_ref, wna_ref, wdf_ref, wout_ref,
                  ng_ref, sh_ref, sc_ref, wrh_ref, wrl_ref, xo_ref, h2_ref, lg_ref):
    d = x_ref.shape[2]
    y_ft = _dot(f_ref[0], wft_ref[...])
    y_na = _dot(na_ref[0], wna_ref[...])
    y_df = _dot(df_ref[0], wdf_ref[...])
    m = (gate_ref[0, :, 0:d].astype(F32) * y_ft + gate_ref[0, :, d:2 * d].astype(F32) * y_na
         + gate_ref[0, :, 2 * d:3 * d].astype(F32) * y_df)
    y = _dot(m.astype(BF16), wout_ref[...])
    xn = x_ref[0] + g1_ref[0] * y
    xo_ref[0] = xn
    ms = jnp.mean(xn * xn, axis=-1, keepdims=True)
    h2 = (xn * lax.rsqrt(ms + EPS) * ng_ref[...]) * (1.0 + sc_ref[0]) + sh_ref[0]
    h_hi, h_lo = _split_bf16(h2)
    h2_ref[0] = h_hi
    lg_ref[0] = _dot(h_hi, wrh_ref[...]) + _dot(h_lo, wrh_ref[...]) + _dot(h_hi, wrl_ref[...])


def _merge(x, f, o_na, o_df, gates, g1, w_ft, w_na_o, w_df_o, w_out, norm_g, sh2, sc2, wr_hi, wr_lo):
    b, n, d = x.shape
    tm = min(512, n)
    tok = lambda w: pl.BlockSpec((1, tm, w), lambda bi, i: (bi, i, 0))
    full = lambda a: pl.BlockSpec(a.shape, lambda bi, i: (0,) * a.ndim)
    mod = pl.BlockSpec((1, 1, d), lambda bi, i: (bi, 0, 0))
    return pl.pallas_call(
        _merge_kernel,
        grid=(b, n // tm),
        in_specs=[tok(d), tok(FT_WIDTH), tok(HEAD_W), tok(HEAD_W), tok(N_BRANCHES * d), mod,
                  full(w_ft), full(w_na_o), full(w_df_o), full(w_out), full(norm_g), mod, mod,
                  full(wr_hi), full(wr_lo)],
        out_specs=[tok(d), tok(d), tok(ROUTER_PAD)],
        out_shape=[jax.ShapeDtypeStruct((b, n, d), F32), jax.ShapeDtypeStruct((b, n, d), BF16),
                   jax.ShapeDtypeStruct((b, n, ROUTER_PAD), F32)],
        compiler_params=_cparams(("parallel", "parallel")),
        name="merge",
    )(x, f, o_na, o_df, gates, g1, w_ft, w_na_o, w_df_o, w_out, norm_g, sh2, sc2, wr_hi, wr_lo)


ROUTE_TILE = LANES
SLOT_ALIGN = 16
ONE_BITS = 0x3F800000


FAST_WINDOW = 48


def _slot_window(cap):
    return min(ROUTE_TILE + SLOT_ALIGN, cap)


def _fast_window(cap):
    return min(FAST_WINDOW, cap)


def _slot_windows(lo, cap):
    b, nt, _ = lo.shape
    filled = jnp.concatenate([lo[:, 1:], jnp.full((b, 1, N_EXPERTS), cap, lo.dtype)], axis=1)
    aligned = (lo // SLOT_ALIGN) * SLOT_ALIGN
    lo_slow = jnp.minimum(aligned, cap - _slot_window(cap))
    lo_fast = jnp.minimum(aligned, cap - _fast_window(cap))
    fast_ok = jnp.all(filled - lo_fast <= _fast_window(cap), axis=-1)
    return {"lo_slow": lo_slow.reshape(-1), "lo_fast": lo_fast.reshape(-1),
            "fast_ok": fast_ok.astype(jnp.int32).reshape(-1)}


def _route_kernel(lg_ref, pos_ref, aff_ref, lo_ref, bits_ref, sel_ref, *, cap):
    n = lg_ref.shape[1]
    nb = n // ROUTE_TILE
    lane = lax.broadcasted_iota(jnp.int32, (n, LANES), 1)
    z = jnp.where(lane < N_EXPERTS, lg_ref[0], MASK_VALUE)
    p = jnp.exp(z - jnp.max(z, axis=-1, keepdims=True))
    aff = p / jnp.sum(p, axis=-1, keepdims=True)
    aff_ref[0] = aff
    bits_ref[...] = pltpu.bitcast(aff, jnp.int32)

    def bisect(_, carry):
        lo, hi = carry
        mid = (lo + hi) >> 1
        cnt = jnp.sum(jnp.where(bits_ref[...] >= mid, 1.0, 0.0), axis=0, keepdims=True)
        ge = cnt >= cap
        return jnp.where(ge, mid, lo), jnp.where(ge, hi, mid)

    lo0 = jnp.zeros((1, LANES), jnp.int32)
    hi0 = jnp.full((1, LANES), ONE_BITS + 1, jnp.int32)
    thr, _ = lax.fori_loop(0, 31, bisect, (lo0, hi0))
    n_above = jnp.sum(jnp.where(bits_ref[...] > thr, 1.0, 0.0), axis=0, keepdims=True)
    need = cap - n_above

    row = lax.broadcasted_iota(jnp.int32, (ROUTE_TILE, ROUTE_TILE), 0)
    col = lax.broadcasted_iota(jnp.int32, (ROUTE_TILE, ROUTE_TILE), 1)
    tri = jnp.where(row >= col, 1.0, 0.0).astype(BF16)

    carry = jnp.zeros((1, LANES), F32)
    for blk in range(nb):
        rows = slice(blk * ROUTE_TILE, (blk + 1) * ROUTE_TILE)
        bb = bits_ref[rows]
        eq = jnp.where(bb == thr, 1.0, 0.0)
        incl = _dot(tri, eq.astype(BF16))
        before = incl - eq + carry
        take = jnp.where(before < need, eq, 0.0)
        sel_ref[rows] = jnp.where(bb > thr, 1.0, take)
        carry = carry + incl[ROUTE_TILE - 1:ROUTE_TILE]

    carry = jnp.zeros((1, LANES), F32)
    for blk in range(nb):
        rows = slice(blk * ROUTE_TILE, (blk + 1) * ROUTE_TILE)
        sel = sel_ref[rows]
        incl = _dot(tri, sel.astype(BF16))
        pos_ref[0, rows] = jnp.where(sel > 0.0, carry + incl - sel, -1.0)
        lo_ref[0, blk:blk + 1] = carry.astype(jnp.int32)
        carry = carry + incl[ROUTE_TILE - 1:ROUTE_TILE]


def _route(logits, cap):
    b, n, _ = logits.shape
    nt = n // ROUTE_TILE
    tok = pl.BlockSpec((1, n, LANES), lambda bi: (bi, 0, 0))
    return pl.pallas_call(
        functools.partial(_route_kernel, cap=cap),
        grid=(b,),
        in_specs=[tok],
        out_specs=[tok, tok, pl.BlockSpec((1, nt, LANES), lambda bi: (bi, 0, 0))],
        out_shape=[jax.ShapeDtypeStruct((b, n, LANES), F32), jax.ShapeDtypeStruct((b, n, LANES), F32),
                   jax.ShapeDtypeStruct((b, nt, LANES), jnp.int32)],
        scratch_shapes=[pltpu.VMEM((n, LANES), jnp.int32), pltpu.VMEM((n, LANES), F32)],
        compiler_params=_cparams(("parallel",)),
        name="route",
    )(logits)


DISPATCH_COLS = 256


def _dispatch_kernel(lo_fast_ref, lo_slow_ref, fast_ref, h_ref, pos_ref, o_ref, *, win_fast, win_slow):
    b = pl.program_id(0)
    t = pl.program_id(1)
    nt = pl.num_programs(1)

    @pl.when(t == 0)
    def _():
        o_ref[...] = jnp.zeros_like(o_ref)

    def run(win, lo_ref):
        slot = lax.broadcasted_iota(jnp.int32, (win, ROUTE_TILE), 0).astype(F32)
        pos_t = pos_ref[0, 0]
        los = [pl.multiple_of(lo_ref[(b * nt + t) * N_EXPERTS + e], SLOT_ALIGN) for e in range(N_EXPERTS)]
        onehot = jnp.concatenate(
            [jnp.where(pos_t[e:e + 1, :] - los[e].astype(F32) == slot, 1.0, 0.0).astype(BF16)
             for e in range(N_EXPERTS)], axis=0)
        for c0 in range(0, h_ref.shape[2], DISPATCH_COLS):
            cols = slice(c0, c0 + DISPATCH_COLS)
            res = _dot(onehot, h_ref[0, :, cols]).astype(BF16)
            for e in range(N_EXPERTS):
                rows = pl.ds(los[e], win)
                o_ref[0, e, rows, cols] = o_ref[0, e, rows, cols] + res[e * win:(e + 1) * win]

    if win_fast == win_slow:
        run(win_slow, lo_slow_ref)
    else:
        fast = fast_ref[b * nt + t] == 1
        pl.when(fast)(functools.partial(run, win_fast, lo_fast_ref))
        pl.when(jnp.logical_not(fast))(functools.partial(run, win_slow, lo_slow_ref))


def _dispatch(windows, h2, pos_t, cap):
    b, n, d = h2.shape
    nt = n // ROUTE_TILE
    grid_spec = pltpu.PrefetchScalarGridSpec(
        num_scalar_prefetch=3,
        grid=(b, nt),
        in_specs=[
            pl.BlockSpec((1, ROUTE_TILE, d), lambda bi, t, *_: (bi, t, 0)),
            pl.BlockSpec((1, 1, N_EXPERTS, ROUTE_TILE), lambda bi, t, *_: (bi, t, 0, 0)),
        ],
        out_specs=pl.BlockSpec((1, N_EXPERTS, cap, d), lambda bi, t, *_: (bi, 0, 0, 0)),
    )
    return pl.pallas_call(
        functools.partial(_dispatch_kernel, win_fast=_fast_window(cap), win_slow=_slot_window(cap)),
        grid_spec=grid_spec,
        out_shape=jax.ShapeDtypeStruct((b, N_EXPERTS, cap, d), BF16),
        compiler_params=_cparams(("arbitrary", "arbitrary")),
        name="dispatch",
    )(windows["lo_fast"], windows["lo_slow"], windows["fast_ok"], h2, pos_t)


def _expert_kernel(x_ref, wg_ref, wu_ref, wd_ref, o_ref, *, f_chunk):
    x = x_ref[0, 0]
    ff = wg_ref.shape[2]
    acc = jnp.zeros(x.shape, F32)
    for f0 in range(0, ff, f_chunk):
        a = _dot(x, wg_ref[0, :, f0:f0 + f_chunk])
        u = _dot(x, wu_ref[0, :, f0:f0 + f_chunk])
        hm = (a * jax.nn.sigmoid(a) * u).astype(BF16)
        acc = acc + _dot(hm, wd_ref[0, f0:f0 + f_chunk, :])
    o_ref[0, 0] = acc.astype(BF16)


def _experts(xe, w_gate, w_up, w_down):
    b, e, cap, d = xe.shape
    ff = w_gate.shape[2]
    tok = pl.BlockSpec((1, 1, cap, d), lambda ei, bi: (bi, ei, 0, 0))
    return pl.pallas_call(
        functools.partial(_expert_kernel, f_chunk=min(512, ff)),
        grid=(e, b),
        in_specs=[
            tok,
            pl.BlockSpec((1, d, ff), lambda ei, bi: (ei, 0, 0)),
            pl.BlockSpec((1, d, ff), lambda ei, bi: (ei, 0, 0)),
            pl.BlockSpec((1, ff, d), lambda ei, bi: (ei, 0, 0)),
        ],
        out_specs=tok,
        out_shape=jax.ShapeDtypeStruct((b, e, cap, d), BF16),
        compiler_params=_cparams(("arbitrary", "arbitrary")),
        name="experts",
    )(xe, w_gate, w_up, w_down)


def _combine_kernel(lo_ref, x_ref, y_ref, pos_ref, aff_ref, g_ref, o_ref, *, win):
    b = pl.program_id(0)
    t = pl.program_id(1)
    nt = pl.num_programs(1)
    tile = x_ref.shape[1]
    slot = lax.broadcasted_iota(jnp.int32, (tile, win), 1).astype(F32)
    pos = pos_ref[0]
    aff = aff_ref[0]
    acc = jnp.zeros(x_ref.shape[1:], F32)
    for e in range(N_EXPERTS):
        lo = pl.multiple_of(lo_ref[(b * nt + t) * N_EXPERTS + e], SLOT_ALIGN)
        onehot = jnp.where(pos[:, e:e + 1] - lo.astype(F32) == slot, 1.0, 0.0).astype(BF16)
        acc = acc + aff[:, e:e + 1] * _dot(onehot, y_ref[0, e, pl.ds(lo, win), :])
    o_ref[0] = x_ref[0] + g_ref[0] * acc


def _combine(lo_flat, x, ye, rel, aff, g2, cap):
    b, n, d = x.shape
    e = ye.shape[1]
    nt = n // ROUTE_TILE
    tok = lambda w: pl.BlockSpec((1, ROUTE_TILE, w), lambda bi, t, lo: (bi, t, 0))
    grid_spec = pltpu.PrefetchScalarGridSpec(
        num_scalar_prefetch=1,
        grid=(b, nt),
        in_specs=[
            tok(d),
            pl.BlockSpec((1, e, cap, d), lambda bi, t, lo: (bi, 0, 0, 0)),
            tok(LANES),
            tok(LANES),
            pl.BlockSpec((1, 1, d), lambda bi, t, lo: (bi, 0, 0)),
        ],
        out_specs=tok(d),
    )
    return pl.pallas_call(
        functools.partial(_combine_kernel, win=_slot_window(cap)),
        grid_spec=grid_spec,
        out_shape=jax.ShapeDtypeStruct((b, n, d), F32),
        compiler_params=_cparams(("arbitrary", "arbitrary")),
        name="combine",
    )(lo_flat, x, ye, rel, aff, g2)


def _expert_choice_residual(x, h2, logits, g2, w_gate, w_up, w_down):
    b, n, d = h2.shape
    cap = EC_CAPACITY_FACTOR * n // N_EXPERTS
    nt = n // ROUTE_TILE
    assert n % ROUTE_TILE == 0 and cap % SLOT_ALIGN == 0
    pos, aff, lo = _route(logits, cap)
    windows = _slot_windows(lo[:, :, :N_EXPERTS], cap)
    pos_t = pos[:, :, :N_EXPERTS].reshape(b, nt, ROUTE_TILE, N_EXPERTS).transpose(0, 1, 3, 2)
    ye = _experts(_dispatch(windows, h2, pos_t, cap), w_gate, w_up, w_down)
    return _combine(windows["lo_slow"], x, ye, pos, aff, g2, cap)


def _dft_mats(n):
    k = jnp.arange(n, dtype=jnp.int32)
    ang = ((k[:, None] * k[None, :]) % n).astype(F32) * (2.0 * math.pi / n)
    return jnp.cos(ang), jnp.sin(ang)


def _dft_mats_bf16(n, block=64):
    assert n % block == 0
    k = jnp.arange(n, dtype=jnp.int32)
    a = jnp.arange(n // block, dtype=jnp.int32) * block
    b = jnp.arange(block, dtype=jnp.int32)
    ang_a = ((a[:, None] * k[None, :]) % n).astype(F32) * (2.0 * math.pi / n)
    ang_b = ((b[:, None] * k[None, :]) % n).astype(F32) * (2.0 * math.pi / n)
    ca, sa = jnp.cos(ang_a)[:, None, :], jnp.sin(ang_a)[:, None, :]
    cb, sb = jnp.cos(ang_b)[None], jnp.sin(ang_b)[None]
    cos_m = (ca * cb - sa * sb).reshape(n, n)
    nsin_m = (-(sa * cb + ca * sb)).reshape(n, n)
    return cos_m.astype(BF16), nsin_m.astype(BF16)


def _chan_dft():
    c, s = _dft_mats(FT_GROUP_DIM)
    eye = jnp.eye(FT_GROUPS, dtype=F32)
    return jnp.concatenate([jnp.kron(eye, c), jnp.kron(eye, s)], axis=1).astype(BF16)


def _group_mean_mat(group):
    gid = jnp.arange(HEAD_W) // group
    return jnp.where(gid[:, None] == gid[None, :], 1.0 / group, 0.0).astype(BF16)


def _rope_tables(n):
    t = jnp.arange(n)
    row = (t // GRID_W).astype(F32)
    col = (t % GRID_W).astype(F32)
    ax = DF_QK_DIM // 2
    inv = ROPE_BASE ** (-jnp.arange(0, ax, 2, dtype=F32) / ax)
    lane = jnp.arange(LANES)
    freq = inv[lane % (ax // 2)]
    pos = jnp.where(((lane % DF_QK_DIM) < ax)[None, :], row[:, None], col[:, None])
    ang = pos * freq[None, :]
    sign = jnp.where((lane % ax) < ax // 2, -1.0, 1.0)
    return jnp.cos(ang), jnp.sin(ang) * sign[None, :]


def kernel(x, c, ctx, c_ctx, norm1_g, norm2_g, w_ada, b_ada, w_in, na_qn_g, na_kn_g, na_rpb, df_qn_g, df_kn_g,
           df_lambda, df_subln_g, w_ft, w_na_o, w_df_o, w_out, w_router, w_gate, w_up, w_down):
    b, n, d = x.shape
    lc = ctx.shape[1]
    depth = w_ada.shape[0]
    assert b + 1 <= MOD_ROWS and n % GRID_W == 0

    cc = jnp.zeros((MOD_ROWS, d), F32).at[:b].set(c).at[b].set(c_ctx)
    mods = _ada(cc, w_ada, b_ada)

    consts = {"chan_dft": _chan_dft(), "g64": _group_mean_mat(NA_HEAD_DIM), "g32": _group_mean_mat(DF_QK_DIM)}
    rope_lat = _rope_tables(n)
    rope_ctx = (jnp.zeros((lc, LANES), F32), jnp.zeros((lc, LANES), F32))
    dft_lat = _dft_mats_bf16(n)
    dft_ctx = _dft_mats_bf16(lc)
    rows = n // GRID_W

    xc = ctx
    for i in range(depth):
        last = i == depth - 1
        lam_init = 0.8 - 0.6 * math.exp(-0.3 * i)
        m_lat = mods[i, :b].reshape(b, 6, 1, d)
        m_ctx = jnp.broadcast_to(mods[i, b].reshape(1, 6, 1, d), (b, 6, 1, d))
        sh1, sc1, g1, sh2, sc2, g2 = [m_lat[:, j] for j in range(6)]
        csh1, csc1, cg1, csh2, csc2, cg2 = [m_ctx[:, j] for j in range(6)]

        w_in_b = w_in[i].astype(BF16)
        gains = {
            "naq": jnp.tile(na_qn_g[i], NA_HEADS).reshape(1, HEAD_W),
            "nak": jnp.tile(na_kn_g[i], NA_HEADS).reshape(1, HEAD_W),
            "dfq": jnp.tile(df_qn_g[i], 2 * DF_HEADS).reshape(1, HEAD_W),
            "dfk": jnp.tile(df_kn_g[i], 2 * DF_HEADS).reshape(1, HEAD_W),
        }
        n1g = norm1_g[i].reshape(1, d)
        n2g = norm2_g[i].reshape(1, d)
        subln = jnp.tile(df_subln_g[i], 2).reshape(1, LANES)
        w_ft_b, w_na_b, w_df_b, w_out_b = (w.astype(BF16) for w in (w_ft[i], w_na_o[i], w_df_o[i], w_out[i]))
        wr = jnp.zeros((d, ROUTER_PAD), F32).at[:, :N_EXPERTS].set(w_router[i])
        wr_hi, wr_lo = _split_bf16(wr)
        wg_b, wu_b, wd_b = w_gate[i].astype(BF16), w_up[i].astype(BF16), w_down[i].astype(BF16)

        if last:
            nkc, nvc, dkc, dvc = _inproj(xc, csh1, csc1, n1g, w_in_b, consts, gains, rope_ctx,
                                         rope=False, kv_only=True)
        else:
            fabc, nqc, dqc, nkc, nvc, dkc, dvc, gatec = _inproj(xc, csh1, csc1, n1g, w_in_b, consts, gains,
                                                                 rope_ctx, rope=False, kv_only=False)

        fab, nq, dq, nk, nv, dk, dv, gate = _inproj(x, sh1, sc1, n1g, w_in_b, consts, gains, rope_lat,
                                                     rope=True, kv_only=False)
        f = _fourier(fab, *dft_lat)
        o_na = _na_attention(nq, nk, nv, nkc, nvc, *_na_bias_table(na_rpb[i], rows))
        o_df = _diff_attention(dq, [(jnp.concatenate([dk, dkc], axis=1), jnp.concatenate([dv, dvc], axis=1))],
                               df_lambda[i], subln, lam_init)
        x, h2, logits = _merge(x, f, o_na, o_df, gate, g1, w_ft_b, w_na_b, w_df_b, w_out_b, n2g, sh2, sc2,
                               wr_hi, wr_lo)
        x = _expert_choice_residual(x, h2, logits, g2, wg_b, wu_b, wd_b)

        if not last:
            fc = _fourier(fabc, *dft_ctx)
            o_nac = _ctx_na_attention(nqc, nkc, nvc)
            o_dfc = _diff_attention(dqc, [(dkc, dvc)], df_lambda[i], subln, lam_init)
            xc, hc2, logits_c = _merge(xc, fc, o_nac, o_dfc, gatec, cg1, w_ft_b, w_na_b, w_df_b, w_out_b, n2g,
                                       csh2, csc2, wr_hi, wr_lo)
            xc = _expert_choice_residual(xc, hc2, logits_c, cg2, wg_b, wu_b, wd_b)
    return x
```

```python
import functools
import math

import jax
import jax.numpy as jnp
from jax import lax
from jax.experimental import pallas as pl
from jax.experimental.pallas import tpu as pltpu

F32 = jnp.float32
BF16 = jnp.bfloat16

GRID_W = 64
FT_GROUPS = 4
FT_GROUP_DIM = 64
FT_WIDTH = FT_GROUPS * FT_GROUP_DIM
NA_HEADS = 6
NA_HEAD_DIM = 64
NA_WIDTH = NA_HEADS * NA_HEAD_DIM
NA_WIN_H = 8
NA_WIN_W = 16
DF_HEADS = 6
DF_QK_DIM = 32
DF_V_DIM = 2 * DF_QK_DIM
DF_QK_WIDTH = DF_HEADS * 2 * DF_QK_DIM
DF_WIDTH = DF_HEADS * DF_V_DIM
N_BRANCHES = 3
N_EXPERTS = 16
EC_CAPACITY_FACTOR = 2
ROPE_BASE = 10000.0
EPS = 1e-6
MASK_VALUE = -1e30

LANES = 128
VMEM_LIMIT_BYTES = 56 * 1024 * 1024

HEAD_W = 384
N_PAIRS = HEAD_W // LANES
MOD_ROWS = 16
ROUTER_PAD = LANES


def _cparams(sem):
    return pltpu.CompilerParams(dimension_semantics=sem, vmem_limit_bytes=VMEM_LIMIT_BYTES)


def _dot(a, b):
    return jnp.dot(a, b, preferred_element_type=F32)


def _dot_nt(a, b):
    return lax.dot_general(a, b, (((1,), (1,)), ((), ())), preferred_element_type=F32)


def _split_bf16(v):
    hi = v.astype(BF16)
    lo = (v - hi.astype(F32)).astype(BF16)
    return hi, lo


def _ada_kernel(c_ref, w_ref, b_ref, o_ref):
    c = c_ref[...]
    a = c * jax.nn.sigmoid(c)
    a_hi, a_lo = _split_bf16(a)
    w_hi, w_lo = _split_bf16(w_ref[0])
    acc = _dot(a_hi, w_hi) + _dot(a_lo, w_hi) + _dot(a_hi, w_lo)
    o_ref[0] = acc + b_ref[0]


def _ada(cc, w_ada, b_ada):
    depth, d, d6 = w_ada.shape
    tn = 512
    return pl.pallas_call(
        _ada_kernel,
        grid=(depth, d6 // tn),
        in_specs=[
            pl.BlockSpec((MOD_ROWS, d), lambda l, j: (0, 0)),
            pl.BlockSpec((1, d, tn), lambda l, j: (l, 0, j)),
            pl.BlockSpec((1, 1, tn), lambda l, j: (l, 0, j)),
        ],
        out_specs=pl.BlockSpec((1, MOD_ROWS, tn), lambda l, j: (l, 0, j)),
        out_shape=jax.ShapeDtypeStruct((depth, MOD_ROWS, d6), F32),
        compiler_params=_cparams(("arbitrary", "arbitrary")),
        name="ada",
    )(cc, w_ada, b_ada.reshape(depth, 1, d6))


OFF_Q = FT_WIDTH
OFF_DQ = OFF_Q + NA_WIDTH
OFF_KV = OFF_DQ + DF_QK_WIDTH
OFF_NV = OFF_KV + NA_WIDTH
OFF_DK = OFF_NV + NA_WIDTH
OFF_DV = OFF_DK + DF_QK_WIDTH
OFF_GATE = OFF_DV + DF_WIDTH


def _group_rms(v, gmat_ref, gain_ref):
    ms = _dot((v * v).astype(BF16), gmat_ref[...])
    return v * lax.rsqrt(ms + EPS) * gain_ref[...]


def _rope_chunk(vj, cos, sin_signed, first_half):
    fwd = pltpu.roll(vj, LANES - 8, 1)
    bwd = pltpu.roll(vj, 8, 1)
    partner = jnp.where(first_half, fwd, bwd)
    return vj * cos + partner * sin_signed


def _inproj_kernel(x_ref, sh_ref, sc_ref, ng_ref, w_ref, cd_ref, g64_ref, g32_ref,
                   naq_g_ref, nak_g_ref, dfq_g_ref, dfk_g_ref, cos_ref, sin_ref,
                   *out_refs, rope, kv_only):
    x = x_ref[0]
    ms = jnp.mean(x * x, axis=-1, keepdims=True)
    y = x * lax.rsqrt(ms + EPS) * ng_ref[...]
    h = (y * (1.0 + sc_ref[0]) + sh_ref[0]).astype(BF16)

    def proj(c0, c1):
        return _dot(h, w_ref[:, c0:c1])

    if rope:
        lane = lax.broadcasted_iota(jnp.int32, (x.shape[0], LANES), 1)
        first_half = (lane & 8) == 0
        cos = cos_ref[...]
        sin_signed = sin_ref[...]

    def df_qk(v, gain_ref, scale, o_ref):
        v = _group_rms(v, g32_ref, gain_ref)
        for j in range(N_PAIRS):
            vj = v[:, j * LANES:(j + 1) * LANES]
            if rope:
                vj = _rope_chunk(vj, cos, sin_signed, first_half)
            o_ref[0, :, j * LANES:(j + 1) * LANES] = (vj * scale).astype(BF16)

    if kv_only:
        nk_ref, nv_ref, dk_ref, dv_ref = out_refs
    else:
        fab_ref, nq_ref, dq_ref, nk_ref, nv_ref, dk_ref, dv_ref, gate_ref = out_refs
        u = proj(0, FT_WIDTH)
        fab_ref[0] = _dot(u.astype(BF16), cd_ref[...]).astype(BF16)
        nq = _group_rms(proj(OFF_Q, OFF_DQ), g64_ref, naq_g_ref)
        nq_ref[0] = (nq * (NA_HEAD_DIM ** -0.5)).astype(BF16)
        df_qk(proj(OFF_DQ, OFF_KV), dfq_g_ref, DF_QK_DIM ** -0.5 * math.log2(math.e), dq_ref)
        d = x.shape[1]
        for j in range(N_BRANCHES):
            z = proj(OFF_GATE + j * d, OFF_GATE + (j + 1) * d)
            gate_ref[0, :, j * d:(j + 1) * d] = jax.nn.sigmoid(z).astype(BF16)

    nk_ref[0] = _group_rms(proj(OFF_KV, OFF_NV), g64_ref, nak_g_ref).astype(BF16)
    nv_ref[0] = proj(OFF_NV, OFF_DK).astype(BF16)
    df_qk(proj(OFF_DK, OFF_DV), dfk_g_ref, 1.0, dk_ref)
    dv = proj(OFF_DV, OFF_GATE).astype(BF16)
    for j in range(N_PAIRS):
        dv_ref[0, :, 2 * j * LANES:(2 * j + 1) * LANES] = dv[:, j * LANES:(j + 1) * LANES]
        dv_ref[0, :, (2 * j + 1) * LANES:(2 * j + 2) * LANES] = jnp.ones((dv.shape[0], LANES), BF16)


def _inproj(x, sh, sc, norm_g, w_in, consts, gains, rope_tabs, *, rope, kv_only):
    b, n, d = x.shape
    tm = min(512, n)
    tok = lambda w: pl.BlockSpec((1, tm, w), lambda bi, i: (bi, i, 0))
    full = lambda a: pl.BlockSpec(a.shape, lambda bi, i: (0,) * a.ndim)
    mod = pl.BlockSpec((1, 1, d), lambda bi, i: (bi, 0, 0))
    cos_t, sin_t = rope_tabs
    tab = pl.BlockSpec((tm, LANES), lambda bi, i: (i, 0))
    slab = jax.ShapeDtypeStruct((b, n, HEAD_W), BF16)
    slab_aug = jax.ShapeDtypeStruct((b, n, 2 * HEAD_W), BF16)
    if kv_only:
        out_shape = [slab] * 3 + [slab_aug]
        out_specs = [tok(HEAD_W)] * 3 + [tok(2 * HEAD_W)]
    else:
        out_shape = [jax.ShapeDtypeStruct((b, n, 2 * FT_WIDTH), BF16)] + [slab] * 5 + [
            slab_aug, jax.ShapeDtypeStruct((b, n, N_BRANCHES * d), BF16)]
        out_specs = [tok(2 * FT_WIDTH)] + [tok(HEAD_W)] * 5 + [tok(2 * HEAD_W), tok(N_BRANCHES * d)]
    args = [x, sh, sc, norm_g, w_in, consts["chan_dft"], consts["g64"], consts["g32"],
            gains["naq"], gains["nak"], gains["dfq"], gains["dfk"], cos_t, sin_t]
    in_specs = [tok(d), mod, mod, full(norm_g), full(w_in), full(consts["chan_dft"]),
                full(consts["g64"]), full(consts["g32"]), full(gains["naq"]), full(gains["nak"]),
                full(gains["dfq"]), full(gains["dfk"]), tab, tab]
    return pl.pallas_call(
        functools.partial(_inproj_kernel, rope=rope, kv_only=kv_only),
        grid=(b, n // tm),
        in_specs=in_specs,
        out_specs=out_specs,
        out_shape=out_shape,
        compiler_params=_cparams(("parallel", "parallel")),
        name="inproj_kv" if kv_only else "inproj",
    )(*args)


def _fourier_kernel(c_ref, s_ref, ab_ref, o_ref, acc_ref, *, scale):
    k = pl.program_id(1)

    @pl.when(k == 0)
    def _():
        acc_ref[...] = jnp.zeros_like(acc_ref)

    cm = c_ref[...]
    sm = s_ref[...]
    for bi in range(ab_ref.shape[0]):
        acc_ref[bi] += _dot(cm, ab_ref[bi, :, :FT_WIDTH]) + _dot(sm, ab_ref[bi, :, FT_WIDTH:])

    @pl.when(k == pl.num_programs(1) - 1)
    def _():
        o_ref[...] = (acc_ref[...] * scale).astype(o_ref.dtype)


def _fourier(fab, cos_m, nsin_m):
    b, n, _ = fab.shape
    tn = min(1024, n)
    tk = min(512, n)
    scale = 1.0 / math.sqrt(n * FT_GROUP_DIM)
    return pl.pallas_call(
        functools.partial(_fourier_kernel, scale=scale),
        grid=(n // tn, n // tk),
        in_specs=[
            pl.BlockSpec((tn, tk), lambda i, k: (i, k)),
            pl.BlockSpec((tn, tk), lambda i, k: (i, k)),
            pl.BlockSpec((b, tk, 2 * FT_WIDTH), lambda i, k: (0, k, 0)),
        ],
        out_specs=pl.BlockSpec((b, tn, FT_WIDTH), lambda i, k: (0, i, 0)),
        out_shape=jax.ShapeDtypeStruct((b, n, FT_WIDTH), BF16),
        scratch_shapes=[pltpu.VMEM((b, tn, FT_WIDTH), F32)],
        compiler_params=_cparams(("parallel", "arbitrary")),
        name="fourier",
    )(cos_m, nsin_m, fab)


NA_ROW_GROUP = 4


def _na_kernel(pid_ref, q_ref, k_ref, v_ref, kc_ref, vc_ref, bias_ref, o_ref, *, rows, kh):
    del pid_ref
    g = pl.program_id(1)
    rq = NA_ROW_GROUP * GRID_W
    key_rows = NA_ROW_GROUP + kh - 1
    u = jnp.clip(NA_ROW_GROUP * g - kh // 2, 0, rows - key_rows)
    start = pl.multiple_of(u * GRID_W, GRID_W)
    lane = lax.broadcasted_iota(jnp.int32, (rq, LANES), 1)
    first = lane < NA_HEAD_DIM
    for pr in range(N_PAIRS):
        cols = slice(pr * LANES, (pr + 1) * LANES)
        q2 = q_ref[0, :, cols]
        zero = jnp.zeros_like(q2)
        qcat = jnp.concatenate([jnp.where(first, q2, zero), jnp.where(first, zero, q2)], axis=0)
        kb = k_ref[0, pl.ds(start, key_rows * GRID_W), cols]
        vb = v_ref[0, pl.ds(start, key_rows * GRID_W), cols]
        s1 = _dot_nt(qcat, kb) + bias_ref[0, pr]
        s2 = _dot_nt(qcat, kc_ref[0, :, cols])
        m = jnp.maximum(jnp.max(s1, axis=-1, keepdims=True), jnp.max(s2, axis=-1, keepdims=True))
        p1 = jnp.exp(s1 - m)
        p2 = jnp.exp(s2 - m)
        l = jnp.sum(p1, axis=-1, keepdims=True) + jnp.sum(p2, axis=-1, keepdims=True)
        o = (_dot(p1.astype(BF16), vb) + _dot(p2.astype(BF16), vc_ref[0, :, cols])) / l
        o_ref[0, :, cols] = jnp.where(first, o[:rq], o[rq:]).astype(BF16)


def _na_attention(nq, nk, nv, nkc, nvc, bias, pattern_ids):
    b, n, _ = nq.shape
    lc = nkc.shape[1]
    rows = n // GRID_W
    kh = min(NA_WIN_H, rows)
    rq = NA_ROW_GROUP * GRID_W
    grid_spec = pltpu.PrefetchScalarGridSpec(
        num_scalar_prefetch=1,
        grid=(b, rows // NA_ROW_GROUP),
        in_specs=[
            pl.BlockSpec((1, rq, HEAD_W), lambda bi, g, pid: (bi, g, 0)),
            pl.BlockSpec((1, n, HEAD_W), lambda bi, g, pid: (bi, 0, 0)),
            pl.BlockSpec((1, n, HEAD_W), lambda bi, g, pid: (bi, 0, 0)),
            pl.BlockSpec((1, lc, HEAD_W), lambda bi, g, pid: (bi, 0, 0)),
            pl.BlockSpec((1, lc, HEAD_W), lambda bi, g, pid: (bi, 0, 0)),
            pl.BlockSpec((1,) + bias.shape[1:], lambda bi, g, pid: (pid[g], 0, 0, 0)),
        ],
        out_specs=pl.BlockSpec((1, rq, HEAD_W), lambda bi, g, pid: (bi, g, 0)),
    )
    return pl.pallas_call(
        functools.partial(_na_kernel, rows=rows, kh=kh),
        grid_spec=grid_spec,
        out_shape=jax.ShapeDtypeStruct((b, n, HEAD_W), BF16),
        compiler_params=_cparams(("parallel", "arbitrary")),
        name="na_attention",
    )(pattern_ids, nq, nk, nv, nkc, nvc, bias)


def _na_patterns(rows):
    kh = min(NA_WIN_H, rows)
    key_rows = NA_ROW_GROUP + kh - 1
    assert rows % NA_ROW_GROUP == 0 and rows >= key_rows
    patterns, ids = [], []
    for g in range(rows // NA_ROW_GROUP):
        u = min(max(NA_ROW_GROUP * g - kh // 2, 0), rows - key_rows)
        geo = []
        for r in range(NA_ROW_GROUP * g, NA_ROW_GROUP * (g + 1)):
            rs = min(max(r - kh // 2, 0), rows - kh)
            geo.append((r - u, rs - u))
        geo = tuple(geo)
        if geo not in patterns:
            patterns.append(geo)
        ids.append(patterns.index(geo))
    return patterns, ids


def _na_bias_table(rpb, rows):
    kh = min(NA_WIN_H, rows)
    key_rows = NA_ROW_GROUP + kh - 1
    patterns, ids = _na_patterns(rows)
    geo = jnp.asarray(patterns, dtype=jnp.int32)
    rq_off, win_off = geo[..., 0], geo[..., 1]
    a = jnp.arange(key_rows)
    row_ok = (a >= win_off[..., None]) & (a < win_off[..., None] + kh)
    row_idx = jnp.clip(a - rq_off[..., None] + NA_WIN_H - 1, 0, 2 * NA_WIN_H - 2)
    cols = jnp.arange(GRID_W)
    col_start = jnp.clip(cols - NA_WIN_W // 2, 0, GRID_W - NA_WIN_W)
    kc = jnp.arange(GRID_W)
    col_ok = (kc[None, :] >= col_start[:, None]) & (kc[None, :] < col_start[:, None] + NA_WIN_W)
    col_idx = jnp.clip(kc[None, :] - cols[:, None] + NA_WIN_W - 1, 0, 2 * NA_WIN_W - 2)
    t = rpb[:, row_idx]
    pick = (jnp.arange(2 * NA_WIN_W - 1)[:, None, None] == col_idx[None]).astype(F32)
    t = jnp.einsum('hpraj,jck->hprack', t, pick, precision=lax.Precision.HIGHEST)
    ok = row_ok[None, :, :, :, None, None] & col_ok[None, None, None, None]
    t = jnp.where(ok, t, MASK_VALUE)
    t = t.transpose(1, 0, 2, 4, 3, 5)
    n_pat = len(patterns)
    t = t.reshape(n_pat, N_PAIRS, 2 * NA_ROW_GROUP * GRID_W, key_rows * GRID_W)
    return t.astype(F32), jnp.asarray(ids, dtype=jnp.int32)


def _ctx_na_kernel(q_ref, k_ref, v_ref, o_ref):
    tq = q_ref.shape[1]
    lane = lax.broadcasted_iota(jnp.int32, (tq, LANES), 1)
    for pr in range(N_PAIRS):
        cols = slice(pr * LANES, (pr + 1) * LANES)
        q2 = q_ref[0, :, cols]
        kb = k_ref[0, :, cols]
        vb = v_ref[0, :, cols]
        acc = jnp.zeros((tq, LANES), F32)
        for hh in range(2):
            in_head = (lane >= hh * NA_HEAD_DIM) & (lane < (hh + 1) * NA_HEAD_DIM)
            qm = jnp.where(in_head, q2, jnp.zeros_like(q2))
            s = _dot_nt(qm, kb)
            p = jnp.exp(s - jnp.max(s, axis=-1, keepdims=True))
            l = jnp.sum(p, axis=-1, keepdims=True)
            acc = jnp.where(in_head, _dot(p.astype(BF16), vb) / l, acc)
        o_ref[0, :, cols] = acc.astype(BF16)


def _ctx_na_attention(q, k, v):
    b, n, _ = q.shape
    spec = pl.BlockSpec((1, n, HEAD_W), lambda bi: (bi, 0, 0))
    return pl.pallas_call(
        _ctx_na_kernel,
        grid=(b,),
        in_specs=[spec, spec, spec],
        out_specs=spec,
        out_shape=jax.ShapeDtypeStruct((b, n, HEAD_W), BF16),
        compiler_params=_cparams(("parallel",)),
        name="ctx_na_attention",
    )(q, k, v)


def _diff_kernel(q_ref, lam_ref, g_ref, *refs, lam_init, n_src):
    src_refs, o_ref, k_ref, v_ref = refs[:2 * n_src], refs[2 * n_src], refs[-2], refs[-1]

    @pl.when(pl.program_id(1) == 0)
    def _():
        off = 0
        for j in range(n_src):
            n = src_refs[2 * j].shape[1]
            k_ref[off:off + n] = src_refs[2 * j][0]
            v_ref[off:off + n] = src_refs[2 * j + 1][0]
            off += n

    tq = q_ref.shape[1]
    lp = lam_ref[...]
    lam = (jnp.exp(jnp.sum(lp[0:1] * lp[1:2], axis=-1, keepdims=True))
           - jnp.exp(jnp.sum(lp[2:3] * lp[3:4], axis=-1, keepdims=True)) + lam_init)
    lane = lax.broadcasted_iota(jnp.int32, (tq, LANES), 1)
    for pr in range(N_PAIRS):
        cols = slice(pr * LANES, (pr + 1) * LANES)
        q2 = q_ref[0, :, cols]
        zero = jnp.zeros_like(q2)
        outp = jnp.zeros((tq, LANES), F32)
        for hh in range(2):
            comp = []
            for c in range(2):
                lo = hh * DF_V_DIM + c * DF_QK_DIM
                qm = jnp.where((lane >= lo) & (lane < lo + DF_QK_DIM), q2, zero)
                s = _dot_nt(qm, k_ref[:, cols])
                p = jnp.exp2(s - jnp.max(s, axis=-1, keepdims=True)).astype(BF16)
                res = _dot(p, v_ref[:, 2 * pr * LANES:2 * (pr + 1) * LANES])
                comp.append(res[:, :LANES] / res[:, LANES:LANES + 1])
            in_head = (lane >= hh * DF_V_DIM) & (lane < (hh + 1) * DF_V_DIM)
            oh = jnp.where(in_head, comp[0] - lam * comp[1], 0.0)
            ms = jnp.sum(oh * oh, axis=-1, keepdims=True) * (1.0 / DF_V_DIM)
            outp = outp + oh * lax.rsqrt(ms + EPS) * g_ref[...] * (1.0 - lam_init)
        o_ref[0, :, cols] = outp.astype(BF16)


def _diff_attention(dq, kvs, lam_p, subln_g, lam_init):
    b, nq, _ = dq.shape
    tq = min(256, nq)
    nk = sum(k.shape[1] for k, _ in kvs)
    kv_specs, kv_args = [], []
    for k, v in kvs:
        kv_specs += [pl.BlockSpec((1, k.shape[1], HEAD_W), lambda bi, i: (bi, 0, 0)),
                     pl.BlockSpec((1, k.shape[1], 2 * HEAD_W), lambda bi, i: (bi, 0, 0))]
        kv_args += [k, v]
    return pl.pallas_call(
        functools.partial(_diff_kernel, lam_init=lam_init, n_src=len(kvs)),
        grid=(b, nq // tq),
        in_specs=[
            pl.BlockSpec((1, tq, HEAD_W), lambda bi, i: (bi, i, 0)),
            pl.BlockSpec(lam_p.shape, lambda bi, i: (0, 0)),
            pl.BlockSpec(subln_g.shape, lambda bi, i: (0, 0)),
        ] + kv_specs,
        out_specs=pl.BlockSpec((1, tq, HEAD_W), lambda bi, i: (bi, i, 0)),
        out_shape=jax.ShapeDtypeStruct((b, nq, HEAD_W), BF16),
        scratch_shapes=[pltpu.VMEM((nk, HEAD_W), BF16), pltpu.VMEM((nk, 2 * HEAD_W), BF16)],
        compiler_params=_cparams(("parallel", "arbitrary")),
        name="diff_attention",
    )(dq, lam_p, subln_g, *kv_args)


def _merge_kernel(x_ref, f_ref, na_ref, df_ref, gate_ref, g1_ref, wft_ref, wna_ref, wdf_ref, wout_ref,
                  ng_ref, sh_ref, sc_ref, wrh_ref, wrl_ref, xo_ref, h2_ref, lg_ref):
    d = x_ref.shape[2]
    y_ft = _dot(f_ref[0], wft_ref[...])
    y_na = _dot(na_ref[0], wna_ref[...])
    y_df = _dot(df_ref[0], wdf_ref[...])
    m = (gate_ref[0, :, 0:d].astype(F32) * y_ft + gate_ref[0, :, d:2 * d].astype(F32) * y_na
         + gate_ref[0, :, 2 * d:3 * d].astype(F32) * y_df)
    y = _dot(m.astype(BF16), wout_ref[...])
    xn = x_ref[0] + g1_ref[0] * y
    xo_ref[0] = xn
    ms = jnp.mean(xn * xn, axis=-1, keepdims=True)
    h2 = (xn * lax.rsqrt(ms + EPS) * ng_ref[...]) * (1.0 + sc_ref[0]) + sh_ref[0]
    h_hi, h_lo = _split_bf16(h2)
    h2_ref[0] = h_hi
    lg_ref[0] = _dot(h_hi, wrh_ref[...]) + _dot(h_lo, wrh_ref[...]) + _dot(h_hi, wrl_ref[...])


def _merge(x, f, o_na, o_df, gates, g1, w_ft, w_na_o, w_df_o, w_out, norm_g, sh2, sc2, wr_hi, wr_lo):
    b, n, d = x.shape
    tm = min(512, n)
    tok = lambda w: pl.BlockSpec((1, tm, w), lambda bi, i: (bi, i, 0))
    full = lambda a: pl.BlockSpec(a.shape, lambda bi, i: (0,) * a.ndim)
    mod = pl.BlockSpec((1, 1, d), lambda bi, i: (bi, 0, 0))
    return pl.pallas_call(
        _merge_kernel,
        grid=(b, n // tm),
        in_specs=[tok(d), tok(FT_WIDTH), tok(HEAD_W), tok(HEAD_W), tok(N_BRANCHES * d), mod,
                  full(w_ft), full(w_na_o), full(w_df_o), full(w_out), full(norm_g), mod, mod,
                  full(wr_hi), full(wr_lo)],
        out_specs=[tok(d), tok(d), tok(ROUTER_PAD)],
        out_shape=[jax.ShapeDtypeStruct((b, n, d), F32), jax.ShapeDtypeStruct((b, n, d), BF16),
                   jax.ShapeDtypeStruct((b, n, ROUTER_PAD), F32)],
        compiler_params=_cparams(("parallel", "parallel")),
        name="merge",
    )(x, f, o_na, o_df, gates, g1, w_ft, w_na_o, w_df_o, w_out, norm_g, sh2, sc2, wr_hi, wr_lo)


ROUTE_TILE = LANES
SLOT_ALIGN = 16
ONE_BITS = 0x3F800000


FAST_WINDOW = 48


def _slot_window(cap):
    return min(ROUTE_TILE + SLOT_ALIGN, cap)


def _fast_window(cap):
    return min(FAST_WINDOW, cap)


def _slot_windows(lo, cap):
    b, nt, _ = lo.shape
    filled = jnp.concatenate([lo[:, 1:], jnp.full((b, 1, N_EXPERTS), cap, lo.dtype)], axis=1)
    aligned = (lo // SLOT_ALIGN) * SLOT_ALIGN
    lo_slow = jnp.minimum(aligned, cap - _slot_window(cap))
    lo_fast = jnp.minimum(aligned, cap - _fast_window(cap))
    fast_ok = jnp.all(filled - lo_fast <= _fast_window(cap), axis=-1)
    return {"lo_slow": lo_slow.reshape(-1), "lo_fast": lo_fast.reshape(-1),
            "fast_ok": fast_ok.astype(jnp.int32).reshape(-1)}


def _route_kernel(lg_ref, pos_ref, aff_ref, lo_ref, bits_ref, sel_ref, *, cap):
    n = lg_ref.shape[1]
    nb = n // ROUTE_TILE
    lane = lax.broadcasted_iota(jnp.int32, (n, LANES), 1)
    z = jnp.where(lane < N_EXPERTS, lg_ref[0], MASK_VALUE)
    p = jnp.exp(z - jnp.max(z, axis=-1, keepdims=True))
    aff = p / jnp.sum(p, axis=-1, keepdims=True)
    aff_ref[0] = aff
    bits_ref[...] = pltpu.bitcast(aff, jnp.int32)

    def bisect(_, carry):
        lo, hi = carry
        mid = (lo + hi) >> 1
        cnt = jnp.sum(jnp.where(bits_ref[...] >= mid, 1.0, 0.0), axis=0, keepdims=True)
        ge = cnt >= cap
        return jnp.where(ge, mid, lo), jnp.where(ge, hi, mid)

    lo0 = jnp.zeros((1, LANES), jnp.int32)
    hi0 = jnp.full((1, LANES), ONE_BITS + 1, jnp.int32)
    thr, _ = lax.fori_loop(0, 31, bisect, (lo0, hi0))
    n_above = jnp.sum(jnp.where(bits_ref[...] > thr, 1.0, 0.0), axis=0, keepdims=True)
    need = cap - n_above

    row = lax.broadcasted_iota(jnp.int32, (ROUTE_TILE, ROUTE_TILE), 0)
    col = lax.broadcasted_iota(jnp.int32, (ROUTE_TILE, ROUTE_TILE), 1)
    tri = jnp.where(row >= col, 1.0, 0.0).astype(BF16)

    carry = jnp.zeros((1, LANES), F32)
    for blk in range(nb):
        rows = slice(blk * ROUTE_TILE, (blk + 1) * ROUTE_TILE)
        bb = bits_ref[rows]
        eq = jnp.where(bb == thr, 1.0, 0.0)
        incl = _dot(tri, eq.astype(BF16))
        before = incl - eq + carry
        take = jnp.where(before < need, eq, 0.0)
        sel_ref[rows] = jnp.where(bb > thr, 1.0, take)
        carry = carry + incl[ROUTE_TILE - 1:ROUTE_TILE]

    carry = jnp.zeros((1, LANES), F32)
    for blk in range(nb):
        rows = slice(blk * ROUTE_TILE, (blk + 1) * ROUTE_TILE)
        sel = sel_ref[rows]
        incl = _dot(tri, sel.astype(BF16))
        pos_ref[0, rows] = jnp.where(sel > 0.0, carry + incl - sel, -1.0)
        lo_ref[0, blk:blk + 1] = carry.astype(jnp.int32)
        carry = carry + incl[ROUTE_TILE - 1:ROUTE_TILE]


def _route(logits, cap):
    b, n, _ = logits.shape
    nt = n // ROUTE_TILE
    tok = pl.BlockSpec((1, n, LANES), lambda bi: (bi, 0, 0))
    return pl.pallas_call(
        functools.partial(_route_kernel, cap=cap),
        grid=(b,),
        in_specs=[tok],
        out_specs=[tok, tok, pl.BlockSpec((1, nt, LANES), lambda bi: (bi, 0, 0))],
        out_shape=[jax.ShapeDtypeStruct((b, n, LANES), F32), jax.ShapeDtypeStruct((b, n, LANES), F32),
                   jax.ShapeDtypeStruct((b, nt, LANES), jnp.int32)],
        scratch_shapes=[pltpu.VMEM((n, LANES), jnp.int32), pltpu.VMEM((n, LANES), F32)],
        compiler_params=_cparams(("parallel",)),
        name="route",
    )(logits)


DISPATCH_COLS = 256


def _dispatch_kernel(lo_fast_ref, lo_slow_ref, fast_ref, h_ref, pos_ref, o_ref, *, win_fast, win_slow):
    b = pl.program_id(0)
    t = pl.program_id(1)
    nt = pl.num_programs(1)

    @pl.when(t == 0)
    def _():
        o_ref[...] = jnp.zeros_like(o_ref)

    def run(win, lo_ref):
        slot = lax.broadcasted_iota(jnp.int32, (win, ROUTE_TILE), 0).astype(F32)
        pos_t = pos_ref[0, 0]
        los = [pl.multiple_of(lo_ref[(b * nt + t) * N_EXPERTS + e], SLOT_ALIGN) for e in range(N_EXPERTS)]
        onehot = jnp.concatenate(
            [jnp.where(pos_t[e:e + 1, :] - los[e].astype(F32) == slot, 1.0, 0.0).astype(BF16)
             for e in range(N_EXPERTS)], axis=0)
        for c0 in range(0, h_ref.shape[2], DISPATCH_COLS):
            cols = slice(c0, c0 + DISPATCH_COLS)
            res = _dot(onehot, h_ref[0, :, cols]).astype(BF16)
            for e in range(N_EXPERTS):
                rows = pl.ds(los[e], win)
                o_ref[0, e, rows, cols] = o_ref[0, e, rows, cols] + res[e * win:(e + 1) * win]

    if win_fast == win_slow:
        run(win_slow, lo_slow_ref)
    else:
        fast = fast_ref[b * nt + t] == 1
        pl.when(fast)(functools.partial(run, win_fast, lo_fast_ref))
        pl.when(jnp.logical_not(fast))(functools.partial(run, win_slow, lo_slow_ref))


def _dispatch(windows, h2, pos_t, cap):
    b, n, d = h2.shape
    nt = n // ROUTE_TILE
    grid_spec = pltpu.PrefetchScalarGridSpec(
        num_scalar_prefetch=3,
        grid=(b, nt),
        in_specs=[
            pl.BlockSpec((1, ROUTE_TILE, d), lambda bi, t, *_: (bi, t, 0)),
            pl.BlockSpec((1, 1, N_EXPERTS, ROUTE_TILE), lambda bi, t, *_: (bi, t, 0, 0)),
        ],
        out_specs=pl.BlockSpec((1, N_EXPERTS, cap, d), lambda bi, t, *_: (bi, 0, 0, 0)),
    )
    return pl.pallas_call(
        functools.partial(_dispatch_kernel, win_fast=_fast_window(cap), win_slow=_slot_window(cap)),
        grid_spec=grid_spec,
        out_shape=jax.ShapeDtypeStruct((b, N_EXPERTS, cap, d), BF16),
        compiler_params=_cparams(("arbitrary", "arbitrary")),
        name="dispatch",
    )(windows["lo_fast"], windows["lo_slow"], windows["fast_ok"], h2, pos_t)


def _expert_kernel(*refs, f_chunk, n_sets):
    x_refs, (wg_ref, wu_ref, wd_ref), o_refs = refs[:n_sets], refs[n_sets:n_sets + 3], refs[n_sets + 3:]
    xs = [r[0, 0] for r in x_refs]
    x = xs[0] if n_sets == 1 else jnp.concatenate(xs, axis=0)
    ff = wg_ref.shape[3]
    acc = jnp.zeros(x.shape, F32)
    for f0 in range(0, ff, f_chunk):
        a = _dot(x, wg_ref[0, 0, :, f0:f0 + f_chunk])
        u = _dot(x, wu_ref[0, 0, :, f0:f0 + f_chunk])
        hm = (a * jax.nn.sigmoid(a) * u).astype(BF16)
        acc = acc + _dot(hm, wd_ref[0, 0, f0:f0 + f_chunk, :])
    off = 0
    for xr, o_ref in zip(xs, o_refs):
        o_ref[0, 0] = acc[off:off + xr.shape[0]].astype(BF16)
        off += xr.shape[0]


def _experts(xes, w_gate, w_up, w_down, layer):
    b, e, _, d = xes[0].shape
    ff = w_gate.shape[3]
    toks = [pl.BlockSpec((1, 1, xe.shape[2], d), lambda ei, bi: (bi, ei, 0, 0)) for xe in xes]
    return pl.pallas_call(
        functools.partial(_expert_kernel, f_chunk=min(512, ff), n_sets=len(xes)),
        grid=(e, b),
        in_specs=toks + [
            pl.BlockSpec((1, 1, d, ff), lambda ei, bi: (layer, ei, 0, 0)),
            pl.BlockSpec((1, 1, d, ff), lambda ei, bi: (layer, ei, 0, 0)),
            pl.BlockSpec((1, 1, ff, d), lambda ei, bi: (layer, ei, 0, 0)),
        ],
        out_specs=toks,
        out_shape=[jax.ShapeDtypeStruct(xe.shape, BF16) for xe in xes],
        compiler_params=_cparams(("arbitrary", "arbitrary")),
        name="experts",
    )(*xes, w_gate, w_up, w_down)


def _combine_kernel(lo_ref, x_ref, y_ref, pos_ref, aff_ref, g_ref, o_ref, *, win):
    b = pl.program_id(0)
    t = pl.program_id(1)
    nt = pl.num_programs(1)
    tile = x_ref.shape[1]
    slot = lax.broadcasted_iota(jnp.int32, (tile, win), 1).astype(F32)
    pos = pos_ref[0]
    aff = aff_ref[0]
    acc = jnp.zeros(x_ref.shape[1:], F32)
    for e in range(N_EXPERTS):
        lo = pl.multiple_of(lo_ref[(b * nt + t) * N_EXPERTS + e], SLOT_ALIGN)
        onehot = jnp.where(pos[:, e:e + 1] - lo.astype(F32) == slot, 1.0, 0.0).astype(BF16)
        acc = acc + aff[:, e:e + 1] * _dot(onehot, y_ref[0, e, pl.ds(lo, win), :])
    o_ref[0] = x_ref[0] + g_ref[0] * acc


def _combine(lo_flat, x, ye, rel, aff, g2, cap):
    b, n, d = x.shape
    e = ye.shape[1]
    nt = n // ROUTE_TILE
    tok = lambda w: pl.BlockSpec((1, ROUTE_TILE, w), lambda bi, t, lo: (bi, t, 0))
    grid_spec = pltpu.PrefetchScalarGridSpec(
        num_scalar_prefetch=1,
        grid=(b, nt),
        in_specs=[
            tok(d),
            pl.BlockSpec((1, e, cap, d), lambda bi, t, lo: (bi, 0, 0, 0)),
            tok(LANES),
            tok(LANES),
            pl.BlockSpec((1, 1, d), lambda bi, t, lo: (bi, 0, 0)),
        ],
        out_specs=tok(d),
    )
    return pl.pallas_call(
        functools.partial(_combine_kernel, win=_slot_window(cap)),
        grid_spec=grid_spec,
        out_shape=jax.ShapeDtypeStruct((b, n, d), F32),
        compiler_params=_cparams(("arbitrary", "arbitrary")),
        name="combine",
    )(lo_flat, x, ye, rel, aff, g2)


def _route_and_dispatch(h2, logits):
    b, n, _ = h2.shape
    cap = EC_CAPACITY_FACTOR * n // N_EXPERTS
    nt = n // ROUTE_TILE
    assert n % ROUTE_TILE == 0 and cap % SLOT_ALIGN == 0
    pos, aff, lo = _route(logits, cap)
    windows = _slot_windows(lo[:, :, :N_EXPERTS], cap)
    pos_t = pos[:, :, :N_EXPERTS].reshape(b, nt, ROUTE_TILE, N_EXPERTS).transpose(0, 1, 3, 2)
    xe = _dispatch(windows, h2, pos_t, cap)
    return xe, {"lo": windows["lo_slow"], "pos": pos, "aff": aff, "cap": cap}


def _combine_residual(x, ye, routing, g2):
    return _combine(routing["lo"], x, ye, routing["pos"], routing["aff"], g2, routing["cap"])


def _dft_mats(n):
    k = jnp.arange(n, dtype=jnp.int32)
    ang = ((k[:, None] * k[None, :]) % n).astype(F32) * (2.0 * math.pi / n)
    return jnp.cos(ang), jnp.sin(ang)


def _dft_mats_bf16(n, block=64):
    assert n % block == 0
    k = jnp.arange(n, dtype=jnp.int32)
    a = jnp.arange(n // block, dtype=jnp.int32) * block
    b = jnp.arange(block, dtype=jnp.int32)
    ang_a = ((a[:, None] * k[None, :]) % n).astype(F32) * (2.0 * math.pi / n)
    ang_b = ((b[:, None] * k[None, :]) % n).astype(F32) * (2.0 * math.pi / n)
    ca, sa = jnp.cos(ang_a)[:, None, :], jnp.sin(ang_a)[:, None, :]
    cb, sb = jnp.cos(ang_b)[None], jnp.sin(ang_b)[None]
    cos_m = (ca * cb - sa * sb).reshape(n, n)
    nsin_m = (-(sa * cb + ca * sb)).reshape(n, n)
    return cos_m.astype(BF16), nsin_m.astype(BF16)


def _chan_dft():
    c, s = _dft_mats(FT_GROUP_DIM)
    eye = jnp.eye(FT_GROUPS, dtype=F32)
    return jnp.concatenate([jnp.kron(eye, c), jnp.kron(eye, s)], axis=1).astype(BF16)


def _group_mean_mat(group):
    gid = jnp.arange(HEAD_W) // group
    return jnp.where(gid[:, None] == gid[None, :], 1.0 / group, 0.0).astype(BF16)


def _rope_tables(n):
    t = jnp.arange(n)
    row = (t // GRID_W).astype(F32)
    col = (t % GRID_W).astype(F32)
    ax = DF_QK_DIM // 2
    inv = ROPE_BASE ** (-jnp.arange(0, ax, 2, dtype=F32) / ax)
    lane = jnp.arange(LANES)
    freq = inv[lane % (ax // 2)]
    pos = jnp.where(((lane % DF_QK_DIM) < ax)[None, :], row[:, None], col[:, None])
    ang = pos * freq[None, :]
    sign = jnp.where((lane % ax) < ax // 2, -1.0, 1.0)
    return jnp.cos(ang), jnp.sin(ang) * sign[None, :]


def kernel(x, c, ctx, c_ctx, norm1_g, norm2_g, w_ada, b_ada, w_in, na_qn_g, na_kn_g, na_rpb, df_qn_g, df_kn_g,
           df_lambda, df_subln_g, w_ft, w_na_o, w_df_o, w_out, w_router, w_gate, w_up, w_down):
    b, n, d = x.shape
    lc = ctx.shape[1]
    depth = w_ada.shape[0]
    assert b + 1 <= MOD_ROWS and n % GRID_W == 0

    cc = jnp.zeros((MOD_ROWS, d), F32).at[:b].set(c).at[b].set(c_ctx)
    mods = _ada(cc, w_ada, b_ada)

    consts = {"chan_dft": _chan_dft(), "g64": _group_mean_mat(NA_HEAD_DIM), "g32": _group_mean_mat(DF_QK_DIM)}
    rope_lat = _rope_tables(n)
    rope_ctx = (jnp.zeros((lc, LANES), F32), jnp.zeros((lc, LANES), F32))
    dft_lat = _dft_mats_bf16(n)
    dft_ctx = _dft_mats_bf16(lc)
    rows = n // GRID_W

    w_gate_b, w_up_b, w_down_b = w_gate.astype(BF16), w_up.astype(BF16), w_down.astype(BF16)

    xc = ctx
    for i in range(depth):
        last = i == depth - 1
        lam_init = 0.8 - 0.6 * math.exp(-0.3 * i)
        m_lat = mods[i, :b].reshape(b, 6, 1, d)
        m_ctx = jnp.broadcast_to(mods[i, b].reshape(1, 6, 1, d), (b, 6, 1, d))
        sh1, sc1, g1, sh2, sc2, g2 = [m_lat[:, j] for j in range(6)]
        csh1, csc1, cg1, csh2, csc2, cg2 = [m_ctx[:, j] for j in range(6)]

        w_in_b = w_in[i].astype(BF16)
        gains = {
            "naq": jnp.tile(na_qn_g[i], NA_HEADS).reshape(1, HEAD_W),
            "nak": jnp.tile(na_kn_g[i], NA_HEADS).reshape(1, HEAD_W),
            "dfq": jnp.tile(df_qn_g[i], 2 * DF_HEADS).reshape(1, HEAD_W),
            "dfk": jnp.tile(df_kn_g[i], 2 * DF_HEADS).reshape(1, HEAD_W),
        }
        n1g = norm1_g[i].reshape(1, d)
        n2g = norm2_g[i].reshape(1, d)
        subln = jnp.tile(df_subln_g[i], 2).reshape(1, LANES)
        w_ft_b, w_na_b, w_df_b, w_out_b = (w.astype(BF16) for w in (w_ft[i], w_na_o[i], w_df_o[i], w_out[i]))
        wr = jnp.zeros((d, ROUTER_PAD), F32).at[:, :N_EXPERTS].set(w_router[i])
        wr_hi, wr_lo = _split_bf16(wr)

        if last:
            nkc, nvc, dkc, dvc = _inproj(xc, csh1, csc1, n1g, w_in_b, consts, gains, rope_ctx,
                                         rope=False, kv_only=True)
        else:
            fabc, nqc, dqc, nkc, nvc, dkc, dvc, gatec = _inproj(xc, csh1, csc1, n1g, w_in_b, consts, gains,
                                                                 rope_ctx, rope=False, kv_only=False)

        fab, nq, dq, nk, nv, dk, dv, gate = _inproj(x, sh1, sc1, n1g, w_in_b, consts, gains, rope_lat,
                                                     rope=True, kv_only=False)
        f = _fourier(fab, *dft_lat)
        o_na = _na_attention(nq, nk, nv, nkc, nvc, *_na_bias_table(na_rpb[i], rows))
        o_df = _diff_attention(dq, [(dk, dv), (dkc, dvc)], df_lambda[i], subln, lam_init)
        x, h2, logits = _merge(x, f, o_na, o_df, gate, g1, w_ft_b, w_na_b, w_df_b, w_out_b, n2g, sh2, sc2,
                               wr_hi, wr_lo)
        xe, routing = _route_and_dispatch(h2, logits)

        if last:
            (ye,) = _experts([xe], w_gate_b, w_up_b, w_down_b, i)
        else:
            fc = _fourier(fabc, *dft_ctx)
            o_nac = _ctx_na_attention(nqc, nkc, nvc)
            o_dfc = _diff_attention(dqc, [(dkc, dvc)], df_lambda[i], subln, lam_init)
            xc, hc2, logits_c = _merge(xc, fc, o_nac, o_dfc, gatec, cg1, w_ft_b, w_na_b, w_df_b, w_out_b, n2g,
                                       csh2, csc2, wr_hi, wr_lo)
            xec, routing_c = _route_and_dispatch(hc2, logits_c)
            ye, yec = _experts([xe, xec], w_gate_b, w_up_b, w_down_b, i)
            xc = _combine_residual(xc, yec, routing_c, cg2)
        x = _combine_residual(x, ye, routing, g2)
    return x
```

```python
import functools
import math

import jax
import jax.numpy as jnp
from jax import lax
from jax.experimental import pallas as pl
from jax.experimental.pallas import tpu as pltpu

F32 = jnp.float32
BF16 = jnp.bfloat16

GRID_W = 64
FT_GROUPS = 4
FT_GROUP_DIM = 64
FT_WIDTH = FT_GROUPS * FT_GROUP_DIM
NA_HEADS = 6
NA_HEAD_DIM = 64
NA_WIDTH = NA_HEADS * NA_HEAD_DIM
NA_WIN_H = 8
NA_WIN_W = 16
DF_HEADS = 6
DF_QK_DIM = 32
DF_V_DIM = 2 * DF_QK_DIM
DF_QK_WIDTH = DF_HEADS * 2 * DF_QK_DIM
DF_WIDTH = DF_HEADS * DF_V_DIM
N_BRANCHES = 3
N_EXPERTS = 16
EC_CAPACITY_FACTOR = 2
ROPE_BASE = 10000.0
EPS = 1e-6
MASK_VALUE = -1e30

LANES = 128
VMEM_LIMIT_BYTES = 56 * 1024 * 1024

HEAD_W = 384
N_PAIRS = HEAD_W // LANES
MOD_ROWS = 16
ROUTER_PAD = LANES


def _cparams(sem):
    return pltpu.CompilerParams(dimension_semantics=sem, vmem_limit_bytes=VMEM_LIMIT_BYTES)


def _dot(a, b):
    return jnp.dot(a, b, preferred_element_type=F32)


def _dot_nt(a, b):
    return lax.dot_general(a, b, (((1,), (1,)), ((), ())), preferred_element_type=F32)


def _split_bf16(v):
    hi = v.astype(BF16)
    lo = (v - hi.astype(F32)).astype(BF16)
    return hi, lo


def _ada_kernel(c_ref, w_ref, b_ref, o_ref):
    c = c_ref[...]
    a = c * jax.nn.sigmoid(c)
    a_hi, a_lo = _split_bf16(a)
    w_hi, w_lo = _split_bf16(w_ref[0])
    acc = _dot(a_hi, w_hi) + _dot(a_lo, w_hi) + _dot(a_hi, w_lo)
    o_ref[0] = acc + b_ref[0]


def _ada(cc, w_ada, b_ada):
    depth, d, d6 = w_ada.shape
    tn = 512
    return pl.pallas_call(
        _ada_kernel,
        grid=(depth, d6 // tn),
        in_specs=[
            pl.BlockSpec((MOD_ROWS, d), lambda l, j: (0, 0)),
            pl.BlockSpec((1, d, tn), lambda l, j: (l, 0, j)),
            pl.BlockSpec((1, 1, tn), lambda l, j: (l, 0, j)),
        ],
        out_specs=pl.BlockSpec((1, MOD_ROWS, tn), lambda l, j: (l, 0, j)),
        out_shape=jax.ShapeDtypeStruct((depth, MOD_ROWS, d6), F32),
        compiler_params=_cparams(("arbitrary", "arbitrary")),
        name="ada",
    )(cc, w_ada, b_ada.reshape(depth, 1, d6))


OFF_Q = FT_WIDTH
OFF_DQ = OFF_Q + NA_WIDTH
OFF_KV = OFF_DQ + DF_QK_WIDTH
OFF_NV = OFF_KV + NA_WIDTH
OFF_DK = OFF_NV + NA_WIDTH
OFF_DV = OFF_DK + DF_QK_WIDTH
OFF_GATE = OFF_DV + DF_WIDTH


def _group_rms(v, gmat_ref, gain_ref):
    ms = _dot((v * v).astype(BF16), gmat_ref[...])
    return v * lax.rsqrt(ms + EPS) * gain_ref[...]


def _rope_chunk(vj, cos, sin_signed, first_half):
    fwd = pltpu.roll(vj, LANES - 8, 1)
    bwd = pltpu.roll(vj, 8, 1)
    partner = jnp.where(first_half, fwd, bwd)
    return vj * cos + partner * sin_signed


def _inproj_kernel(x_ref, sh_ref, sc_ref, ng_ref, w_ref, cd_ref, g64_ref, g32_ref,
                   naq_g_ref, nak_g_ref, dfq_g_ref, dfk_g_ref, cos_ref, sin_ref,
                   *out_refs, rope, kv_only):
    x = x_ref[0]
    ms = jnp.mean(x * x, axis=-1, keepdims=True)
    y = x * lax.rsqrt(ms + EPS) * ng_ref[...]
    h = (y * (1.0 + sc_ref[0]) + sh_ref[0]).astype(BF16)

    def proj(c0, c1):
        return _dot(h, w_ref[:, c0:c1])

    if rope:
        lane = lax.broadcasted_iota(jnp.int32, (x.shape[0], LANES), 1)
        first_half = (lane & 8) == 0
        cos = cos_ref[...]
        sin_signed = sin_ref[...]

    def df_qk(v, gain_ref, scale, o_ref):
        v = _group_rms(v, g32_ref, gain_ref)
        for j in range(N_PAIRS):
            vj = v[:, j * LANES:(j + 1) * LANES]
            if rope:
                vj = _rope_chunk(vj, cos, sin_signed, first_half)
            o_ref[0, :, j * LANES:(j + 1) * LANES] = (vj * scale).astype(BF16)

    if kv_only:
        nk_ref, nv_ref, dk_ref, dv_ref = out_refs
    else:
        fab_ref, nq_ref, dq_ref, nk_ref, nv_ref, dk_ref, dv_ref, gate_ref = out_refs
        u = proj(0, FT_WIDTH)
        fab_ref[0] = _dot(u.astype(BF16), cd_ref[...]).astype(BF16)
        nq = _group_rms(proj(OFF_Q, OFF_DQ), g64_ref, naq_g_ref)
        nq_ref[0] = (nq * (NA_HEAD_DIM ** -0.5)).astype(BF16)
        df_qk(proj(OFF_DQ, OFF_KV), dfq_g_ref, DF_QK_DIM ** -0.5 * math.log2(math.e), dq_ref)
        d = x.shape[1]
        for j in range(N_BRANCHES):
            z = proj(OFF_GATE + j * d, OFF_GATE + (j + 1) * d)
            gate_ref[0, :, j * d:(j + 1) * d] = jax.nn.sigmoid(z).astype(BF16)

    nk_ref[0] = _group_rms(proj(OFF_KV, OFF_NV), g64_ref, nak_g_ref).astype(BF16)
    nv_ref[0] = proj(OFF_NV, OFF_DK).astype(BF16)
    df_qk(proj(OFF_DK, OFF_DV), dfk_g_ref, 1.0, dk_ref)
    dv = proj(OFF_DV, OFF_GATE).astype(BF16)
    for j in range(N_PAIRS):
        dv_ref[0, :, 2 * j * LANES:(2 * j + 1) * LANES] = dv[:, j * LANES:(j + 1) * LANES]
        dv_ref[0, :, (2 * j + 1) * LANES:(2 * j + 2) * LANES] = jnp.ones((dv.shape[0], LANES), BF16)


def _inproj(x, sh, sc, norm_g, w_in, consts, gains, rope_tabs, *, rope, kv_only):
    b, n, d = x.shape
    tm = min(512, n)
    tok = lambda w: pl.BlockSpec((1, tm, w), lambda bi, i: (bi, i, 0))
    full = lambda a: pl.BlockSpec(a.shape, lambda bi, i: (0,) * a.ndim)
    mod = pl.BlockSpec((1, 1, d), lambda bi, i: (bi, 0, 0))
    cos_t, sin_t = rope_tabs
    tab = pl.BlockSpec((tm, LANES), lambda bi, i: (i, 0))
    slab = jax.ShapeDtypeStruct((b, n, HEAD_W), BF16)
    slab_aug = jax.ShapeDtypeStruct((b, n, 2 * HEAD_W), BF16)
    if kv_only:
        out_shape = [slab] * 3 + [slab_aug]
        out_specs = [tok(HEAD_W)] * 3 + [tok(2 * HEAD_W)]
    else:
        out_shape = [jax.ShapeDtypeStruct((b, n, 2 * FT_WIDTH), BF16)] + [slab] * 5 + [
            slab_aug, jax.ShapeDtypeStruct((b, n, N_BRANCHES * d), BF16)]
        out_specs = [tok(2 * FT_WIDTH)] + [tok(HEAD_W)] * 5 + [tok(2 * HEAD_W), tok(N_BRANCHES * d)]
    args = [x, sh, sc, norm_g, w_in, consts["chan_dft"], consts["g64"], consts["g32"],
            gains["naq"], gains["nak"], gains["dfq"], gains["dfk"], cos_t, sin_t]
    in_specs = [tok(d), mod, mod, full(norm_g), full(w_in), full(consts["chan_dft"]),
                full(consts["g64"]), full(consts["g32"]), full(gains["naq"]), full(gains["nak"]),
                full(gains["dfq"]), full(gains["dfk"]), tab, tab]
    return pl.pallas_call(
        functools.partial(_inproj_kernel, rope=rope, kv_only=kv_only),
        grid=(b, n // tm),
        in_specs=in_specs,
        out_specs=out_specs,
        out_shape=out_shape,
        compiler_params=_cparams(("parallel", "parallel")),
        name="inproj_kv" if kv_only else "inproj",
    )(*args)


def _fourier_kernel(c_ref, s_ref, ab_ref, o_ref, acc_ref, *, scale):
    k = pl.program_id(1)

    @pl.when(k == 0)
    def _():
        acc_ref[...] = jnp.zeros_like(acc_ref)

    cm = c_ref[...]
    sm = s_ref[...]
    for bi in range(ab_ref.shape[0]):
        acc_ref[bi] += _dot(cm, ab_ref[bi, :, :FT_WIDTH]) + _dot(sm, ab_ref[bi, :, FT_WIDTH:])

    @pl.when(k == pl.num_programs(1) - 1)
    def _():
        o_ref[...] = (acc_ref[...] * scale).astype(o_ref.dtype)


def _fourier(fab, cos_m, nsin_m):
    b, n, _ = fab.shape
    tn = min(1024, n)
    tk = min(512, n)
    scale = 1.0 / math.sqrt(n * FT_GROUP_DIM)
    return pl.pallas_call(
        functools.partial(_fourier_kernel, scale=scale),
        grid=(n // tn, n // tk),
        in_specs=[
            pl.BlockSpec((tn, tk), lambda i, k: (i, k)),
            pl.BlockSpec((tn, tk), lambda i, k: (i, k)),
            pl.BlockSpec((b, tk, 2 * FT_WIDTH), lambda i, k: (0, k, 0)),
        ],
        out_specs=pl.BlockSpec((b, tn, FT_WIDTH), lambda i, k: (0, i, 0)),
        out_shape=jax.ShapeDtypeStruct((b, n, FT_WIDTH), BF16),
        scratch_shapes=[pltpu.VMEM((b, tn, FT_WIDTH), F32)],
        compiler_params=_cparams(("parallel", "arbitrary")),
        name="fourier",
    )(cos_m, nsin_m, fab)


NA_ROW_GROUP = 4


def _na_kernel(pid_ref, q_ref, k_ref, v_ref, kc_ref, vc_ref, bias_ref, o_ref, *, rows, kh):
    del pid_ref
    g = pl.program_id(1)
    rq = NA_ROW_GROUP * GRID_W
    key_rows = NA_ROW_GROUP + kh - 1
    u = jnp.clip(NA_ROW_GROUP * g - kh // 2, 0, rows - key_rows)
    start = pl.multiple_of(u * GRID_W, GRID_W)
    lane = lax.broadcasted_iota(jnp.int32, (rq, LANES), 1)
    first = lane < NA_HEAD_DIM
    for pr in range(N_PAIRS):
        cols = slice(pr * LANES, (pr + 1) * LANES)
        q2 = q_ref[0, :, cols]
        zero = jnp.zeros_like(q2)
        qcat = jnp.concatenate([jnp.where(first, q2, zero), jnp.where(first, zero, q2)], axis=0)
        kb = k_ref[0, pl.ds(start, key_rows * GRID_W), cols]
        vb = v_ref[0, pl.ds(start, key_rows * GRID_W), cols]
        s1 = _dot_nt(qcat, kb) + bias_ref[0, pr]
        s2 = _dot_nt(qcat, kc_ref[0, :, cols])
        m = jnp.maximum(jnp.max(s1, axis=-1, keepdims=True), jnp.max(s2, axis=-1, keepdims=True))
        p1 = jnp.exp(s1 - m)
        p2 = jnp.exp(s2 - m)
        l = jnp.sum(p1, axis=-1, keepdims=True) + jnp.sum(p2, axis=-1, keepdims=True)
        o = (_dot(p1.astype(BF16), vb) + _dot(p2.astype(BF16), vc_ref[0, :, cols])) / l
        o_ref[0, :, cols] = jnp.where(first, o[:rq], o[rq:]).astype(BF16)


def _na_attention(nq, nk, nv, nkc, nvc, bias, pattern_ids):
    b, n, _ = nq.shape
    lc = nkc.shape[1]
    rows = n // GRID_W
    kh = min(NA_WIN_H, rows)
    rq = NA_ROW_GROUP * GRID_W
    grid_spec = pltpu.PrefetchScalarGridSpec(
        num_scalar_prefetch=1,
        grid=(b, rows // NA_ROW_GROUP),
        in_specs=[
            pl.BlockSpec((1, rq, HEAD_W), lambda bi, g, pid: (bi, g, 0)),
            pl.BlockSpec((1, n, HEAD_W), lambda bi, g, pid: (bi, 0, 0)),
            pl.BlockSpec((1, n, HEAD_W), lambda bi, g, pid: (bi, 0, 0)),
            pl.BlockSpec((1, lc, HEAD_W), lambda bi, g, pid: (bi, 0, 0)),
            pl.BlockSpec((1, lc, HEAD_W), lambda bi, g, pid: (bi, 0, 0)),
            pl.BlockSpec((1,) + bias.shape[1:], lambda bi, g, pid: (pid[g], 0, 0, 0)),
        ],
        out_specs=pl.BlockSpec((1, rq, HEAD_W), lambda bi, g, pid: (bi, g, 0)),
    )
    return pl.pallas_call(
        functools.partial(_na_kernel, rows=rows, kh=kh),
        grid_spec=grid_spec,
        out_shape=jax.ShapeDtypeStruct((b, n, HEAD_W), BF16),
        compiler_params=_cparams(("parallel", "arbitrary")),
        name="na_attention",
    )(pattern_ids, nq, nk, nv, nkc, nvc, bias)


def _na_patterns(rows):
    kh = min(NA_WIN_H, rows)
    key_rows = NA_ROW_GROUP + kh - 1
    assert rows % NA_ROW_GROUP == 0 and rows >= key_rows
    patterns, ids = [], []
    for g in range(rows // NA_ROW_GROUP):
        u = min(max(NA_ROW_GROUP * g - kh // 2, 0), rows - key_rows)
        geo = []
        for r in range(NA_ROW_GROUP * g, NA_ROW_GROUP * (g + 1)):
            rs = min(max(r - kh // 2, 0), rows - kh)
            geo.append((r - u, rs - u))
        geo = tuple(geo)
        if geo not in patterns:
            patterns.append(geo)
        ids.append(patterns.index(geo))
    return patterns, ids


def _na_bias_table(rpb, rows):
    kh = min(NA_WIN_H, rows)
    key_rows = NA_ROW_GROUP + kh - 1
    patterns, ids = _na_patterns(rows)
    geo = jnp.asarray(patterns, dtype=jnp.int32)
    rq_off, win_off = geo[..., 0], geo[..., 1]
    a = jnp.arange(key_rows)
    row_ok = (a >= win_off[..., None]) & (a < win_off[..., None] + kh)
    row_idx = jnp.clip(a - rq_off[..., None] + NA_WIN_H - 1, 0, 2 * NA_WIN_H - 2)
    cols = jnp.arange(GRID_W)
    col_start = jnp.clip(cols - NA_WIN_W // 2, 0, GRID_W - NA_WIN_W)
    kc = jnp.arange(GRID_W)
    col_ok = (kc[None, :] >= col_start[:, None]) & (kc[None, :] < col_start[:, None] + NA_WIN_W)
    col_idx = jnp.clip(kc[None, :] - cols[:, None] + NA_WIN_W - 1, 0, 2 * NA_WIN_W - 2)
    t = rpb[:, row_idx]
    pick = (jnp.arange(2 * NA_WIN_W - 1)[:, None, None] == col_idx[None]).astype(F32)
    t = jnp.einsum('hpraj,jck->hprack', t, pick, precision=lax.Precision.HIGHEST)
    ok = row_ok[None, :, :, :, None, None] & col_ok[None, None, None, None]
    t = jnp.where(ok, t, MASK_VALUE)
    t = t.transpose(1, 0, 2, 4, 3, 5)
    n_pat = len(patterns)
    t = t.reshape(n_pat, N_PAIRS, 2 * NA_ROW_GROUP * GRID_W, key_rows * GRID_W)
    return t.astype(F32), jnp.asarray(ids, dtype=jnp.int32)


def _ctx_na_kernel(q_ref, k_ref, v_ref, o_ref):
    tq = q_ref.shape[1]
    lane = lax.broadcasted_iota(jnp.int32, (tq, LANES), 1)
    for pr in range(N_PAIRS):
        cols = slice(pr * LANES, (pr + 1) * LANES)
        q2 = q_ref[0, :, cols]
        kb = k_ref[0, :, cols]
        vb = v_ref[0, :, cols]
        acc = jnp.zeros((tq, LANES), F32)
        for hh in range(2):
            in_head = (lane >= hh * NA_HEAD_DIM) & (lane < (hh + 1) * NA_HEAD_DIM)
            qm = jnp.where(in_head, q2, jnp.zeros_like(q2))
            s = _dot_nt(qm, kb)
            p = jnp.exp(s - jnp.max(s, axis=-1, keepdims=True))
            l = jnp.sum(p, axis=-1, keepdims=True)
            acc = jnp.where(in_head, _dot(p.astype(BF16), vb) / l, acc)
        o_ref[0, :, cols] = acc.astype(BF16)


def _ctx_na_attention(q, k, v):
    b, n, _ = q.shape
    spec = pl.BlockSpec((1, n, HEAD_W), lambda bi: (bi, 0, 0))
    return pl.pallas_call(
        _ctx_na_kernel,
        grid=(b,),
        in_specs=[spec, spec, spec],
        out_specs=spec,
        out_shape=jax.ShapeDtypeStruct((b, n, HEAD_W), BF16),
        compiler_params=_cparams(("parallel",)),
        name="ctx_na_attention",
    )(q, k, v)


def _diff_kernel(q_ref, lam_ref, g_ref, *refs, lam_init, n_src):
    src_refs, o_ref, k_ref, v_ref = refs[:2 * n_src], refs[2 * n_src], refs[-2], refs[-1]

    @pl.when(pl.program_id(1) == 0)
    def _():
        off = 0
        for j in range(n_src):
            n = src_refs[2 * j].shape[1]
            k_ref[off:off + n] = src_refs[2 * j][0]
            v_ref[off:off + n] = src_refs[2 * j + 1][0]
            off += n

    tq = q_ref.shape[1]
    lp = lam_ref[...]
    lam = (jnp.exp(jnp.sum(lp[0:1] * lp[1:2], axis=-1, keepdims=True))
           - jnp.exp(jnp.sum(lp[2:3] * lp[3:4], axis=-1, keepdims=True)) + lam_init)
    lane = lax.broadcasted_iota(jnp.int32, (tq, LANES), 1)
    for pr in range(N_PAIRS):
        cols = slice(pr * LANES, (pr + 1) * LANES)
        q2 = q_ref[0, :, cols]
        zero = jnp.zeros_like(q2)
        outp = jnp.zeros((tq, LANES), F32)
        for hh in range(2):
            comp = []
            for c in range(2):
                lo = hh * DF_V_DIM + c * DF_QK_DIM
                qm = jnp.where((lane >= lo) & (lane < lo + DF_QK_DIM), q2, zero)
                s = _dot_nt(qm, k_ref[:, cols])
                p = jnp.exp2(s - jnp.max(s, axis=-1, keepdims=True)).astype(BF16)
                res = _dot(p, v_ref[:, 2 * pr * LANES:2 * (pr + 1) * LANES])
                comp.append(res[:, :LANES] / res[:, LANES:LANES + 1])
            in_head = (lane >= hh * DF_V_DIM) & (lane < (hh + 1) * DF_V_DIM)
            oh = jnp.where(in_head, comp[0] - lam * comp[1], 0.0)
            ms = jnp.sum(oh * oh, axis=-1, keepdims=True) * (1.0 / DF_V_DIM)
            outp = outp + oh * lax.rsqrt(ms + EPS) * g_ref[...] * (1.0 - lam_init)
        o_ref[0, :, cols] = outp.astype(BF16)


def _diff_attention(dq, kvs, lam_p, subln_g, lam_init):
    b, nq, _ = dq.shape
    tq = min(256, nq)
    nk = sum(k.shape[1] for k, _ in kvs)
    kv_specs, kv_args = [], []
    for k, v in kvs:
        kv_specs += [pl.BlockSpec((1, k.shape[1], HEAD_W), lambda bi, i: (bi, 0, 0)),
                     pl.BlockSpec((1, k.shape[1], 2 * HEAD_W), lambda bi, i: (bi, 0, 0))]
        kv_args += [k, v]
    return pl.pallas_call(
        functools.partial(_diff_kernel, lam_init=lam_init, n_src=len(kvs)),
        grid=(b, nq // tq),
        in_specs=[
            pl.BlockSpec((1, tq, HEAD_W), lambda bi, i: (bi, i, 0)),
            pl.BlockSpec(lam_p.shape, lambda bi, i: (0, 0)),
            pl.BlockSpec(subln_g.shape, lambda bi, i: (0, 0)),
        ] + kv_specs,
        out_specs=pl.BlockSpec((1, tq, HEAD_W), lambda bi, i: (bi, i, 0)),
        out_shape=jax.ShapeDtypeStruct((b, nq, HEAD_W), BF16),
        scratch_shapes=[pltpu.VMEM((nk, HEAD_W), BF16), pltpu.VMEM((nk, 2 * HEAD_W), BF16)],
        compiler_params=_cparams(("parallel", "arbitrary")),
        name="diff_attention",
    )(dq, lam_p, subln_g, *kv_args)


def _merge_kernel(x_ref, f_ref, na_ref, df_ref, gate_ref, g1_ref, wft_ref, wna_ref, wdf_ref, wout_ref,
                  ng_ref, sh_ref, sc_ref, wrh_ref, wrl_ref, xo_ref, h2_ref, lg_ref):
    d = x_ref.shape[2]
    y_ft = _dot(f_ref[0], wft_ref[...])
    y_na = _dot(na_ref[0], wna_ref[...])
    y_df = _dot(df_ref[0], wdf_ref[...])
    m = (gate_ref[0, :, 0:d].astype(F32) * y_ft + gate_ref[0, :, d:2 * d].astype(F32) * y_na
         + gate_ref[0, :, 2 * d:3 * d].astype(F32) * y_df)
    y = _dot(m.astype(BF16), wout_ref[...])
    xn = x_ref[0] + g1_ref[0] * y
    xo_ref[0] = xn
    ms = jnp.mean(xn * xn, axis=-1, keepdims=True)
    h2 = (xn * lax.rsqrt(ms + EPS) * ng_ref[...]) * (1.0 + sc_ref[0]) + sh_ref[0]
    h_hi, h_lo = _split_bf16(h2)
    h2_ref[0] = h_hi
    lg_ref[0] = _dot(h_hi, wrh_ref[...]) + _dot(h_lo, wrh_ref[...]) + _dot(h_hi, wrl_ref[...])


def _merge(x, f, o_na, o_df, gates, g1, w_ft, w_na_o, w_df_o, w_out, norm_g, sh2, sc2, wr_hi, wr_lo):
    b, n, d = x.shape
    tm = min(512, n)
    tok = lambda w: pl.BlockSpec((1, tm, w), lambda bi, i: (bi, i, 0))
    full = lambda a: pl.BlockSpec(a.shape, lambda bi, i: (0,) * a.ndim)
    mod = pl.BlockSpec((1, 1, d), lambda bi, i: (bi, 0, 0))
    return pl.pallas_call(
        _merge_kernel,
        grid=(b, n // tm),
        in_specs=[tok(d), tok(FT_WIDTH), tok(HEAD_W), tok(HEAD_W), tok(N_BRANCHES * d), mod,
                  full(w_ft), full(w_na_o), full(w_df_o), full(w_out), full(norm_g), mod, mod,
                  full(wr_hi), full(wr_lo)],
        out_specs=[tok(d), tok(d), tok(ROUTER_PAD)],
        out_shape=[jax.ShapeDtypeStruct((b, n, d), F32), jax.ShapeDtypeStruct((b, n, d), BF16),
                   jax.ShapeDtypeStruct((b, n, ROUTER_PAD), F32)],
        compiler_params=_cparams(("parallel", "parallel")),
        name="merge",
    )(x, f, o_na, o_df, gates, g1, w_ft, w_na_o, w_df_o, w_out, norm_g, sh2, sc2, wr_hi, wr_lo)


ROUTE_TILE = LANES
SLOT_ALIGN = 16
ONE_BITS = 0x3F800000


FAST_WINDOW = 48


def _slot_window(cap):
    return min(ROUTE_TILE + SLOT_ALIGN, cap)


def _fast_window(cap):
    return min(FAST_WINDOW, cap)


def _slot_windows(lo, cap):
    b, nt, _ = lo.shape
    filled = jnp.concatenate([lo[:, 1:], jnp.full((b, 1, N_EXPERTS), cap, lo.dtype)], axis=1)
    aligned = (lo // SLOT_ALIGN) * SLOT_ALIGN
    lo_slow = jnp.minimum(aligned, cap - _slot_window(cap))
    lo_fast = jnp.minimum(aligned, cap - _fast_window(cap))
    fast_ok = jnp.all(filled - lo_fast <= _fast_window(cap), axis=-1)
    return {"lo_slow": lo_slow.reshape(-1), "lo_fast": lo_fast.reshape(-1),
            "fast_ok": fast_ok.astype(jnp.int32).reshape(-1)}


def _route_kernel(lg_ref, pos_ref, aff_ref, lo_ref, bits_ref, sel_ref, packed_ref, *, cap):
    n = lg_ref.shape[1]
    nb = n // ROUTE_TILE
    lane = lax.broadcasted_iota(jnp.int32, (n, LANES), 1)
    z = jnp.where(lane < N_EXPERTS, lg_ref[0], MASK_VALUE)
    p = jnp.exp(z - jnp.max(z, axis=-1, keepdims=True))
    aff = p / jnp.sum(p, axis=-1, keepdims=True)
    aff_ref[0] = aff
    bits_ref[...] = pltpu.bitcast(aff, jnp.int32)

    groups = LANES // N_EXPERTS
    rows_p = n // groups
    packed = bits_ref[0:rows_p]
    for j in range(1, groups):
        packed = packed + pltpu.roll(bits_ref[j * rows_p:(j + 1) * rows_p], N_EXPERTS * j, 1)
    packed_ref[...] = packed

    def count(mask):
        part = jnp.broadcast_to(jnp.sum(jnp.where(mask, 1.0, 0.0), axis=0, keepdims=True), (8, LANES))
        tot = part
        for j in range(1, groups):
            tot = tot + pltpu.roll(part, N_EXPERTS * j, 1)
        return tot[0:1]

    def bisect(_, carry):
        lo, hi = carry
        mid = (lo + hi) >> 1
        ge = count(packed_ref[...] >= mid) >= cap
        return jnp.where(ge, mid, lo), jnp.where(ge, hi, mid)

    lo0 = jnp.zeros((1, LANES), jnp.int32)
    hi0 = jnp.full((1, LANES), ONE_BITS + 1, jnp.int32)
    thr, _ = lax.fori_loop(0, 31, bisect, (lo0, hi0))
    need = cap - count(packed_ref[...] > thr)

    row = lax.broadcasted_iota(jnp.int32, (ROUTE_TILE, ROUTE_TILE), 0)
    col = lax.broadcasted_iota(jnp.int32, (ROUTE_TILE, ROUTE_TILE), 1)
    tri = jnp.where(row >= col, 1.0, 0.0).astype(BF16)

    carry = jnp.zeros((1, LANES), F32)
    for blk in range(nb):
        rows = slice(blk * ROUTE_TILE, (blk + 1) * ROUTE_TILE)
        bb = bits_ref[rows]
        eq = jnp.where(bb == thr, 1.0, 0.0)
        incl = _dot(tri, eq.astype(BF16))
        before = incl - eq + carry
        take = jnp.where(before < need, eq, 0.0)
        sel_ref[rows] = jnp.where(bb > thr, 1.0, take)
        carry = carry + incl[ROUTE_TILE - 1:ROUTE_TILE]

    carry = jnp.zeros((1, LANES), F32)
    for blk in range(nb):
        rows = slice(blk * ROUTE_TILE, (blk + 1) * ROUTE_TILE)
        sel = sel_ref[rows]
        incl = _dot(tri, sel.astype(BF16))
        pos_ref[0, rows] = jnp.where(sel > 0.0, carry + incl - sel, -1.0)
        lo_ref[0, blk:blk + 1] = carry.astype(jnp.int32)
        carry = carry + incl[ROUTE_TILE - 1:ROUTE_TILE]


def _route(logits, cap):
    b, n, _ = logits.shape
    nt = n // ROUTE_TILE
    tok = pl.BlockSpec((1, n, LANES), lambda bi: (bi, 0, 0))
    return pl.pallas_call(
        functools.partial(_route_kernel, cap=cap),
        grid=(b,),
        in_specs=[tok],
        out_specs=[tok, tok, pl.BlockSpec((1, nt, LANES), lambda bi: (bi, 0, 0))],
        out_shape=[jax.ShapeDtypeStruct((b, n, LANES), F32), jax.ShapeDtypeStruct((b, n, LANES), F32),
                   jax.ShapeDtypeStruct((b, nt, LANES), jnp.int32)],
        scratch_shapes=[pltpu.VMEM((n, LANES), jnp.int32), pltpu.VMEM((n, LANES), F32),
                        pltpu.VMEM((n // (LANES // N_EXPERTS), LANES), jnp.int32)],
        compiler_params=_cparams(("parallel",)),
        name="route",
    )(logits)


DISPATCH_COLS = 256


def _dispatch_kernel(lo_fast_ref, lo_slow_ref, fast_ref, h_ref, pos_ref, o_ref, *, win_fast, win_slow):
    b = pl.program_id(0)
    t = pl.program_id(1)
    nt = pl.num_programs(1)

    @pl.when(t == 0)
    def _():
        o_ref[...] = jnp.zeros_like(o_ref)

    def run(win, lo_ref):
        slot = lax.broadcasted_iota(jnp.int32, (win, ROUTE_TILE), 0).astype(F32)
        pos_t = pos_ref[0, 0]
        los = [pl.multiple_of(lo_ref[(b * nt + t) * N_EXPERTS + e], SLOT_ALIGN) for e in range(N_EXPERTS)]
        onehot = jnp.concatenate(
            [jnp.where(pos_t[e:e + 1, :] - los[e].astype(F32) == slot, 1.0, 0.0).astype(BF16)
             for e in range(N_EXPERTS)], axis=0)
        for c0 in range(0, h_ref.shape[2], DISPATCH_COLS):
            cols = slice(c0, c0 + DISPATCH_COLS)
            res = _dot(onehot, h_ref[0, :, cols]).astype(BF16)
            for e in range(N_EXPERTS):
                rows = pl.ds(los[e], win)
                o_ref[0, e, rows, cols] = o_ref[0, e, rows, cols] + res[e * win:(e + 1) * win]

    if win_fast == win_slow:
        run(win_slow, lo_slow_ref)
    else:
        fast = fast_ref[b * nt + t] == 1
        pl.when(fast)(functools.partial(run, win_fast, lo_fast_ref))
        pl.when(jnp.logical_not(fast))(functools.partial(run, win_slow, lo_slow_ref))


def _dispatch(windows, h2, pos_t, cap):
    b, n, d = h2.shape
    nt = n // ROUTE_TILE
    grid_spec = pltpu.PrefetchScalarGridSpec(
        num_scalar_prefetch=3,
        grid=(b, nt),
        in_specs=[
            pl.BlockSpec((1, ROUTE_TILE, d), lambda bi, t, *_: (bi, t, 0)),
            pl.BlockSpec((1, 1, N_EXPERTS, ROUTE_TILE), lambda bi, t, *_: (bi, t, 0, 0)),
        ],
        out_specs=pl.BlockSpec((1, N_EXPERTS, cap, d), lambda bi, t, *_: (bi, 0, 0, 0)),
    )
    return pl.pallas_call(
        functools.partial(_dispatch_kernel, win_fast=_fast_window(cap), win_slow=_slot_window(cap)),
        grid_spec=grid_spec,
        out_shape=jax.ShapeDtypeStruct((b, N_EXPERTS, cap, d), BF16),
        compiler_params=_cparams(("arbitrary", "arbitrary")),
        name="dispatch",
    )(windows["lo_fast"], windows["lo_slow"], windows["fast_ok"], h2, pos_t)


def _expert_kernel(*refs, f_chunk, n_sets):
    x_refs, (wg_ref, wu_ref, wd_ref), o_refs = refs[:n_sets], refs[n_sets:n_sets + 3], refs[n_sets + 3:]
    xs = [r[0, 0] for r in x_refs]
    x = xs[0] if n_sets == 1 else jnp.concatenate(xs, axis=0)
    ff = wg_ref.shape[3]
    acc = jnp.zeros(x.shape, F32)
    for f0 in range(0, ff, f_chunk):
        a = _dot(x, wg_ref[0, 0, :, f0:f0 + f_chunk])
        u = _dot(x, wu_ref[0, 0, :, f0:f0 + f_chunk])
        hm = (a * jax.nn.sigmoid(a) * u).astype(BF16)
        acc = acc + _dot(hm, wd_ref[0, 0, f0:f0 + f_chunk, :])
    off = 0
    for xr, o_ref in zip(xs, o_refs):
        o_ref[0, 0] = acc[off:off + xr.shape[0]].astype(BF16)
        off += xr.shape[0]


def _experts(xes, w_gate, w_up, w_down, layer):
    b, e, _, d = xes[0].shape
    ff = w_gate.shape[3]
    toks = [pl.BlockSpec((1, 1, xe.shape[2], d), lambda ei, bi: (bi, ei, 0, 0)) for xe in xes]
    return pl.pallas_call(
        functools.partial(_expert_kernel, f_chunk=min(512, ff), n_sets=len(xes)),
        grid=(e, b),
        in_specs=toks + [
            pl.BlockSpec((1, 1, d, ff), lambda ei, bi: (layer, ei, 0, 0)),
            pl.BlockSpec((1, 1, d, ff), lambda ei, bi: (layer, ei, 0, 0)),
            pl.BlockSpec((1, 1, ff, d), lambda ei, bi: (layer, ei, 0, 0)),
        ],
        out_specs=toks,
        out_shape=[jax.ShapeDtypeStruct(xe.shape, BF16) for xe in xes],
        compiler_params=_cparams(("arbitrary", "arbitrary")),
        name="experts",
    )(*xes, w_gate, w_up, w_down)


def _combine_kernel(lo_fast_ref, lo_slow_ref, fast_ref, x_ref, y_ref, pos_ref, aff_ref, g_ref, spread_ref, slot_ref,
                    o_ref, *, win_fast, win_slow):
    b = pl.program_id(0)
    t = pl.program_id(1)
    nt = pl.num_programs(1)
    tile = x_ref.shape[1]
    base = (b * nt + t) * N_EXPERTS

    def wide():
        slot = lax.broadcasted_iota(jnp.int32, (tile, win_slow), 1).astype(F32)
        pos = pos_ref[0]
        aff = aff_ref[0]
        acc = jnp.zeros(x_ref.shape[1:], F32)
        for e in range(N_EXPERTS):
            lo = pl.multiple_of(lo_slow_ref[base + e], SLOT_ALIGN)
            pick = jnp.where(pos[:, e:e + 1] - lo.astype(F32) == slot, aff[:, e:e + 1], 0.0).astype(BF16)
            acc = acc + _dot(pick, y_ref[0, e, pl.ds(lo, win_slow), :])
        o_ref[0] = x_ref[0] + g_ref[0] * acc

    def narrow():
        lane = lax.broadcasted_iota(jnp.int32, (1, LANES), 1)
        lo_vec = jnp.zeros((1, LANES), F32)
        los = []
        for e in range(N_EXPERTS):
            lo = pl.multiple_of(lo_fast_ref[base + e], SLOT_ALIGN)
            los.append(lo)
            lo_vec = jnp.where(lane == e, lo.astype(F32), lo_vec)
        pos = pos_ref[0]
        rel = jnp.where(pos >= 0.0, pos - lo_vec, -1.0).astype(BF16)
        rel_wide = _dot(rel, spread_ref[...])
        gate_wide = _dot(aff_ref[0].astype(BF16), spread_ref[...])
        pick = jnp.where(rel_wide == slot_ref[...], gate_wide, 0.0).astype(BF16)
        y_cat = jnp.concatenate([y_ref[0, e, pl.ds(los[e], win_fast), :] for e in range(N_EXPERTS)], axis=0)
        o_ref[0] = x_ref[0] + g_ref[0] * _dot(pick, y_cat)

    if win_fast == win_slow:
        wide()
    else:
        fast = fast_ref[b * nt + t] == 1
        pl.when(fast)(narrow)
        pl.when(jnp.logical_not(fast))(wide)


def _combine(windows, x, ye, pos, aff, g2, cap):
    b, n, d = x.shape
    e = ye.shape[1]
    nt = n // ROUTE_TILE
    tok = lambda w: pl.BlockSpec((1, ROUTE_TILE, w), lambda bi, t, *_: (bi, t, 0))
    win_fast = _fast_window(cap)
    j = jnp.arange(N_EXPERTS * win_fast)
    spread = (jnp.arange(LANES)[:, None] == (j // win_fast)[None, :]).astype(BF16)
    slot = (j % win_fast).astype(F32)[None, :]
    grid_spec = pltpu.PrefetchScalarGridSpec(
        num_scalar_prefetch=3,
        grid=(b, nt),
        in_specs=[
            tok(d),
            pl.BlockSpec((1, e, cap, d), lambda bi, t, *_: (bi, 0, 0, 0)),
            tok(LANES),
            tok(LANES),
            pl.BlockSpec((1, 1, d), lambda bi, t, *_: (bi, 0, 0)),
            pl.BlockSpec(spread.shape, lambda bi, t, *_: (0, 0)),
            pl.BlockSpec(slot.shape, lambda bi, t, *_: (0, 0)),
        ],
        out_specs=tok(d),
    )
    return pl.pallas_call(
        functools.partial(_combine_kernel, win_fast=win_fast, win_slow=_slot_window(cap)),
        grid_spec=grid_spec,
        out_shape=jax.ShapeDtypeStruct((b, n, d), F32),
        compiler_params=_cparams(("arbitrary", "arbitrary")),
        name="combine",
    )(windows["lo_fast"], windows["lo_slow"], windows["fast_ok"], x, ye, pos, aff, g2, spread, slot)


def _route_and_dispatch(h2, logits):
    b, n, _ = h2.shape
    cap = EC_CAPACITY_FACTOR * n // N_EXPERTS
    nt = n // ROUTE_TILE
    assert n % ROUTE_TILE == 0 and cap % SLOT_ALIGN == 0
    pos, aff, lo = _route(logits, cap)
    windows = _slot_windows(lo[:, :, :N_EXPERTS], cap)
    pos_t = pos[:, :, :N_EXPERTS].reshape(b, nt, ROUTE_TILE, N_EXPERTS).transpose(0, 1, 3, 2)
    xe = _dispatch(windows, h2, pos_t, cap)
    return xe, {"windows": windows, "pos": pos, "aff": aff, "cap": cap}


def _combine_residual(x, ye, routing, g2):
    return _combine(routing["windows"], x, ye, routing["pos"], routing["aff"], g2, routing["cap"])


def _dft_mats(n):
    k = jnp.arange(n, dtype=jnp.int32)
    ang = ((k[:, None] * k[None, :]) % n).astype(F32) * (2.0 * math.pi / n)
    return jnp.cos(ang), jnp.sin(ang)


def _dft_mats_bf16(n, block=64):
    assert n % block == 0
    k = jnp.arange(n, dtype=jnp.int32)
    a = jnp.arange(n // block, dtype=jnp.int32) * block
    b = jnp.arange(block, dtype=jnp.int32)
    ang_a = ((a[:, None] * k[None, :]) % n).astype(F32) * (2.0 * math.pi / n)
    ang_b = ((b[:, None] * k[None, :]) % n).astype(F32) * (2.0 * math.pi / n)
    ca, sa = jnp.cos(ang_a)[:, None, :], jnp.sin(ang_a)[:, None, :]
    cb, sb = jnp.cos(ang_b)[None], jnp.sin(ang_b)[None]
    cos_m = (ca * cb - sa * sb).reshape(n, n)
    nsin_m = (-(sa * cb + ca * sb)).reshape(n, n)
    return cos_m.astype(BF16), nsin_m.astype(BF16)


def _chan_dft():
    c, s = _dft_mats(FT_GROUP_DIM)
    eye = jnp.eye(FT_GROUPS, dtype=F32)
    return jnp.concatenate([jnp.kron(eye, c), jnp.kron(eye, s)], axis=1).astype(BF16)


def _group_mean_mat(group):
    gid = jnp.arange(HEAD_W) // group
    return jnp.where(gid[:, None] == gid[None, :], 1.0 / group, 0.0).astype(BF16)


def _rope_tables(n):
    t = jnp.arange(n)
    row = (t // GRID_W).astype(F32)
    col = (t % GRID_W).astype(F32)
    ax = DF_QK_DIM // 2
    inv = ROPE_BASE ** (-jnp.arange(0, ax, 2, dtype=F32) / ax)
    lane = jnp.arange(LANES)
    freq = inv[lane % (ax // 2)]
    pos = jnp.where(((lane % DF_QK_DIM) < ax)[None, :], row[:, None], col[:, None])
    ang = pos * freq[None, :]
    sign = jnp.where((lane % ax) < ax // 2, -1.0, 1.0)
    return jnp.cos(ang), jnp.sin(ang) * sign[None, :]


def kernel(x, c, ctx, c_ctx, norm1_g, norm2_g, w_ada, b_ada, w_in, na_qn_g, na_kn_g, na_rpb, df_qn_g, df_kn_g,
           df_lambda, df_subln_g, w_ft, w_na_o, w_df_o, w_out, w_router, w_gate, w_up, w_down):
    b, n, d = x.shape
    lc = ctx.shape[1]
    depth = w_ada.shape[0]
    assert b + 1 <= MOD_ROWS and n % GRID_W == 0

    cc = jnp.zeros((MOD_ROWS, d), F32).at[:b].set(c).at[b].set(c_ctx)
    mods = _ada(cc, w_ada, b_ada)

    consts = {"chan_dft": _chan_dft(), "g64": _group_mean_mat(NA_HEAD_DIM), "g32": _group_mean_mat(DF_QK_DIM)}
    rope_lat = _rope_tables(n)
    rope_ctx = (jnp.zeros((lc, LANES), F32), jnp.zeros((lc, LANES), F32))
    dft_lat = _dft_mats_bf16(n)
    dft_ctx = _dft_mats_bf16(lc)
    rows = n // GRID_W

    w_gate_b, w_up_b, w_down_b = w_gate.astype(BF16), w_up.astype(BF16), w_down.astype(BF16)

    xc = ctx
    for i in range(depth):
        last = i == depth - 1
        lam_init = 0.8 - 0.6 * math.exp(-0.3 * i)
        m_lat = mods[i, :b].reshape(b, 6, 1, d)
        m_ctx = jnp.broadcast_to(mods[i, b].reshape(1, 6, 1, d), (b, 6, 1, d))
        sh1, sc1, g1, sh2, sc2, g2 = [m_lat[:, j] for j in range(6)]
        csh1, csc1, cg1, csh2, csc2, cg2 = [m_ctx[:, j] for j in range(6)]

        w_in_b = w_in[i].astype(BF16)
        gains = {
            "naq": jnp.tile(na_qn_g[i], NA_HEADS).reshape(1, HEAD_W),
            "nak": jnp.tile(na_kn_g[i], NA_HEADS).reshape(1, HEAD_W),
            "dfq": jnp.tile(df_qn_g[i], 2 * DF_HEADS).reshape(1, HEAD_W),
            "dfk": jnp.tile(df_kn_g[i], 2 * DF_HEADS).reshape(1, HEAD_W),
        }
        n1g = norm1_g[i].reshape(1, d)
        n2g = norm2_g[i].reshape(1, d)
        subln = jnp.tile(df_subln_g[i], 2).reshape(1, LANES)
        w_ft_b, w_na_b, w_df_b, w_out_b = (w.astype(BF16) for w in (w_ft[i], w_na_o[i], w_df_o[i], w_out[i]))
        wr = jnp.zeros((d, ROUTER_PAD), F32).at[:, :N_EXPERTS].set(w_router[i])
        wr_hi, wr_lo = _split_bf16(wr)

        if last:
            nkc, nvc, dkc, dvc = _inproj(xc, csh1, csc1, n1g, w_in_b, consts, gains, rope_ctx,
                                         rope=False, kv_only=True)
        else:
            fabc, nqc, dqc, nkc, nvc, dkc, dvc, gatec = _inproj(xc, csh1, csc1, n1g, w_in_b, consts, gains,
                                                                 rope_ctx, rope=False, kv_only=False)

        fab, nq, dq, nk, nv, dk, dv, gate = _inproj(x, sh1, sc1, n1g, w_in_b, consts, gains, rope_lat,
                                                     rope=True, kv_only=False)
        f = _fourier(fab, *dft_lat)
        o_na = _na_attention(nq, nk, nv, nkc, nvc, *_na_bias_table(na_rpb[i], rows))
        o_df = _diff_attention(dq, [(dk, dv), (dkc, dvc)], df_lambda[i], subln, lam_init)
        x, h2, logits = _merge(x, f, o_na, o_df, gate, g1, w_ft_b, w_na_b, w_df_b, w_out_b, n2g, sh2, sc2,
                               wr_hi, wr_lo)
        xe, routing = _route_and_dispatch(h2, logits)

        if last:
            (ye,) = _experts([xe], w_gate_b, w_up_b, w_down_b, i)
        else:
            fc = _fourier(fabc, *dft_ctx)
            o_nac = _ctx_na_attention(nqc, nkc, nvc)
            o_dfc = _diff_attention(dqc, [(dkc, dvc)], df_lambda[i], subln, lam_init)
            xc, hc2, logits_c = _merge(xc, fc, o_nac, o_dfc, gatec, cg1, w_ft_b, w_na_b, w_df_b, w_out_b, n2g,
                                       csh2, csc2, wr_hi, wr_lo)
            xec, routing_c = _route_and_dispatch(hc2, logits_c)
            ye, yec = _experts([xe, xec], w_gate_b, w_up_b, w_down_b, i)
            xc = _combine_residual(xc, yec, routing_c, cg2)
        x = _combine_residual(x, ye, routing, g2)
    return x
```

```python
import functools
import math

import jax
import jax.numpy as jnp
from jax import lax
from jax.experimental import pallas as pl
from jax.experimental.pallas import tpu as pltpu

F32 = jnp.float32
BF16 = jnp.bfloat16

GRID_W = 64
FT_GROUPS = 4
FT_GROUP_DIM = 64
FT_WIDTH = FT_GROUPS * FT_GROUP_DIM
NA_HEADS = 6
NA_HEAD_DIM = 64
NA_WIDTH = NA_HEADS * NA_HEAD_DIM
NA_WIN_H = 8
NA_WIN_W = 16
DF_HEADS = 6
DF_QK_DIM = 32
DF_V_DIM = 2 * DF_QK_DIM
DF_QK_WIDTH = DF_HEADS * 2 * DF_QK_DIM
DF_WIDTH = DF_HEADS * DF_V_DIM
N_BRANCHES = 3
N_EXPERTS = 16
EC_CAPACITY_FACTOR = 2
ROPE_BASE = 10000.0
EPS = 1e-6
MASK_VALUE = -1e30

LANES = 128
VMEM_LIMIT_BYTES = 56 * 1024 * 1024

HEAD_W = 384
N_PAIRS = HEAD_W // LANES
MOD_ROWS = 16
ROUTER_PAD = LANES


def _cparams(sem):
    return pltpu.CompilerParams(dimension_semantics=sem, vmem_limit_bytes=VMEM_LIMIT_BYTES)


def _dot(a, b):
    return jnp.dot(a, b, preferred_element_type=F32)


def _dot_nt(a, b):
    return lax.dot_general(a, b, (((1,), (1,)), ((), ())), preferred_element_type=F32)


def _split_bf16(v):
    hi = v.astype(BF16)
    lo = (v - hi.astype(F32)).astype(BF16)
    return hi, lo


def _ada_kernel(c_ref, w_ref, b_ref, o_ref):
    c = c_ref[...]
    a = c * jax.nn.sigmoid(c)
    a_hi, a_lo = _split_bf16(a)
    w_hi, w_lo = _split_bf16(w_ref[0])
    acc = _dot(a_hi, w_hi) + _dot(a_lo, w_hi) + _dot(a_hi, w_lo)
    o_ref[0] = acc + b_ref[0]


def _ada(cc, w_ada, b_ada):
    depth, d, d6 = w_ada.shape
    tn = 512
    return pl.pallas_call(
        _ada_kernel,
        grid=(depth, d6 // tn),
        in_specs=[
            pl.BlockSpec((MOD_ROWS, d), lambda l, j: (0, 0)),
            pl.BlockSpec((1, d, tn), lambda l, j: (l, 0, j)),
            pl.BlockSpec((1, 1, tn), lambda l, j: (l, 0, j)),
        ],
        out_specs=pl.BlockSpec((1, MOD_ROWS, tn), lambda l, j: (l, 0, j)),
        out_shape=jax.ShapeDtypeStruct((depth, MOD_ROWS, d6), F32),
        compiler_params=_cparams(("arbitrary", "arbitrary")),
        name="ada",
    )(cc, w_ada, b_ada.reshape(depth, 1, d6))


OFF_Q = FT_WIDTH
OFF_DQ = OFF_Q + NA_WIDTH
OFF_KV = OFF_DQ + DF_QK_WIDTH
OFF_NV = OFF_KV + NA_WIDTH
OFF_DK = OFF_NV + NA_WIDTH
OFF_DV = OFF_DK + DF_QK_WIDTH
OFF_GATE = OFF_DV + DF_WIDTH


def _group_rms(v, gmat_ref, gain_ref):
    ms = _dot((v * v).astype(BF16), gmat_ref[...])
    return v * lax.rsqrt(ms + EPS) * gain_ref[...]


def _rope_chunk(vj, cos, sin_signed, first_half):
    fwd = pltpu.roll(vj, LANES - 8, 1)
    bwd = pltpu.roll(vj, 8, 1)
    partner = jnp.where(first_half, fwd, bwd)
    return vj * cos + partner * sin_signed


def _inproj_kernel(x_ref, sh_ref, sc_ref, ng_ref, w_ref, cd_ref, g64_ref, g32_ref,
                   naq_g_ref, nak_g_ref, dfq_g_ref, dfk_g_ref, cos_ref, sin_ref,
                   *out_refs, rope, kv_only):
    x = x_ref[0]
    ms = jnp.mean(x * x, axis=-1, keepdims=True)
    y = x * lax.rsqrt(ms + EPS) * ng_ref[...]
    h = (y * (1.0 + sc_ref[0]) + sh_ref[0]).astype(BF16)

    def proj(c0, c1):
        return _dot(h, w_ref[:, c0:c1])

    if rope:
        lane = lax.broadcasted_iota(jnp.int32, (x.shape[0], LANES), 1)
        first_half = (lane & 8) == 0
        cos = cos_ref[...]
        sin_signed = sin_ref[...]

    def df_qk(v, gain_ref, scale, o_ref):
        v = _group_rms(v, g32_ref, gain_ref)
        for j in range(N_PAIRS):
            vj = v[:, j * LANES:(j + 1) * LANES]
            if rope:
                vj = _rope_chunk(vj, cos, sin_signed, first_half)
            o_ref[0, :, j * LANES:(j + 1) * LANES] = (vj * scale).astype(BF16)

    if kv_only:
        nk_ref, nv_ref, dk_ref, dv_ref = out_refs
    else:
        fab_ref, nq_ref, dq_ref, nk_ref, nv_ref, dk_ref, dv_ref, gate_ref = out_refs
        u = proj(0, FT_WIDTH)
        fab_ref[0] = _dot(u.astype(BF16), cd_ref[...]).astype(BF16)
        nq = _group_rms(proj(OFF_Q, OFF_DQ), g64_ref, naq_g_ref)
        nq_ref[0] = (nq * (NA_HEAD_DIM ** -0.5)).astype(BF16)
        df_qk(proj(OFF_DQ, OFF_KV), dfq_g_ref, DF_QK_DIM ** -0.5 * math.log2(math.e), dq_ref)
        d = x.shape[1]
        for j in range(N_BRANCHES):
            z = proj(OFF_GATE + j * d, OFF_GATE + (j + 1) * d)
            gate_ref[0, :, j * d:(j + 1) * d] = jax.nn.sigmoid(z).astype(BF16)

    nk_ref[0] = _group_rms(proj(OFF_KV, OFF_NV), g64_ref, nak_g_ref).astype(BF16)
    nv_ref[0] = proj(OFF_NV, OFF_DK).astype(BF16)
    df_qk(proj(OFF_DK, OFF_DV), dfk_g_ref, 1.0, dk_ref)
    dv = proj(OFF_DV, OFF_GATE).astype(BF16)
    for j in range(N_PAIRS):
        dv_ref[0, :, 2 * j * LANES:(2 * j + 1) * LANES] = dv[:, j * LANES:(j + 1) * LANES]
        dv_ref[0, :, (2 * j + 1) * LANES:(2 * j + 2) * LANES] = jnp.ones((dv.shape[0], LANES), BF16)


def _inproj(x, sh, sc, norm_g, w_in, consts, gains, rope_tabs, *, rope, kv_only):
    b, n, d = x.shape
    tm = min(512, n)
    tok = lambda w: pl.BlockSpec((1, tm, w), lambda bi, i: (bi, i, 0))
    full = lambda a: pl.BlockSpec(a.shape, lambda bi, i: (0,) * a.ndim)
    mod = pl.BlockSpec((1, 1, d), lambda bi, i: (bi, 0, 0))
    cos_t, sin_t = rope_tabs
    tab = pl.BlockSpec((tm, LANES), lambda bi, i: (i, 0))
    slab = jax.ShapeDtypeStruct((b, n, HEAD_W), BF16)
    slab_aug = jax.ShapeDtypeStruct((b, n, 2 * HEAD_W), BF16)
    if kv_only:
        out_shape = [slab] * 3 + [slab_aug]
        out_specs = [tok(HEAD_W)] * 3 + [tok(2 * HEAD_W)]
    else:
        out_shape = [jax.ShapeDtypeStruct((b, n, 2 * FT_WIDTH), BF16)] + [slab] * 5 + [
            slab_aug, jax.ShapeDtypeStruct((b, n, N_BRANCHES * d), BF16)]
        out_specs = [tok(2 * FT_WIDTH)] + [tok(HEAD_W)] * 5 + [tok(2 * HEAD_W), tok(N_BRANCHES * d)]
    args = [x, sh, sc, norm_g, w_in, consts["chan_dft"], consts["g64"], consts["g32"],
            gains["naq"], gains["nak"], gains["dfq"], gains["dfk"], cos_t, sin_t]
    in_specs = [tok(d), mod, mod, full(norm_g), full(w_in), full(consts["chan_dft"]),
                full(consts["g64"]), full(consts["g32"]), full(gains["naq"]), full(gains["nak"]),
                full(gains["dfq"]), full(gains["dfk"]), tab, tab]
    return pl.pallas_call(
        functools.partial(_inproj_kernel, rope=rope, kv_only=kv_only),
        grid=(b, n // tm),
        in_specs=in_specs,
        out_specs=out_specs,
        out_shape=out_shape,
        compiler_params=_cparams(("parallel", "parallel")),
        name="inproj_kv" if kv_only else "inproj",
    )(*args)


def _fourier_kernel(c_ref, s_ref, head_ref, tail_ref, mid_ref, o_ref, acc_ref, *, scale):
    k = pl.program_id(1)

    @pl.when(k == 0)
    def _():
        acc_ref[...] = jnp.zeros_like(acc_ref)

    cm = c_ref[...]
    sm = s_ref[...]
    for bi in range(head_ref.shape[0]):
        head = head_ref[bi].astype(F32)
        tail = tail_ref[bi].astype(F32)
        af = (head[:, :FT_WIDTH] + tail[:, :FT_WIDTH]).astype(BF16)
        bf = (head[:, FT_WIDTH:] - tail[:, FT_WIDTH:]).astype(BF16)
        acc_ref[bi] += _dot(cm, af) + _dot(sm, bf)

    @pl.when(k == pl.num_programs(1) - 1)
    def _():
        row = lax.broadcasted_iota(jnp.int32, (acc_ref.shape[1], 1), 0)
        sign = jnp.where((row & 1) == 0, 1.0, -1.0)
        for bi in range(head_ref.shape[0]):
            o_ref[bi] = ((acc_ref[bi] + sign * mid_ref[bi].astype(F32)) * scale).astype(o_ref.dtype)


def _fourier(fab, cos_m, nsin_m):
    b, n, _ = fab.shape
    half = n // 2
    tn = min(1024, n)
    tk = min(512, half)
    assert tn % 2 == 0
    scale = 1.0 / math.sqrt(n * FT_GROUP_DIM)
    head = fab[:, :half]
    tail = jnp.concatenate([jnp.zeros_like(fab[:, :1]), jnp.flip(fab[:, half + 1:], axis=1)], axis=1)
    mid = fab[:, half:half + 1, :FT_WIDTH]
    return pl.pallas_call(
        functools.partial(_fourier_kernel, scale=scale),
        grid=(n // tn, half // tk),
        in_specs=[
            pl.BlockSpec((tn, tk), lambda i, k: (i, k)),
            pl.BlockSpec((tn, tk), lambda i, k: (i, k)),
            pl.BlockSpec((b, tk, 2 * FT_WIDTH), lambda i, k: (0, k, 0)),
            pl.BlockSpec((b, tk, 2 * FT_WIDTH), lambda i, k: (0, k, 0)),
            pl.BlockSpec((b, 1, FT_WIDTH), lambda i, k: (0, 0, 0)),
        ],
        out_specs=pl.BlockSpec((b, tn, FT_WIDTH), lambda i, k: (0, i, 0)),
        out_shape=jax.ShapeDtypeStruct((b, n, FT_WIDTH), BF16),
        scratch_shapes=[pltpu.VMEM((b, tn, FT_WIDTH), F32)],
        compiler_params=_cparams(("parallel", "arbitrary")),
        name="fourier",
    )(cos_m, nsin_m, head, tail, mid)


NA_ROW_GROUP = 4


def _na_kernel(pid_ref, q_ref, k_ref, v_ref, kc_ref, vc_ref, bias_ref, o_ref, *, rows, kh):
    del pid_ref
    g = pl.program_id(1)
    rq = NA_ROW_GROUP * GRID_W
    key_rows = NA_ROW_GROUP + kh - 1
    u = jnp.clip(NA_ROW_GROUP * g - kh // 2, 0, rows - key_rows)
    start = pl.multiple_of(u * GRID_W, GRID_W)
    lane = lax.broadcasted_iota(jnp.int32, (rq, LANES), 1)
    first = lane < NA_HEAD_DIM
    for pr in range(N_PAIRS):
        cols = slice(pr * LANES, (pr + 1) * LANES)
        q2 = q_ref[0, :, cols]
        zero = jnp.zeros_like(q2)
        qcat = jnp.concatenate([jnp.where(first, q2, zero), jnp.where(first, zero, q2)], axis=0)
        kb = k_ref[0, pl.ds(start, key_rows * GRID_W), cols]
        vb = v_ref[0, pl.ds(start, key_rows * GRID_W), cols]
        s1 = _dot_nt(qcat, kb) + bias_ref[0, pr]
        s2 = _dot_nt(qcat, kc_ref[0, :, cols])
        m = jnp.maximum(jnp.max(s1, axis=-1, keepdims=True), jnp.max(s2, axis=-1, keepdims=True))
        p1 = jnp.exp(s1 - m)
        p2 = jnp.exp(s2 - m)
        l = jnp.sum(p1, axis=-1, keepdims=True) + jnp.sum(p2, axis=-1, keepdims=True)
        o = (_dot(p1.astype(BF16), vb) + _dot(p2.astype(BF16), vc_ref[0, :, cols])) / l
        o_ref[0, :, cols] = jnp.where(first, o[:rq], o[rq:]).astype(BF16)


def _na_attention(nq, nk, nv, nkc, nvc, bias, pattern_ids):
    b, n, _ = nq.shape
    lc = nkc.shape[1]
    rows = n // GRID_W
    kh = min(NA_WIN_H, rows)
    rq = NA_ROW_GROUP * GRID_W
    grid_spec = pltpu.PrefetchScalarGridSpec(
        num_scalar_prefetch=1,
        grid=(b, rows // NA_ROW_GROUP),
        in_specs=[
            pl.BlockSpec((1, rq, HEAD_W), lambda bi, g, pid: (bi, g, 0)),
            pl.BlockSpec((1, n, HEAD_W), lambda bi, g, pid: (bi, 0, 0)),
            pl.BlockSpec((1, n, HEAD_W), lambda bi, g, pid: (bi, 0, 0)),
            pl.BlockSpec((1, lc, HEAD_W), lambda bi, g, pid: (bi, 0, 0)),
            pl.BlockSpec((1, lc, HEAD_W), lambda bi, g, pid: (bi, 0, 0)),
            pl.BlockSpec((1,) + bias.shape[1:], lambda bi, g, pid: (pid[g], 0, 0, 0)),
        ],
        out_specs=pl.BlockSpec((1, rq, HEAD_W), lambda bi, g, pid: (bi, g, 0)),
    )
    return pl.pallas_call(
        functools.partial(_na_kernel, rows=rows, kh=kh),
        grid_spec=grid_spec,
        out_shape=jax.ShapeDtypeStruct((b, n, HEAD_W), BF16),
        compiler_params=_cparams(("parallel", "arbitrary")),
        name="na_attention",
    )(pattern_ids, nq, nk, nv, nkc, nvc, bias)


def _na_patterns(rows):
    kh = min(NA_WIN_H, rows)
    key_rows = NA_ROW_GROUP + kh - 1
    assert rows % NA_ROW_GROUP == 0 and rows >= key_rows
    patterns, ids = [], []
    for g in range(rows // NA_ROW_GROUP):
        u = min(max(NA_ROW_GROUP * g - kh // 2, 0), rows - key_rows)
        geo = []
        for r in range(NA_ROW_GROUP * g, NA_ROW_GROUP * (g + 1)):
            rs = min(max(r - kh // 2, 0), rows - kh)
            geo.append((r - u, rs - u))
        geo = tuple(geo)
        if geo not in patterns:
            patterns.append(geo)
        ids.append(patterns.index(geo))
    return patterns, ids


def _na_bias_table(rpb, rows):
    kh = min(NA_WIN_H, rows)
    key_rows = NA_ROW_GROUP + kh - 1
    patterns, ids = _na_patterns(rows)
    geo = jnp.asarray(patterns, dtype=jnp.int32)
    rq_off, win_off = geo[..., 0], geo[..., 1]
    a = jnp.arange(key_rows)
    row_ok = (a >= win_off[..., None]) & (a < win_off[..., None] + kh)
    row_idx = jnp.clip(a - rq_off[..., None] + NA_WIN_H - 1, 0, 2 * NA_WIN_H - 2)
    cols = jnp.arange(GRID_W)
    col_start = jnp.clip(cols - NA_WIN_W // 2, 0, GRID_W - NA_WIN_W)
    kc = jnp.arange(GRID_W)
    col_ok = (kc[None, :] >= col_start[:, None]) & (kc[None, :] < col_start[:, None] + NA_WIN_W)
    col_idx = jnp.clip(kc[None, :] - cols[:, None] + NA_WIN_W - 1, 0, 2 * NA_WIN_W - 2)
    t = rpb[:, row_idx]
    pick = (jnp.arange(2 * NA_WIN_W - 1)[:, None, None] == col_idx[None]).astype(F32)
    t = jnp.einsum('hpraj,jck->hprack', t, pick, precision=lax.Precision.HIGHEST)
    ok = row_ok[None, :, :, :, None, None] & col_ok[None, None, None, None]
    t = jnp.where(ok, t, MASK_VALUE)
    t = t.transpose(1, 0, 2, 4, 3, 5)
    n_pat = len(patterns)
    t = t.reshape(n_pat, N_PAIRS, 2 * NA_ROW_GROUP * GRID_W, key_rows * GRID_W)
    return t.astype(F32), jnp.asarray(ids, dtype=jnp.int32)


def _ctx_na_kernel(q_ref, k_ref, v_ref, o_ref):
    tq = q_ref.shape[1]
    lane = lax.broadcasted_iota(jnp.int32, (tq, LANES), 1)
    for pr in range(N_PAIRS):
        cols = slice(pr * LANES, (pr + 1) * LANES)
        q2 = q_ref[0, :, cols]
        kb = k_ref[0, :, cols]
        vb = v_ref[0, :, cols]
        acc = jnp.zeros((tq, LANES), F32)
        for hh in range(2):
            in_head = (lane >= hh * NA_HEAD_DIM) & (lane < (hh + 1) * NA_HEAD_DIM)
            qm = jnp.where(in_head, q2, jnp.zeros_like(q2))
            s = _dot_nt(qm, kb)
            p = jnp.exp(s - jnp.max(s, axis=-1, keepdims=True))
            l = jnp.sum(p, axis=-1, keepdims=True)
            acc = jnp.where(in_head, _dot(p.astype(BF16), vb) / l, acc)
        o_ref[0, :, cols] = acc.astype(BF16)


def _ctx_na_attention(q, k, v):
    b, n, _ = q.shape
    spec = pl.BlockSpec((1, n, HEAD_W), lambda bi: (bi, 0, 0))
    return pl.pallas_call(
        _ctx_na_kernel,
        grid=(b,),
        in_specs=[spec, spec, spec],
        out_specs=spec,
        out_shape=jax.ShapeDtypeStruct((b, n, HEAD_W), BF16),
        compiler_params=_cparams(("parallel",)),
        name="ctx_na_attention",
    )(q, k, v)


def _diff_kernel(q_ref, lam_ref, g_ref, *refs, lam_init, n_src):
    src_refs, o_ref, k_ref, v_ref = refs[:2 * n_src], refs[2 * n_src], refs[-2], refs[-1]

    @pl.when(pl.program_id(1) == 0)
    def _():
        off = 0
        for j in range(n_src):
            n = src_refs[2 * j].shape[1]
            k_ref[off:off + n] = src_refs[2 * j][0]
            v_ref[off:off + n] = src_refs[2 * j + 1][0]
            off += n

    tq = q_ref.shape[1]
    lp = lam_ref[...]
    lam = (jnp.exp(jnp.sum(lp[0:1] * lp[1:2], axis=-1, keepdims=True))
           - jnp.exp(jnp.sum(lp[2:3] * lp[3:4], axis=-1, keepdims=True)) + lam_init)
    lane = lax.broadcasted_iota(jnp.int32, (tq, LANES), 1)
    for pr in range(N_PAIRS):
        cols = slice(pr * LANES, (pr + 1) * LANES)
        q2 = q_ref[0, :, cols]
        zero = jnp.zeros_like(q2)
        outp = jnp.zeros((tq, LANES), F32)
        for hh in range(2):
            comp = []
            for c in range(2):
                lo = hh * DF_V_DIM + c * DF_QK_DIM
                qm = jnp.where((lane >= lo) & (lane < lo + DF_QK_DIM), q2, zero)
                s = _dot_nt(qm, k_ref[:, cols])
                p = jnp.exp2(s - jnp.max(s, axis=-1, keepdims=True)).astype(BF16)
                res = _dot(p, v_ref[:, 2 * pr * LANES:2 * (pr + 1) * LANES])
                comp.append(res[:, :LANES] / res[:, LANES:LANES + 1])
            in_head = (lane >= hh * DF_V_DIM) & (lane < (hh + 1) * DF_V_DIM)
            oh = jnp.where(in_head, comp[0] - lam * comp[1], 0.0)
            ms = jnp.sum(oh * oh, axis=-1, keepdims=True) * (1.0 / DF_V_DIM)
            outp = outp + oh * lax.rsqrt(ms + EPS) * g_ref[...] * (1.0 - lam_init)
        o_ref[0, :, cols] = outp.astype(BF16)


def _diff_attention(dq, kvs, lam_p, subln_g, lam_init):
    b, nq, _ = dq.shape
    tq = min(256, nq)
    nk = sum(k.shape[1] for k, _ in kvs)
    kv_specs, kv_args = [], []
    for k, v in kvs:
        kv_specs += [pl.BlockSpec((1, k.shape[1], HEAD_W), lambda bi, i: (bi, 0, 0)),
                     pl.BlockSpec((1, k.shape[1], 2 * HEAD_W), lambda bi, i: (bi, 0, 0))]
        kv_args += [k, v]
    return pl.pallas_call(
        functools.partial(_diff_kernel, lam_init=lam_init, n_src=len(kvs)),
        grid=(b, nq // tq),
        in_specs=[
            pl.BlockSpec((1, tq, HEAD_W), lambda bi, i: (bi, i, 0)),
            pl.BlockSpec(lam_p.shape, lambda bi, i: (0, 0)),
            pl.BlockSpec(subln_g.shape, lambda bi, i: (0, 0)),
        ] + kv_specs,
        out_specs=pl.BlockSpec((1, tq, HEAD_W), lambda bi, i: (bi, i, 0)),
        out_shape=jax.ShapeDtypeStruct((b, nq, HEAD_W), BF16),
        scratch_shapes=[pltpu.VMEM((nk, HEAD_W), BF16), pltpu.VMEM((nk, 2 * HEAD_W), BF16)],
        compiler_params=_cparams(("parallel", "arbitrary")),
        name="diff_attention",
    )(dq, lam_p, subln_g, *kv_args)


def _merge_kernel(x_ref, f_ref, na_ref, df_ref, gate_ref, g1_ref, wft_ref, wna_ref, wdf_ref, wout_ref,
                  ng_ref, sh_ref, sc_ref, wrh_ref, wrl_ref, xo_ref, h2_ref, lg_ref):
    d = x_ref.shape[2]
    y_ft = _dot(f_ref[0], wft_ref[...])
    y_na = _dot(na_ref[0], wna_ref[...])
    y_df = _dot(df_ref[0], wdf_ref[...])
    m = (gate_ref[0, :, 0:d].astype(F32) * y_ft + gate_ref[0, :, d:2 * d].astype(F32) * y_na
         + gate_ref[0, :, 2 * d:3 * d].astype(F32) * y_df)
    y = _dot(m.astype(BF16), wout_ref[...])
    xn = x_ref[0] + g1_ref[0] * y
    xo_ref[0] = xn
    ms = jnp.mean(xn * xn, axis=-1, keepdims=True)
    h2 = (xn * lax.rsqrt(ms + EPS) * ng_ref[...]) * (1.0 + sc_ref[0]) + sh_ref[0]
    h_hi, h_lo = _split_bf16(h2)
    h2_ref[0] = h_hi
    lg_ref[0] = _dot(h_hi, wrh_ref[...]) + _dot(h_lo, wrh_ref[...]) + _dot(h_hi, wrl_ref[...])


def _merge(x, f, o_na, o_df, gates, g1, w_ft, w_na_o, w_df_o, w_out, norm_g, sh2, sc2, wr_hi, wr_lo):
    b, n, d = x.shape
    tm = min(512, n)
    tok = lambda w: pl.BlockSpec((1, tm, w), lambda bi, i: (bi, i, 0))
    full = lambda a: pl.BlockSpec(a.shape, lambda bi, i: (0,) * a.ndim)
    mod = pl.BlockSpec((1, 1, d), lambda bi, i: (bi, 0, 0))
    return pl.pallas_call(
        _merge_kernel,
        grid=(b, n // tm),
        in_specs=[tok(d), tok(FT_WIDTH), tok(HEAD_W), tok(HEAD_W), tok(N_BRANCHES * d), mod,
                  full(w_ft), full(w_na_o), full(w_df_o), full(w_out), full(norm_g), mod, mod,
                  full(wr_hi), full(wr_lo)],
        out_specs=[tok(d), tok(d), tok(ROUTER_PAD)],
        out_shape=[jax.ShapeDtypeStruct((b, n, d), F32), jax.ShapeDtypeStruct((b, n, d), BF16),
                   jax.ShapeDtypeStruct((b, n, ROUTER_PAD), F32)],
        compiler_params=_cparams(("parallel", "parallel")),
        name="merge",
    )(x, f, o_na, o_df, gates, g1, w_ft, w_na_o, w_df_o, w_out, norm_g, sh2, sc2, wr_hi, wr_lo)


ROUTE_TILE = LANES
SLOT_ALIGN = 16
ONE_BITS = 0x3F800000


FAST_WINDOW = 48


def _slot_window(cap):
    return min(ROUTE_TILE + SLOT_ALIGN, cap)


def _fast_window(cap):
    return min(FAST_WINDOW, cap)


def _slot_windows(lo, cap):
    b, nt, _ = lo.shape
    filled = jnp.concatenate([lo[:, 1:], jnp.full((b, 1, N_EXPERTS), cap, lo.dtype)], axis=1)
    aligned = (lo // SLOT_ALIGN) * SLOT_ALIGN
    lo_slow = jnp.minimum(aligned, cap - _slot_window(cap))
    lo_fast = jnp.minimum(aligned, cap - _fast_window(cap))
    fast_ok = jnp.all(filled - lo_fast <= _fast_window(cap), axis=-1)
    return {"lo_slow": lo_slow.reshape(-1), "lo_fast": lo_fast.reshape(-1),
            "fast_ok": fast_ok.astype(jnp.int32).reshape(-1)}


def _route_kernel(lg_ref, pos_ref, aff_ref, lo_ref, bits_ref, sel_ref, packed_ref, *, cap):
    n = lg_ref.shape[1]
    nb = n // ROUTE_TILE
    lane = lax.broadcasted_iota(jnp.int32, (n, LANES), 1)
    z = jnp.where(lane < N_EXPERTS, lg_ref[0], MASK_VALUE)
    p = jnp.exp(z - jnp.max(z, axis=-1, keepdims=True))
    aff = p / jnp.sum(p, axis=-1, keepdims=True)
    aff_ref[0] = aff
    bits_ref[...] = pltpu.bitcast(aff, jnp.int32)

    groups = LANES // N_EXPERTS
    rows_p = n // groups
    packed = bits_ref[0:rows_p]
    for j in range(1, groups):
        packed = packed + pltpu.roll(bits_ref[j * rows_p:(j + 1) * rows_p], N_EXPERTS * j, 1)
    packed_ref[...] = packed

    def count(mask):
        part = jnp.broadcast_to(jnp.sum(jnp.where(mask, 1.0, 0.0), axis=0, keepdims=True), (8, LANES))
        tot = part
        for j in range(1, groups):
            tot = tot + pltpu.roll(part, N_EXPERTS * j, 1)
        return tot[0:1]

    def bisect(_, carry):
        lo, hi = carry
        mid = (lo + hi) >> 1
        ge = count(packed_ref[...] >= mid) >= cap
        return jnp.where(ge, mid, lo), jnp.where(ge, hi, mid)

    lo0 = jnp.zeros((1, LANES), jnp.int32)
    hi0 = jnp.full((1, LANES), ONE_BITS + 1, jnp.int32)
    thr, _ = lax.fori_loop(0, 31, bisect, (lo0, hi0))
    need = cap - count(packed_ref[...] > thr)

    row = lax.broadcasted_iota(jnp.int32, (ROUTE_TILE, ROUTE_TILE), 0)
    col = lax.broadcasted_iota(jnp.int32, (ROUTE_TILE, ROUTE_TILE), 1)
    tri = jnp.where(row >= col, 1.0, 0.0).astype(BF16)

    carry = jnp.zeros((1, LANES), F32)
    for blk in range(nb):
        rows = slice(blk * ROUTE_TILE, (blk + 1) * ROUTE_TILE)
        bb = bits_ref[rows]
        eq = jnp.where(bb == thr, 1.0, 0.0)
        incl = _dot(tri, eq.astype(BF16))
        before = incl - eq + carry
        take = jnp.where(before < need, eq, 0.0)
        sel_ref[rows] = jnp.where(bb > thr, 1.0, take)
        carry = carry + incl[ROUTE_TILE - 1:ROUTE_TILE]

    carry = jnp.zeros((1, LANES), F32)
    for blk in range(nb):
        rows = slice(blk * ROUTE_TILE, (blk + 1) * ROUTE_TILE)
        sel = sel_ref[rows]
        incl = _dot(tri, sel.astype(BF16))
        pos_ref[0, rows] = jnp.where(sel > 0.0, carry + incl - sel, -1.0)
        lo_ref[0, blk:blk + 1] = carry.astype(jnp.int32)
        carry = carry + incl[ROUTE_TILE - 1:ROUTE_TILE]


def _route(logits, cap):
    b, n, _ = logits.shape
    nt = n // ROUTE_TILE
    tok = pl.BlockSpec((1, n, LANES), lambda bi: (bi, 0, 0))
    return pl.pallas_call(
        functools.partial(_route_kernel, cap=cap),
        grid=(b,),
        in_specs=[tok],
        out_specs=[tok, tok, pl.BlockSpec((1, nt, LANES), lambda bi: (bi, 0, 0))],
        out_shape=[jax.ShapeDtypeStruct((b, n, LANES), F32), jax.ShapeDtypeStruct((b, n, LANES), F32),
                   jax.ShapeDtypeStruct((b, nt, LANES), jnp.int32)],
        scratch_shapes=[pltpu.VMEM((n, LANES), jnp.int32), pltpu.VMEM((n, LANES), F32),
                        pltpu.VMEM((n // (LANES // N_EXPERTS), LANES), jnp.int32)],
        compiler_params=_cparams(("parallel",)),
        name="route",
    )(logits)


DISPATCH_COLS = 256


def _dispatch_kernel(lo_fast_ref, lo_slow_ref, fast_ref, h_ref, pos_ref, o_ref, *, win_fast, win_slow):
    b = pl.program_id(0)
    step = pl.program_id(1)
    per_step = pos_ref.shape[1]
    nt = pl.num_programs(1) * per_step

    @pl.when(step == 0)
    def _():
        o_ref[...] = jnp.zeros_like(o_ref)

    def run(sub, t, win, lo_ref):
        slot = lax.broadcasted_iota(jnp.int32, (win, ROUTE_TILE), 0).astype(F32)
        pos_t = pos_ref[0, sub]
        los = [pl.multiple_of(lo_ref[(b * nt + t) * N_EXPERTS + e], SLOT_ALIGN) for e in range(N_EXPERTS)]
        onehot = jnp.concatenate(
            [jnp.where(pos_t[e:e + 1, :] - los[e].astype(F32) == slot, 1.0, 0.0).astype(BF16)
             for e in range(N_EXPERTS)], axis=0)
        tok_rows = slice(sub * ROUTE_TILE, (sub + 1) * ROUTE_TILE)
        for c0 in range(0, h_ref.shape[2], DISPATCH_COLS):
            cols = slice(c0, c0 + DISPATCH_COLS)
            res = _dot(onehot, h_ref[0, tok_rows, cols]).astype(BF16)
            for e in range(N_EXPERTS):
                rows = pl.ds(los[e], win)
                o_ref[0, e, rows, cols] = o_ref[0, e, rows, cols] + res[e * win:(e + 1) * win]

    for sub in range(per_step):
        t = step * per_step + sub
        if win_fast == win_slow:
            run(sub, t, win_slow, lo_slow_ref)
        else:
            fast = fast_ref[b * nt + t] == 1
            pl.when(fast)(functools.partial(run, sub, t, win_fast, lo_fast_ref))
            pl.when(jnp.logical_not(fast))(functools.partial(run, sub, t, win_slow, lo_slow_ref))


def _tiles_per_step(nt):
    return 2 if nt % 2 == 0 else 1


def _dispatch(windows, h2, pos_t, cap):
    b, n, d = h2.shape
    nt = n // ROUTE_TILE
    per_step = _tiles_per_step(nt)
    grid_spec = pltpu.PrefetchScalarGridSpec(
        num_scalar_prefetch=3,
        grid=(b, nt // per_step),
        in_specs=[
            pl.BlockSpec((1, per_step * ROUTE_TILE, d), lambda bi, t, *_: (bi, t, 0)),
            pl.BlockSpec((1, per_step, N_EXPERTS, ROUTE_TILE), lambda bi, t, *_: (bi, t, 0, 0)),
        ],
        out_specs=pl.BlockSpec((1, N_EXPERTS, cap, d), lambda bi, t, *_: (bi, 0, 0, 0)),
    )
    return pl.pallas_call(
        functools.partial(_dispatch_kernel, win_fast=_fast_window(cap), win_slow=_slot_window(cap)),
        grid_spec=grid_spec,
        out_shape=jax.ShapeDtypeStruct((b, N_EXPERTS, cap, d), BF16),
        compiler_params=_cparams(("arbitrary", "arbitrary")),
        name="dispatch",
    )(windows["lo_fast"], windows["lo_slow"], windows["fast_ok"], h2, pos_t)


def _expert_kernel(*refs, f_chunk, n_sets):
    x_refs, (wg_ref, wu_ref, wd_ref), o_refs = refs[:n_sets], refs[n_sets:n_sets + 3], refs[n_sets + 3:]
    xs = [r[0, 0] for r in x_refs]
    x = xs[0] if n_sets == 1 else jnp.concatenate(xs, axis=0)
    ff = wg_ref.shape[3]
    acc = jnp.zeros(x.shape, F32)
    for f0 in range(0, ff, f_chunk):
        a = _dot(x, wg_ref[0, 0, :, f0:f0 + f_chunk])
        u = _dot(x, wu_ref[0, 0, :, f0:f0 + f_chunk])
        hm = (a * jax.nn.sigmoid(a) * u).astype(BF16)
        acc = acc + _dot(hm, wd_ref[0, 0, f0:f0 + f_chunk, :])
    off = 0
    for xr, o_ref in zip(xs, o_refs):
        o_ref[0, 0] = acc[off:off + xr.shape[0]].astype(BF16)
        off += xr.shape[0]


def _experts(xes, w_gate, w_up, w_down, layer):
    b, e, _, d = xes[0].shape
    ff = w_gate.shape[3]
    toks = [pl.BlockSpec((1, 1, xe.shape[2], d), lambda ei, bi: (bi, ei, 0, 0)) for xe in xes]
    return pl.pallas_call(
        functools.partial(_expert_kernel, f_chunk=min(512, ff), n_sets=len(xes)),
        grid=(e, b),
        in_specs=toks + [
            pl.BlockSpec((1, 1, d, ff), lambda ei, bi: (layer, ei, 0, 0)),
            pl.BlockSpec((1, 1, d, ff), lambda ei, bi: (layer, ei, 0, 0)),
            pl.BlockSpec((1, 1, ff, d), lambda ei, bi: (layer, ei, 0, 0)),
        ],
        out_specs=toks,
        out_shape=[jax.ShapeDtypeStruct(xe.shape, BF16) for xe in xes],
        compiler_params=_cparams(("arbitrary", "arbitrary")),
        name="experts",
    )(*xes, w_gate, w_up, w_down)


def _combine_kernel(lo_fast_ref, lo_slow_ref, fast_ref, x_ref, y_ref, pos_ref, aff_ref, g_ref, spread_ref, slot_ref,
                    o_ref, *, win_fast, win_slow):
    b = pl.program_id(0)
    step = pl.program_id(1)
    tile = ROUTE_TILE
    per_step = x_ref.shape[1] // tile
    nt = pl.num_programs(1) * per_step

    def wide(rows, base):
        slot = lax.broadcasted_iota(jnp.int32, (tile, win_slow), 1).astype(F32)
        pos = pos_ref[0, rows]
        aff = aff_ref[0, rows]
        acc = jnp.zeros((tile, x_ref.shape[2]), F32)
        for e in range(N_EXPERTS):
            lo = pl.multiple_of(lo_slow_ref[base + e], SLOT_ALIGN)
            pick = jnp.where(pos[:, e:e + 1] - lo.astype(F32) == slot, aff[:, e:e + 1], 0.0).astype(BF16)
            acc = acc + _dot(pick, y_ref[0, e, pl.ds(lo, win_slow), :])
        o_ref[0, rows] = x_ref[0, rows] + g_ref[0] * acc

    def narrow(rows, base):
        lane = lax.broadcasted_iota(jnp.int32, (1, LANES), 1)
        lo_vec = jnp.zeros((1, LANES), F32)
        los = []
        for e in range(N_EXPERTS):
            lo = pl.multiple_of(lo_fast_ref[base + e], SLOT_ALIGN)
            los.append(lo)
            lo_vec = jnp.where(lane == e, lo.astype(F32), lo_vec)
        pos = pos_ref[0, rows]
        rel = jnp.where(pos >= 0.0, pos - lo_vec, -1.0).astype(BF16)
        rel_wide = _dot(rel, spread_ref[...])
        gate_wide = _dot(aff_ref[0, rows].astype(BF16), spread_ref[...])
        pick = jnp.where(rel_wide == slot_ref[...], gate_wide, 0.0).astype(BF16)
        y_cat = jnp.concatenate([y_ref[0, e, pl.ds(los[e], win_fast), :] for e in range(N_EXPERTS)], axis=0)
        o_ref[0, rows] = x_ref[0, rows] + g_ref[0] * _dot(pick, y_cat)

    for sub in range(per_step):
        t = step * per_step + sub
        rows = slice(sub * tile, (sub + 1) * tile)
        base = (b * nt + t) * N_EXPERTS
        if win_fast == win_slow:
            wide(rows, base)
        else:
            fast = fast_ref[b * nt + t] == 1
            pl.when(fast)(functools.partial(narrow, rows, base))
            pl.when(jnp.logical_not(fast))(functools.partial(wide, rows, base))


def _combine(windows, x, ye, pos, aff, g2, cap):
    b, n, d = x.shape
    e = ye.shape[1]
    nt = n // ROUTE_TILE
    per_step = _tiles_per_step(nt)
    tok = lambda w: pl.BlockSpec((1, per_step * ROUTE_TILE, w), lambda bi, t, *_: (bi, t, 0))
    win_fast = _fast_window(cap)
    j = jnp.arange(N_EXPERTS * win_fast)
    spread = (jnp.arange(LANES)[:, None] == (j // win_fast)[None, :]).astype(BF16)
    slot = (j % win_fast).astype(F32)[None, :]
    grid_spec = pltpu.PrefetchScalarGridSpec(
        num_scalar_prefetch=3,
        grid=(b, nt // per_step),
        in_specs=[
            tok(d),
            pl.BlockSpec((1, e, cap, d), lambda bi, t, *_: (bi, 0, 0, 0)),
            tok(LANES),
            tok(LANES),
            pl.BlockSpec((1, 1, d), lambda bi, t, *_: (bi, 0, 0)),
            pl.BlockSpec(spread.shape, lambda bi, t, *_: (0, 0)),
            pl.BlockSpec(slot.shape, lambda bi, t, *_: (0, 0)),
        ],
        out_specs=tok(d),
    )
    return pl.pallas_call(
        functools.partial(_combine_kernel, win_fast=win_fast, win_slow=_slot_window(cap)),
        grid_spec=grid_spec,
        out_shape=jax.ShapeDtypeStruct((b, n, d), F32),
        compiler_params=_cparams(("arbitrary", "arbitrary")),
        name="combine",
    )(windows["lo_fast"], windows["lo_slow"], windows["fast_ok"], x, ye, pos, aff, g2, spread, slot)


def _route_and_dispatch(h2, logits):
    b, n, _ = h2.shape
    cap = EC_CAPACITY_FACTOR * n // N_EXPERTS
    nt = n // ROUTE_TILE
    assert n % ROUTE_TILE == 0 and cap % SLOT_ALIGN == 0
    pos, aff, lo = _route(logits, cap)
    windows = _slot_windows(lo[:, :, :N_EXPERTS], cap)
    pos_t = pos[:, :, :N_EXPERTS].reshape(b, nt, ROUTE_TILE, N_EXPERTS).transpose(0, 1, 3, 2)
    xe = _dispatch(windows, h2, pos_t, cap)
    return xe, {"windows": windows, "pos": pos, "aff": aff, "cap": cap}


def _combine_residual(x, ye, routing, g2):
    return _combine(routing["windows"], x, ye, routing["pos"], routing["aff"], g2, routing["cap"])


def _dft_mats(n):
    k = jnp.arange(n, dtype=jnp.int32)
    ang = ((k[:, None] * k[None, :]) % n).astype(F32) * (2.0 * math.pi / n)
    return jnp.cos(ang), jnp.sin(ang)


def _dft_mats_bf16(n, block=64):
    assert n % block == 0
    k = jnp.arange(n // 2, dtype=jnp.int32)
    a = jnp.arange(n // block, dtype=jnp.int32) * block
    b = jnp.arange(block, dtype=jnp.int32)
    ang_a = ((a[:, None] * k[None, :]) % n).astype(F32) * (2.0 * math.pi / n)
    ang_b = ((b[:, None] * k[None, :]) % n).astype(F32) * (2.0 * math.pi / n)
    ca, sa = jnp.cos(ang_a)[:, None, :], jnp.sin(ang_a)[:, None, :]
    cb, sb = jnp.cos(ang_b)[None], jnp.sin(ang_b)[None]
    cos_m = (ca * cb - sa * sb).reshape(n, n // 2)
    nsin_m = (-(sa * cb + ca * sb)).reshape(n, n // 2)
    return cos_m.astype(BF16), nsin_m.astype(BF16)


def _chan_dft():
    c, s = _dft_mats(FT_GROUP_DIM)
    eye = jnp.eye(FT_GROUPS, dtype=F32)
    return jnp.concatenate([jnp.kron(eye, c), jnp.kron(eye, s)], axis=1).astype(BF16)


def _group_mean_mat(group):
    gid = jnp.arange(HEAD_W) // group
    return jnp.where(gid[:, None] == gid[None, :], 1.0 / group, 0.0).astype(BF16)


def _rope_tables(n):
    t = jnp.arange(n)
    row = (t // GRID_W).astype(F32)
    col = (t % GRID_W).astype(F32)
    ax = DF_QK_DIM // 2
    inv = ROPE_BASE ** (-jnp.arange(0, ax, 2, dtype=F32) / ax)
    lane = jnp.arange(LANES)
    freq = inv[lane % (ax // 2)]
    pos = jnp.where(((lane % DF_QK_DIM) < ax)[None, :], row[:, None], col[:, None])
    ang = pos * freq[None, :]
    sign = jnp.where((lane % ax) < ax // 2, -1.0, 1.0)
    return jnp.cos(ang), jnp.sin(ang) * sign[None, :]


def kernel(x, c, ctx, c_ctx, norm1_g, norm2_g, w_ada, b_ada, w_in, na_qn_g, na_kn_g, na_rpb, df_qn_g, df_kn_g,
           df_lambda, df_subln_g, w_ft, w_na_o, w_df_o, w_out, w_router, w_gate, w_up, w_down):
    b, n, d = x.shape
    lc = ctx.shape[1]
    depth = w_ada.shape[0]
    assert b + 1 <= MOD_ROWS and n % GRID_W == 0

    cc = jnp.zeros((MOD_ROWS, d), F32).at[:b].set(c).at[b].set(c_ctx)
    mods = _ada(cc, w_ada, b_ada)

    consts = {"chan_dft": _chan_dft(), "g64": _group_mean_mat(NA_HEAD_DIM), "g32": _group_mean_mat(DF_QK_DIM)}
    rope_lat = _rope_tables(n)
    rope_ctx = (jnp.zeros((lc, LANES), F32), jnp.zeros((lc, LANES), F32))
    dft_lat = _dft_mats_bf16(n)
    dft_ctx = _dft_mats_bf16(lc)
    rows = n // GRID_W

    w_gate_b, w_up_b, w_down_b = w_gate.astype(BF16), w_up.astype(BF16), w_down.astype(BF16)

    xc = ctx
    for i in range(depth):
        last = i == depth - 1
        lam_init = 0.8 - 0.6 * math.exp(-0.3 * i)
        m_lat = mods[i, :b].reshape(b, 6, 1, d)
        m_ctx = jnp.broadcast_to(mods[i, b].reshape(1, 6, 1, d), (b, 6, 1, d))
        sh1, sc1, g1, sh2, sc2, g2 = [m_lat[:, j] for j in range(6)]
        csh1, csc1, cg1, csh2, csc2, cg2 = [m_ctx[:, j] for j in range(6)]

        w_in_b = w_in[i].astype(BF16)
        gains = {
            "naq": jnp.tile(na_qn_g[i], NA_HEADS).reshape(1, HEAD_W),
            "nak": jnp.tile(na_kn_g[i], NA_HEADS).reshape(1, HEAD_W),
            "dfq": jnp.tile(df_qn_g[i], 2 * DF_HEADS).reshape(1, HEAD_W),
            "dfk": jnp.tile(df_kn_g[i], 2 * DF_HEADS).reshape(1, HEAD_W),
        }
        n1g = norm1_g[i].reshape(1, d)
        n2g = norm2_g[i].reshape(1, d)
        subln = jnp.tile(df_subln_g[i], 2).reshape(1, LANES)
        w_ft_b, w_na_b, w_df_b, w_out_b = (w.astype(BF16) for w in (w_ft[i], w_na_o[i], w_df_o[i], w_out[i]))
        wr = jnp.zeros((d, ROUTER_PAD), F32).at[:, :N_EXPERTS].set(w_router[i])
        wr_hi, wr_lo = _split_bf16(wr)

        if last:
            nkc, nvc, dkc, dvc = _inproj(xc, csh1, csc1, n1g, w_in_b, consts, gains, rope_ctx,
                                         rope=False, kv_only=True)
        else:
            fabc, nqc, dqc, nkc, nvc, dkc, dvc, gatec = _inproj(xc, csh1, csc1, n1g, w_in_b, consts, gains,
                                                                 rope_ctx, rope=False, kv_only=False)

        fab, nq, dq, nk, nv, dk, dv, gate = _inproj(x, sh1, sc1, n1g, w_in_b, consts, gains, rope_lat,
                                                     rope=True, kv_only=False)
        f = _fourier(fab, *dft_lat)
        o_na = _na_attention(nq, nk, nv, nkc, nvc, *_na_bias_table(na_rpb[i], rows))
        o_df = _diff_attention(dq, [(dk, dv), (dkc, dvc)], df_lambda[i], subln, lam_init)
        x, h2, logits = _merge(x, f, o_na, o_df, gate, g1, w_ft_b, w_na_b, w_df_b, w_out_b, n2g, sh2, sc2,
                               wr_hi, wr_lo)
        xe, routing = _route_and_dispatch(h2, logits)

        if last:
            (ye,) = _experts([xe], w_gate_b, w_up_b, w_down_b, i)
        else:
            fc = _fourier(fabc, *dft_ctx)
            o_nac = _ctx_na_attention(nqc, nkc, nvc)
            o_dfc = _diff_attention(dqc, [(dkc, dvc)], df_lambda[i], subln, lam_init)
            xc, hc2, logits_c = _merge(xc, fc, o_nac, o_dfc, gatec, cg1, w_ft_b, w_na_b, w_df_b, w_out_b, n2g,
                                       csh2, csc2, wr_hi, wr_lo)
            xec, routing_c = _route_and_dispatch(hc2, logits_c)
            ye, yec = _experts([xe, xec], w_gate_b, w_up_b, w_down_b, i)
            xc = _combine_residual(xc, yec, routing_c, cg2)
        x = _combine_residual(x, ye, routing, g2)
    return x
```

```python
import functools
import math

import jax
import jax.numpy as jnp
from jax import lax
from jax.experimental import pallas as pl
from jax.experimental.pallas import tpu as pltpu

F32 = jnp.float32
BF16 = jnp.bfloat16

GRID_W = 64
FT_GROUPS = 4
FT_GROUP_DIM = 64
FT_WIDTH = FT_GROUPS * FT_GROUP_DIM
NA_HEADS = 6
NA_HEAD_DIM = 64
NA_WIDTH = NA_HEADS * NA_HEAD_DIM
NA_WIN_H = 8
NA_WIN_W = 16
DF_HEADS = 6
DF_QK_DIM = 32
DF_V_DIM = 2 * DF_QK_DIM
DF_QK_WIDTH = DF_HEADS * 2 * DF_QK_DIM
DF_WIDTH = DF_HEADS * DF_V_DIM
N_BRANCHES = 3
N_EXPERTS = 16
EC_CAPACITY_FACTOR = 2
ROPE_BASE = 10000.0
EPS = 1e-6
MASK_VALUE = -1e30

LANES = 128
VMEM_LIMIT_BYTES = 56 * 1024 * 1024

HEAD_W = 384
N_PAIRS = HEAD_W // LANES
MOD_ROWS = 16
ROUTER_PAD = LANES


def _cparams(sem):
    return pltpu.CompilerParams(dimension_semantics=sem, vmem_limit_bytes=VMEM_LIMIT_BYTES)


def _dot(a, b):
    return jnp.dot(a, b, preferred_element_type=F32)


def _dot_nt(a, b):
    return lax.dot_general(a, b, (((1,), (1,)), ((), ())), preferred_element_type=F32)


def _split_bf16(v):
    hi = v.astype(BF16)
    lo = (v - hi.astype(F32)).astype(BF16)
    return hi, lo


def _ada_kernel(c_ref, w_ref, b_ref, o_ref):
    c = c_ref[...]
    a = c * jax.nn.sigmoid(c)
    a_hi, a_lo = _split_bf16(a)
    w_hi, w_lo = _split_bf16(w_ref[0])
    acc = _dot(a_hi, w_hi) + _dot(a_lo, w_hi) + _dot(a_hi, w_lo)
    o_ref[0] = acc + b_ref[0]


def _ada(cc, w_ada, b_ada):
    depth, d, d6 = w_ada.shape
    tn = 512
    return pl.pallas_call(
        _ada_kernel,
        grid=(depth, d6 // tn),
        in_specs=[
            pl.BlockSpec((MOD_ROWS, d), lambda l, j: (0, 0)),
            pl.BlockSpec((1, d, tn), lambda l, j: (l, 0, j)),
            pl.BlockSpec((1, 1, tn), lambda l, j: (l, 0, j)),
        ],
        out_specs=pl.BlockSpec((1, MOD_ROWS, tn), lambda l, j: (l, 0, j)),
        out_shape=jax.ShapeDtypeStruct((depth, MOD_ROWS, d6), F32),
        compiler_params=_cparams(("arbitrary", "arbitrary")),
        name="ada",
    )(cc, w_ada, b_ada.reshape(depth, 1, d6))


OFF_Q = FT_WIDTH
OFF_DQ = OFF_Q + NA_WIDTH
OFF_KV = OFF_DQ + DF_QK_WIDTH
OFF_NV = OFF_KV + NA_WIDTH
OFF_DK = OFF_NV + NA_WIDTH
OFF_DV = OFF_DK + DF_QK_WIDTH
OFF_GATE = OFF_DV + DF_WIDTH


def _group_rms(v, gmat_ref, gain_ref):
    ms = _dot((v * v).astype(BF16), gmat_ref[...])
    return v * lax.rsqrt(ms + EPS) * gain_ref[...]


def _rope_chunk(vj, cos, sin_signed, first_half):
    fwd = pltpu.roll(vj, LANES - 8, 1)
    bwd = pltpu.roll(vj, 8, 1)
    partner = jnp.where(first_half, fwd, bwd)
    return vj * cos + partner * sin_signed


def _inproj_kernel(x_ref, sh_ref, sc_ref, ng_ref, w_ref, cd_ref, g64_ref, g32_ref,
                   naq_g_ref, nak_g_ref, dfq_g_ref, dfk_g_ref, cos_ref, sin_ref,
                   *out_refs, rope, kv_only):
    x = x_ref[0]
    ms = jnp.mean(x * x, axis=-1, keepdims=True)
    y = x * lax.rsqrt(ms + EPS) * ng_ref[...]
    h = (y * (1.0 + sc_ref[0]) + sh_ref[0]).astype(BF16)

    def proj(c0, c1):
        return _dot(h, w_ref[:, c0:c1])

    if rope:
        lane = lax.broadcasted_iota(jnp.int32, (x.shape[0], LANES), 1)
        first_half = (lane & 8) == 0
        cos = cos_ref[...]
        sin_signed = sin_ref[...]

    def df_qk(v, gain_ref, scale, o_ref):
        v = _group_rms(v, g32_ref, gain_ref)
        for j in range(N_PAIRS):
            vj = v[:, j * LANES:(j + 1) * LANES]
            if rope:
                vj = _rope_chunk(vj, cos, sin_signed, first_half)
            o_ref[0, :, j * LANES:(j + 1) * LANES] = (vj * scale).astype(BF16)

    if kv_only:
        nk_ref, nv_ref, dk_ref, dv_ref = out_refs
    else:
        fab_ref, nq_ref, dq_ref, nk_ref, nv_ref, dk_ref, dv_ref, gate_ref = out_refs
        u = proj(0, FT_WIDTH)
        fab_ref[0] = _dot(u.astype(BF16), cd_ref[...]).astype(BF16)
        nq = _group_rms(proj(OFF_Q, OFF_DQ), g64_ref, naq_g_ref)
        nq_ref[0] = (nq * (NA_HEAD_DIM ** -0.5)).astype(BF16)
        df_qk(proj(OFF_DQ, OFF_KV), dfq_g_ref, DF_QK_DIM ** -0.5 * math.log2(math.e), dq_ref)
        d = x.shape[1]
        for j in range(N_BRANCHES):
            z = proj(OFF_GATE + j * d, OFF_GATE + (j + 1) * d)
            gate_ref[0, :, j * d:(j + 1) * d] = jax.nn.sigmoid(z).astype(BF16)

    nk_ref[0] = _group_rms(proj(OFF_KV, OFF_NV), g64_ref, nak_g_ref).astype(BF16)
    nv_ref[0] = proj(OFF_NV, OFF_DK).astype(BF16)
    df_qk(proj(OFF_DK, OFF_DV), dfk_g_ref, 1.0, dk_ref)
    dv = proj(OFF_DV, OFF_GATE).astype(BF16)
    for j in range(N_PAIRS):
        dv_ref[0, :, 2 * j * LANES:(2 * j + 1) * LANES] = dv[:, j * LANES:(j + 1) * LANES]
        dv_ref[0, :, (2 * j + 1) * LANES:(2 * j + 2) * LANES] = jnp.ones((dv.shape[0], LANES), BF16)


def _inproj(x, sh, sc, norm_g, w_in, consts, gains, rope_tabs, *, rope, kv_only):
    b, n, d = x.shape
    tm = min(512, n)
    tok = lambda w: pl.BlockSpec((1, tm, w), lambda bi, i: (bi, i, 0))
    full = lambda a: pl.BlockSpec(a.shape, lambda bi, i: (0,) * a.ndim)
    mod = pl.BlockSpec((1, 1, d), lambda bi, i: (bi, 0, 0))
    cos_t, sin_t = rope_tabs
    tab = pl.BlockSpec((tm, LANES), lambda bi, i: (i, 0))
    slab = jax.ShapeDtypeStruct((b, n, HEAD_W), BF16)
    slab_aug = jax.ShapeDtypeStruct((b, n, 2 * HEAD_W), BF16)
    if kv_only:
        out_shape = [slab] * 3 + [slab_aug]
        out_specs = [tok(HEAD_W)] * 3 + [tok(2 * HEAD_W)]
    else:
        out_shape = [jax.ShapeDtypeStruct((b, n, 2 * FT_WIDTH), BF16)] + [slab] * 5 + [
            slab_aug, jax.ShapeDtypeStruct((b, n, N_BRANCHES * d), BF16)]
        out_specs = [tok(2 * FT_WIDTH)] + [tok(HEAD_W)] * 5 + [tok(2 * HEAD_W), tok(N_BRANCHES * d)]
    args = [x, sh, sc, norm_g, w_in, consts["chan_dft"], consts["g64"], consts["g32"],
            gains["naq"], gains["nak"], gains["dfq"], gains["dfk"], cos_t, sin_t]
    in_specs = [tok(d), mod, mod, full(norm_g), full(w_in), full(consts["chan_dft"]),
                full(consts["g64"]), full(consts["g32"]), full(gains["naq"]), full(gains["nak"]),
                full(gains["dfq"]), full(gains["dfk"]), tab, tab]
    return pl.pallas_call(
        functools.partial(_inproj_kernel, rope=rope, kv_only=kv_only),
        grid=(b, n // tm),
        in_specs=in_specs,
        out_specs=out_specs,
        out_shape=out_shape,
        compiler_params=_cparams(("parallel", "parallel")),
        name="inproj_kv" if kv_only else "inproj",
    )(*args)


def _fourier_kernel(c_ref, s_ref, head_ref, tail_ref, mid_ref, o_ref, acc_ref, *, scale):
    k = pl.program_id(1)

    @pl.when(k == 0)
    def _():
        acc_ref[...] = jnp.zeros_like(acc_ref)

    cm = c_ref[...]
    sm = s_ref[...]
    for bi in range(head_ref.shape[0]):
        head = head_ref[bi].astype(F32)
        tail = tail_ref[bi].astype(F32)
        af = (head[:, :FT_WIDTH] + tail[:, :FT_WIDTH]).astype(BF16)
        bf = (head[:, FT_WIDTH:] - tail[:, FT_WIDTH:]).astype(BF16)
        acc_ref[bi] += _dot(cm, af) + _dot(sm, bf)

    @pl.when(k == pl.num_programs(1) - 1)
    def _():
        row = lax.broadcasted_iota(jnp.int32, (acc_ref.shape[1], 1), 0)
        sign = jnp.where((row & 1) == 0, 1.0, -1.0)
        for bi in range(head_ref.shape[0]):
            o_ref[bi] = ((acc_ref[bi] + sign * mid_ref[bi].astype(F32)) * scale).astype(o_ref.dtype)


def _flip_kernel(x_ref, o_ref):
    t = x_ref.shape[1]
    row = lax.broadcasted_iota(jnp.int32, (t, t), 0)
    col = lax.broadcasted_iota(jnp.int32, (t, t), 1)
    anti = jnp.where(row + col == t - 1, 1.0, 0.0).astype(BF16)
    o_ref[0] = _dot(anti, x_ref[0]).astype(o_ref.dtype)


def _reverse_rows(x):
    b, m, w = x.shape
    t = min(512, m)
    nb = m // t
    return pl.pallas_call(
        _flip_kernel,
        grid=(b, nb),
        in_specs=[pl.BlockSpec((1, t, w), lambda bi, j: (bi, nb - 1 - j, 0))],
        out_specs=pl.BlockSpec((1, t, w), lambda bi, j: (bi, j, 0)),
        out_shape=jax.ShapeDtypeStruct(x.shape, x.dtype),
        compiler_params=_cparams(("parallel", "parallel")),
        name="reverse_rows",
    )(x)


def _fourier(fab, cos_m, nsin_m):
    b, n, _ = fab.shape
    half = n // 2
    tn = min(1024, n)
    tk = min(512, half)
    assert tn % 2 == 0
    scale = 1.0 / math.sqrt(n * FT_GROUP_DIM)
    head = fab[:, :half]
    tail = _reverse_rows(jnp.concatenate([fab[:, half + 1:], jnp.zeros_like(fab[:, :1])], axis=1))
    mid = fab[:, half:half + 1, :FT_WIDTH]
    return pl.pallas_call(
        functools.partial(_fourier_kernel, scale=scale),
        grid=(n // tn, half // tk),
        in_specs=[
            pl.BlockSpec((tn, tk), lambda i, k: (i, k)),
            pl.BlockSpec((tn, tk), lambda i, k: (i, k)),
            pl.BlockSpec((b, tk, 2 * FT_WIDTH), lambda i, k: (0, k, 0)),
            pl.BlockSpec((b, tk, 2 * FT_WIDTH), lambda i, k: (0, k, 0)),
            pl.BlockSpec((b, 1, FT_WIDTH), lambda i, k: (0, 0, 0)),
        ],
        out_specs=pl.BlockSpec((b, tn, FT_WIDTH), lambda i, k: (0, i, 0)),
        out_shape=jax.ShapeDtypeStruct((b, n, FT_WIDTH), BF16),
        scratch_shapes=[pltpu.VMEM((b, tn, FT_WIDTH), F32)],
        compiler_params=_cparams(("parallel", "arbitrary")),
        name="fourier",
    )(cos_m, nsin_m, head, tail, mid)


NA_ROW_GROUP = 4


def _na_kernel(pid_ref, q_ref, k_ref, v_ref, kc_ref, vc_ref, bias_ref, o_ref, *, rows, kh):
    del pid_ref
    g = pl.program_id(1)
    rq = NA_ROW_GROUP * GRID_W
    key_rows = NA_ROW_GROUP + kh - 1
    u = jnp.clip(NA_ROW_GROUP * g - kh // 2, 0, rows - key_rows)
    start = pl.multiple_of(u * GRID_W, GRID_W)
    lane = lax.broadcasted_iota(jnp.int32, (rq, LANES), 1)
    first = lane < NA_HEAD_DIM
    for pr in range(N_PAIRS):
        cols = slice(pr * LANES, (pr + 1) * LANES)
        q2 = q_ref[0, :, cols]
        zero = jnp.zeros_like(q2)
        qcat = jnp.concatenate([jnp.where(first, q2, zero), jnp.where(first, zero, q2)], axis=0)
        kb = k_ref[0, pl.ds(start, key_rows * GRID_W), cols]
        vb = v_ref[0, pl.ds(start, key_rows * GRID_W), cols]
        s1 = _dot_nt(qcat, kb) + bias_ref[0, pr]
        s2 = _dot_nt(qcat, kc_ref[0, :, cols])
        m = jnp.maximum(jnp.max(s1, axis=-1, keepdims=True), jnp.max(s2, axis=-1, keepdims=True))
        p1 = jnp.exp(s1 - m)
        p2 = jnp.exp(s2 - m)
        l = jnp.sum(p1, axis=-1, keepdims=True) + jnp.sum(p2, axis=-1, keepdims=True)
        o = (_dot(p1.astype(BF16), vb) + _dot(p2.astype(BF16), vc_ref[0, :, cols])) / l
        o_ref[0, :, cols] = jnp.where(first, o[:rq], o[rq:]).astype(BF16)


def _na_attention(nq, nk, nv, nkc, nvc, bias, pattern_ids):
    b, n, _ = nq.shape
    lc = nkc.shape[1]
    rows = n // GRID_W
    kh = min(NA_WIN_H, rows)
    rq = NA_ROW_GROUP * GRID_W
    grid_spec = pltpu.PrefetchScalarGridSpec(
        num_scalar_prefetch=1,
        grid=(b, rows // NA_ROW_GROUP),
        in_specs=[
            pl.BlockSpec((1, rq, HEAD_W), lambda bi, g, pid: (bi, g, 0)),
            pl.BlockSpec((1, n, HEAD_W), lambda bi, g, pid: (bi, 0, 0)),
            pl.BlockSpec((1, n, HEAD_W), lambda bi, g, pid: (bi, 0, 0)),
            pl.BlockSpec((1, lc, HEAD_W), lambda bi, g, pid: (bi, 0, 0)),
            pl.BlockSpec((1, lc, HEAD_W), lambda bi, g, pid: (bi, 0, 0)),
            pl.BlockSpec((1,) + bias.shape[1:], lambda bi, g, pid: (pid[g], 0, 0, 0)),
        ],
        out_specs=pl.BlockSpec((1, rq, HEAD_W), lambda bi, g, pid: (bi, g, 0)),
    )
    return pl.pallas_call(
        functools.partial(_na_kernel, rows=rows, kh=kh),
        grid_spec=grid_spec,
        out_shape=jax.ShapeDtypeStruct((b, n, HEAD_W), BF16),
        compiler_params=_cparams(("parallel", "arbitrary")),
        name="na_attention",
    )(pattern_ids, nq, nk, nv, nkc, nvc, bias)


def _na_patterns(rows):
    kh = min(NA_WIN_H, rows)
    key_rows = NA_ROW_GROUP + kh - 1
    assert rows % NA_ROW_GROUP == 0 and rows >= key_rows
    patterns, ids = [], []
    for g in range(rows // NA_ROW_GROUP):
        u = min(max(NA_ROW_GROUP * g - kh // 2, 0), rows - key_rows)
        geo = []
        for r in range(NA_ROW_GROUP * g, NA_ROW_GROUP * (g + 1)):
            rs = min(max(r - kh // 2, 0), rows - kh)
            geo.append((r - u, rs - u))
        geo = tuple(geo)
        if geo not in patterns:
            patterns.append(geo)
        ids.append(patterns.index(geo))
    return patterns, ids


def _na_bias_table(rpb, rows):
    kh = min(NA_WIN_H, rows)
    key_rows = NA_ROW_GROUP + kh - 1
    patterns, ids = _na_patterns(rows)
    geo = jnp.asarray(patterns, dtype=jnp.int32)
    rq_off, win_off = geo[..., 0], geo[..., 1]
    a = jnp.arange(key_rows)
    row_ok = (a >= win_off[..., None]) & (a < win_off[..., None] + kh)
    row_idx = jnp.clip(a - rq_off[..., None] + NA_WIN_H - 1, 0, 2 * NA_WIN_H - 2)
    cols = jnp.arange(GRID_W)
    col_start = jnp.clip(cols - NA_WIN_W // 2, 0, GRID_W - NA_WIN_W)
    kc = jnp.arange(GRID_W)
    col_ok = (kc[None, :] >= col_start[:, None]) & (kc[None, :] < col_start[:, None] + NA_WIN_W)
    col_idx = jnp.clip(kc[None, :] - cols[:, None] + NA_WIN_W - 1, 0, 2 * NA_WIN_W - 2)
    t = rpb[:, row_idx]
    pick = (jnp.arange(2 * NA_WIN_W - 1)[:, None, None] == col_idx[None]).astype(F32)
    t = jnp.einsum('hpraj,jck->hprack', t, pick, precision=lax.Precision.HIGHEST)
    ok = row_ok[None, :, :, :, None, None] & col_ok[None, None, None, None]
    t = jnp.where(ok, t, MASK_VALUE)
    t = t.transpose(1, 0, 2, 4, 3, 5)
    n_pat = len(patterns)
    t = t.reshape(n_pat, N_PAIRS, 2 * NA_ROW_GROUP * GRID_W, key_rows * GRID_W)
    return t.astype(F32), jnp.asarray(ids, dtype=jnp.int32)


def _ctx_na_kernel(q_ref, k_ref, v_ref, o_ref):
    tq = q_ref.shape[1]
    lane = lax.broadcasted_iota(jnp.int32, (tq, LANES), 1)
    for pr in range(N_PAIRS):
        cols = slice(pr * LANES, (pr + 1) * LANES)
        q2 = q_ref[0, :, cols]
        kb = k_ref[0, :, cols]
        vb = v_ref[0, :, cols]
        acc = jnp.zeros((tq, LANES), F32)
        for hh in range(2):
            in_head = (lane >= hh * NA_HEAD_DIM) & (lane < (hh + 1) * NA_HEAD_DIM)
            qm = jnp.where(in_head, q2, jnp.zeros_like(q2))
            s = _dot_nt(qm, kb)
            p = jnp.exp(s - jnp.max(s, axis=-1, keepdims=True))
            l = jnp.sum(p, axis=-1, keepdims=True)
            acc = jnp.where(in_head, _dot(p.astype(BF16), vb) / l, acc)
        o_ref[0, :, cols] = acc.astype(BF16)


def _ctx_na_attention(q, k, v):
    b, n, _ = q.shape
    spec = pl.BlockSpec((1, n, HEAD_W), lambda bi: (bi, 0, 0))
    return pl.pallas_call(
        _ctx_na_kernel,
        grid=(b,),
        in_specs=[spec, spec, spec],
        out_specs=spec,
        out_shape=jax.ShapeDtypeStruct((b, n, HEAD_W), BF16),
        compiler_params=_cparams(("parallel",)),
        name="ctx_na_attention",
    )(q, k, v)


def _diff_kernel(q_ref, lam_ref, g_ref, *refs, lam_init, n_src):
    src_refs, o_ref, k_ref, v_ref = refs[:2 * n_src], refs[2 * n_src], refs[-2], refs[-1]

    @pl.when(pl.program_id(1) == 0)
    def _():
        off = 0
        for j in range(n_src):
            n = src_refs[2 * j].shape[1]
            k_ref[off:off + n] = src_refs[2 * j][0]
            v_ref[off:off + n] = src_refs[2 * j + 1][0]
            off += n

    tq = q_ref.shape[1]
    lp = lam_ref[...]
    lam = (jnp.exp(jnp.sum(lp[0:1] * lp[1:2], axis=-1, keepdims=True))
           - jnp.exp(jnp.sum(lp[2:3] * lp[3:4], axis=-1, keepdims=True)) + lam_init)
    lane = lax.broadcasted_iota(jnp.int32, (tq, LANES), 1)
    for pr in range(N_PAIRS):
        cols = slice(pr * LANES, (pr + 1) * LANES)
        q2 = q_ref[0, :, cols]
        zero = jnp.zeros_like(q2)
        outp = jnp.zeros((tq, LANES), F32)
        for hh in range(2):
            comp = []
            for c in range(2):
                lo = hh * DF_V_DIM + c * DF_QK_DIM
                qm = jnp.where((lane >= lo) & (lane < lo + DF_QK_DIM), q2, zero)
                s = _dot_nt(qm, k_ref[:, cols])
                p = jnp.exp2(s - jnp.max(s, axis=-1, keepdims=True)).astype(BF16)
                res = _dot(p, v_ref[:, 2 * pr * LANES:2 * (pr + 1) * LANES])
                comp.append(res[:, :LANES] / res[:, LANES:LANES + 1])
            in_head = (lane >= hh * DF_V_DIM) & (lane < (hh + 1) * DF_V_DIM)
            oh = jnp.where(in_head, comp[0] - lam * comp[1], 0.0)
            ms = jnp.sum(oh * oh, axis=-1, keepdims=True) * (1.0 / DF_V_DIM)
            outp = outp + oh * lax.rsqrt(ms + EPS) * g_ref[...] * (1.0 - lam_init)
        o_ref[0, :, cols] = outp.astype(BF16)


def _diff_attention(dq, kvs, lam_p, subln_g, lam_init):
    b, nq, _ = dq.shape
    tq = min(256, nq)
    nk = sum(k.shape[1] for k, _ in kvs)
    kv_specs, kv_args = [], []
    for k, v in kvs:
        kv_specs += [pl.BlockSpec((1, k.shape[1], HEAD_W), lambda bi, i: (bi, 0, 0)),
                     pl.BlockSpec((1, k.shape[1], 2 * HEAD_W), lambda bi, i: (bi, 0, 0))]
        kv_args += [k, v]
    return pl.pallas_call(
        functools.partial(_diff_kernel, lam_init=lam_init, n_src=len(kvs)),
        grid=(b, nq // tq),
        in_specs=[
            pl.BlockSpec((1, tq, HEAD_W), lambda bi, i: (bi, i, 0)),
            pl.BlockSpec(lam_p.shape, lambda bi, i: (0, 0)),
            pl.BlockSpec(subln_g.shape, lambda bi, i: (0, 0)),
        ] + kv_specs,
        out_specs=pl.BlockSpec((1, tq, HEAD_W), lambda bi, i: (bi, i, 0)),
        out_shape=jax.ShapeDtypeStruct((b, nq, HEAD_W), BF16),
        scratch_shapes=[pltpu.VMEM((nk, HEAD_W), BF16), pltpu.VMEM((nk, 2 * HEAD_W), BF16)],
        compiler_params=_cparams(("parallel", "arbitrary")),
        name="diff_attention",
    )(dq, lam_p, subln_g, *kv_args)


def _merge_kernel(x_ref, f_ref, na_ref, df_ref, gate_ref, g1_ref, wft_ref, wna_ref, wdf_ref, wout_ref,
                  ng_ref, sh_ref, sc_ref, wrh_ref, wrl_ref, xo_ref, h2_ref, lg_ref):
    d = x_ref.shape[2]
    y_ft = _dot(f_ref[0], wft_ref[...])
    y_na = _dot(na_ref[0], wna_ref[...])
    y_df = _dot(df_ref[0], wdf_ref[...])
    m = (gate_ref[0, :, 0:d].astype(F32) * y_ft + gate_ref[0, :, d:2 * d].astype(F32) * y_na
         + gate_ref[0, :, 2 * d:3 * d].astype(F32) * y_df)
    y = _dot(m.astype(BF16), wout_ref[...])
    xn = x_ref[0] + g1_ref[0] * y
    xo_ref[0] = xn
    ms = jnp.mean(xn * xn, axis=-1, keepdims=True)
    h2 = (xn * lax.rsqrt(ms + EPS) * ng_ref[...]) * (1.0 + sc_ref[0]) + sh_ref[0]
    h_hi, h_lo = _split_bf16(h2)
    h2_ref[0] = h_hi
    lg_ref[0] = _dot(h_hi, wrh_ref[...]) + _dot(h_lo, wrh_ref[...]) + _dot(h_hi, wrl_ref[...])


def _merge(x, f, o_na, o_df, gates, g1, w_ft, w_na_o, w_df_o, w_out, norm_g, sh2, sc2, wr_hi, wr_lo):
    b, n, d = x.shape
    tm = min(512, n)
    tok = lambda w: pl.BlockSpec((1, tm, w), lambda bi, i: (bi, i, 0))
    full = lambda a: pl.BlockSpec(a.shape, lambda bi, i: (0,) * a.ndim)
    mod = pl.BlockSpec((1, 1, d), lambda bi, i: (bi, 0, 0))
    return pl.pallas_call(
        _merge_kernel,
        grid=(b, n // tm),
        in_specs=[tok(d), tok(FT_WIDTH), tok(HEAD_W), tok(HEAD_W), tok(N_BRANCHES * d), mod,
                  full(w_ft), full(w_na_o), full(w_df_o), full(w_out), full(norm_g), mod, mod,
                  full(wr_hi), full(wr_lo)],
        out_specs=[tok(d), tok(d), tok(ROUTER_PAD)],
        out_shape=[jax.ShapeDtypeStruct((b, n, d), F32), jax.ShapeDtypeStruct((b, n, d), BF16),
                   jax.ShapeDtypeStruct((b, n, ROUTER_PAD), F32)],
        compiler_params=_cparams(("parallel", "parallel")),
        name="merge",
    )(x, f, o_na, o_df, gates, g1, w_ft, w_na_o, w_df_o, w_out, norm_g, sh2, sc2, wr_hi, wr_lo)


ROUTE_TILE = LANES
SLOT_ALIGN = 16
ONE_BITS = 0x3F800000


FAST_WINDOW = 48


def _slot_window(cap):
    return min(ROUTE_TILE + SLOT_ALIGN, cap)


def _fast_window(cap):
    return min(FAST_WINDOW, cap)


def _slot_windows(lo, cap):
    b, nt, _ = lo.shape
    filled = jnp.concatenate([lo[:, 1:], jnp.full((b, 1, N_EXPERTS), cap, lo.dtype)], axis=1)
    aligned = (lo // SLOT_ALIGN) * SLOT_ALIGN
    lo_slow = jnp.minimum(aligned, cap - _slot_window(cap))
    lo_fast = jnp.minimum(aligned, cap - _fast_window(cap))
    fast_ok = jnp.all(filled - lo_fast <= _fast_window(cap), axis=-1)
    return {"lo_slow": lo_slow.reshape(-1), "lo_fast": lo_fast.reshape(-1),
            "fast_ok": fast_ok.astype(jnp.int32).reshape(-1)}


def _route_kernel(lg_ref, pos_ref, aff_ref, lo_ref, bits_ref, sel_ref, packed_ref, *, cap):
    n = lg_ref.shape[1]
    nb = n // ROUTE_TILE
    lane = lax.broadcasted_iota(jnp.int32, (n, LANES), 1)
    z = jnp.where(lane < N_EXPERTS, lg_ref[0], MASK_VALUE)
    p = jnp.exp(z - jnp.max(z, axis=-1, keepdims=True))
    aff = p / jnp.sum(p, axis=-1, keepdims=True)
    aff_ref[0] = aff
    bits_ref[...] = pltpu.bitcast(aff, jnp.int32)

    groups = LANES // N_EXPERTS
    rows_p = n // groups
    packed = bits_ref[0:rows_p]
    for j in range(1, groups):
        packed = packed + pltpu.roll(bits_ref[j * rows_p:(j + 1) * rows_p], N_EXPERTS * j, 1)
    packed_ref[...] = packed

    def count(mask):
        part = jnp.broadcast_to(jnp.sum(jnp.where(mask, 1.0, 0.0), axis=0, keepdims=True), (8, LANES))
        tot = part
        for j in range(1, groups):
            tot = tot + pltpu.roll(part, N_EXPERTS * j, 1)
        return tot[0:1]

    def bisect(_, carry):
        lo, hi = carry
        mid = (lo + hi) >> 1
        ge = count(packed_ref[...] >= mid) >= cap
        return jnp.where(ge, mid, lo), jnp.where(ge, hi, mid)

    lo0 = jnp.zeros((1, LANES), jnp.int32)
    hi0 = jnp.full((1, LANES), ONE_BITS + 1, jnp.int32)
    thr, _ = lax.fori_loop(0, 31, bisect, (lo0, hi0))
    need = cap - count(packed_ref[...] > thr)

    row = lax.broadcasted_iota(jnp.int32, (ROUTE_TILE, ROUTE_TILE), 0)
    col = lax.broadcasted_iota(jnp.int32, (ROUTE_TILE, ROUTE_TILE), 1)
    tri = jnp.where(row >= col, 1.0, 0.0).astype(BF16)

    carry = jnp.zeros((1, LANES), F32)
    for blk in range(nb):
        rows = slice(blk * ROUTE_TILE, (blk + 1) * ROUTE_TILE)
        bb = bits_ref[rows]
        eq = jnp.where(bb == thr, 1.0, 0.0)
        incl = _dot(tri, eq.astype(BF16))
        before = incl - eq + carry
        take = jnp.where(before < need, eq, 0.0)
        sel_ref[rows] = jnp.where(bb > thr, 1.0, take)
        carry = carry + incl[ROUTE_TILE - 1:ROUTE_TILE]

    carry = jnp.zeros((1, LANES), F32)
    for blk in range(nb):
        rows = slice(blk * ROUTE_TILE, (blk + 1) * ROUTE_TILE)
        sel = sel_ref[rows]
        incl = _dot(tri, sel.astype(BF16))
        pos_ref[0, rows] = jnp.where(sel > 0.0, carry + incl - sel, -1.0)
        lo_ref[0, blk:blk + 1] = carry.astype(jnp.int32)
        carry = carry + incl[ROUTE_TILE - 1:ROUTE_TILE]


def _route(logits, cap):
    b, n, _ = logits.shape
    nt = n // ROUTE_TILE
    tok = pl.BlockSpec((1, n, LANES), lambda bi: (bi, 0, 0))
    return pl.pallas_call(
        functools.partial(_route_kernel, cap=cap),
        grid=(b,),
        in_specs=[tok],
        out_specs=[tok, tok, pl.BlockSpec((1, nt, LANES), lambda bi: (bi, 0, 0))],
        out_shape=[jax.ShapeDtypeStruct((b, n, LANES), F32), jax.ShapeDtypeStruct((b, n, LANES), F32),
                   jax.ShapeDtypeStruct((b, nt, LANES), jnp.int32)],
        scratch_shapes=[pltpu.VMEM((n, LANES), jnp.int32), pltpu.VMEM((n, LANES), F32),
                        pltpu.VMEM((n // (LANES // N_EXPERTS), LANES), jnp.int32)],
        compiler_params=_cparams(("parallel",)),
        name="route",
    )(logits)


DISPATCH_COLS = 256


def _dispatch_kernel(lo_fast_ref, lo_slow_ref, fast_ref, h_ref, pos_ref, o_ref, *, win_fast, win_slow):
    b = pl.program_id(0)
    step = pl.program_id(1)
    per_step = pos_ref.shape[1]
    nt = pl.num_programs(1) * per_step

    @pl.when(step == 0)
    def _():
        o_ref[...] = jnp.zeros_like(o_ref)

    def run(sub, t, win, lo_ref):
        slot = lax.broadcasted_iota(jnp.int32, (win, ROUTE_TILE), 0).astype(F32)
        pos_t = pos_ref[0, sub]
        los = [pl.multiple_of(lo_ref[(b * nt + t) * N_EXPERTS + e], SLOT_ALIGN) for e in range(N_EXPERTS)]
        onehot = jnp.concatenate(
            [jnp.where(pos_t[e:e + 1, :] - los[e].astype(F32) == slot, 1.0, 0.0).astype(BF16)
             for e in range(N_EXPERTS)], axis=0)
        tok_rows = slice(sub * ROUTE_TILE, (sub + 1) * ROUTE_TILE)
        for c0 in range(0, h_ref.shape[2], DISPATCH_COLS):
            cols = slice(c0, c0 + DISPATCH_COLS)
            res = _dot(onehot, h_ref[0, tok_rows, cols]).astype(BF16)
            for e in range(N_EXPERTS):
                rows = pl.ds(los[e], win)
                o_ref[0, e, rows, cols] = o_ref[0, e, rows, cols] + res[e * win:(e + 1) * win]

    for sub in range(per_step):
        t = step * per_step + sub
        if win_fast == win_slow:
            run(sub, t, win_slow, lo_slow_ref)
        else:
            fast = fast_ref[b * nt + t] == 1
            pl.when(fast)(functools.partial(run, sub, t, win_fast, lo_fast_ref))
            pl.when(jnp.logical_not(fast))(functools.partial(run, sub, t, win_slow, lo_slow_ref))


def _tiles_per_step(nt):
    return 2 if nt % 2 == 0 else 1


def _dispatch(windows, h2, pos_t, cap):
    b, n, d = h2.shape
    nt = n // ROUTE_TILE
    per_step = _tiles_per_step(nt)
    grid_spec = pltpu.PrefetchScalarGridSpec(
        num_scalar_prefetch=3,
        grid=(b, nt // per_step),
        in_specs=[
            pl.BlockSpec((1, per_step * ROUTE_TILE, d), lambda bi, t, *_: (bi, t, 0)),
            pl.BlockSpec((1, per_step, N_EXPERTS, ROUTE_TILE), lambda bi, t, *_: (bi, t, 0, 0)),
        ],
        out_specs=pl.BlockSpec((1, N_EXPERTS, cap, d), lambda bi, t, *_: (bi, 0, 0, 0)),
    )
    return pl.pallas_call(
        functools.partial(_dispatch_kernel, win_fast=_fast_window(cap), win_slow=_slot_window(cap)),
        grid_spec=grid_spec,
        out_shape=jax.ShapeDtypeStruct((b, N_EXPERTS, cap, d), BF16),
        compiler_params=_cparams(("arbitrary", "arbitrary")),
        name="dispatch",
    )(windows["lo_fast"], windows["lo_slow"], windows["fast_ok"], h2, pos_t)


def _expert_kernel(*refs, f_chunk, n_sets):
    x_refs, (wg_ref, wu_ref, wd_ref), o_refs = refs[:n_sets], refs[n_sets:n_sets + 3], refs[n_sets + 3:]
    xs = [r[0, 0] for r in x_refs]
    x = xs[0] if n_sets == 1 else jnp.concatenate(xs, axis=0)
    ff = wg_ref.shape[3]
    acc = jnp.zeros(x.shape, F32)
    for f0 in range(0, ff, f_chunk):
        a = _dot(x, wg_ref[0, 0, :, f0:f0 + f_chunk])
        u = _dot(x, wu_ref[0, 0, :, f0:f0 + f_chunk])
        hm = (a * jax.nn.sigmoid(a) * u).astype(BF16)
        acc = acc + _dot(hm, wd_ref[0, 0, f0:f0 + f_chunk, :])
    off = 0
    for xr, o_ref in zip(xs, o_refs):
        o_ref[0, 0] = acc[off:off + xr.shape[0]].astype(BF16)
        off += xr.shape[0]


def _experts(xes, w_gate, w_up, w_down, layer):
    b, e, _, d = xes[0].shape
    ff = w_gate.shape[3]
    toks = [pl.BlockSpec((1, 1, xe.shape[2], d), lambda ei, bi: (bi, ei, 0, 0)) for xe in xes]
    return pl.pallas_call(
        functools.partial(_expert_kernel, f_chunk=min(512, ff), n_sets=len(xes)),
        grid=(e, b),
        in_specs=toks + [
            pl.BlockSpec((1, 1, d, ff), lambda ei, bi: (layer, ei, 0, 0)),
            pl.BlockSpec((1, 1, d, ff), lambda ei, bi: (layer, ei, 0, 0)),
            pl.BlockSpec((1, 1, ff, d), lambda ei, bi: (layer, ei, 0, 0)),
        ],
        out_specs=toks,
        out_shape=[jax.ShapeDtypeStruct(xe.shape, BF16) for xe in xes],
        compiler_params=_cparams(("arbitrary", "arbitrary")),
        name="experts",
    )(*xes, w_gate, w_up, w_down)


def _combine_kernel(lo_fast_ref, lo_slow_ref, fast_ref, x_ref, y_ref, pos_ref, aff_ref, g_ref, spread_ref, slot_ref,
                    o_ref, *, win_fast, win_slow):
    b = pl.program_id(0)
    step = pl.program_id(1)
    tile = ROUTE_TILE
    per_step = x_ref.shape[1] // tile
    nt = pl.num_programs(1) * per_step

    def wide(rows, base):
        slot = lax.broadcasted_iota(jnp.int32, (tile, win_slow), 1).astype(F32)
        pos = pos_ref[0, rows]
        aff = aff_ref[0, rows]
        acc = jnp.zeros((tile, x_ref.shape[2]), F32)
        for e in range(N_EXPERTS):
            lo = pl.multiple_of(lo_slow_ref[base + e], SLOT_ALIGN)
            pick = jnp.where(pos[:, e:e + 1] - lo.astype(F32) == slot, aff[:, e:e + 1], 0.0).astype(BF16)
            acc = acc + _dot(pick, y_ref[0, e, pl.ds(lo, win_slow), :])
        o_ref[0, rows] = x_ref[0, rows] + g_ref[0] * acc

    def narrow(rows, base):
        lane = lax.broadcasted_iota(jnp.int32, (1, LANES), 1)
        lo_vec = jnp.zeros((1, LANES), F32)
        los = []
        for e in range(N_EXPERTS):
            lo = pl.multiple_of(lo_fast_ref[base + e], SLOT_ALIGN)
            los.append(lo)
            lo_vec = jnp.where(lane == e, lo.astype(F32), lo_vec)
        pos = pos_ref[0, rows]
        rel = jnp.where(pos >= 0.0, pos - lo_vec, -1.0).astype(BF16)
        rel_wide = _dot(rel, spread_ref[...])
        gate_wide = _dot(aff_ref[0, rows].astype(BF16), spread_ref[...])
        pick = jnp.where(rel_wide == slot_ref[...], gate_wide, 0.0).astype(BF16)
        y_cat = jnp.concatenate([y_ref[0, e, pl.ds(los[e], win_fast), :] for e in range(N_EXPERTS)], axis=0)
        o_ref[0, rows] = x_ref[0, rows] + g_ref[0] * _dot(pick, y_cat)

    for sub in range(per_step):
        t = step * per_step + sub
        rows = slice(sub * tile, (sub + 1) * tile)
        base = (b * nt + t) * N_EXPERTS
        if win_fast == win_slow:
            wide(rows, base)
        else:
            fast = fast_ref[b * nt + t] == 1
            pl.when(fast)(functools.partial(narrow, rows, base))
            pl.when(jnp.logical_not(fast))(functools.partial(wide, rows, base))


def _combine(windows, x, ye, pos, aff, g2, cap):
    b, n, d = x.shape
    e = ye.shape[1]
    nt = n // ROUTE_TILE
    per_step = _tiles_per_step(nt)
    tok = lambda w: pl.BlockSpec((1, per_step * ROUTE_TILE, w), lambda bi, t, *_: (bi, t, 0))
    win_fast = _fast_window(cap)
    j = jnp.arange(N_EXPERTS * win_fast)
    spread = (jnp.arange(LANES)[:, None] == (j // win_fast)[None, :]).astype(BF16)
    slot = (j % win_fast).astype(F32)[None, :]
    grid_spec = pltpu.PrefetchScalarGridSpec(
        num_scalar_prefetch=3,
        grid=(b, nt // per_step),
        in_specs=[
            tok(d),
            pl.BlockSpec((1, e, cap, d), lambda bi, t, *_: (bi, 0, 0, 0)),
            tok(LANES),
            tok(LANES),
            pl.BlockSpec((1, 1, d), lambda bi, t, *_: (bi, 0, 0)),
            pl.BlockSpec(spread.shape, lambda bi, t, *_: (0, 0)),
            pl.BlockSpec(slot.shape, lambda bi, t, *_: (0, 0)),
        ],
        out_specs=tok(d),
    )
    return pl.pallas_call(
        functools.partial(_combine_kernel, win_fast=win_fast, win_slow=_slot_window(cap)),
        grid_spec=grid_spec,
        out_shape=jax.ShapeDtypeStruct((b, n, d), F32),
        compiler_params=_cparams(("arbitrary", "arbitrary")),
        name="combine",
    )(windows["lo_fast"], windows["lo_slow"], windows["fast_ok"], x, ye, pos, aff, g2, spread, slot)


def _route_and_dispatch(h2, logits):
    b, n, _ = h2.shape
    cap = EC_CAPACITY_FACTOR * n // N_EXPERTS
    nt = n // ROUTE_TILE
    assert n % ROUTE_TILE == 0 and cap % SLOT_ALIGN == 0
    pos, aff, lo = _route(logits, cap)
    windows = _slot_windows(lo[:, :, :N_EXPERTS], cap)
    pos_t = pos[:, :, :N_EXPERTS].reshape(b, nt, ROUTE_TILE, N_EXPERTS).transpose(0, 1, 3, 2)
    xe = _dispatch(windows, h2, pos_t, cap)
    return xe, {"windows": windows, "pos": pos, "aff": aff, "cap": cap}


def _combine_residual(x, ye, routing, g2):
    return _combine(routing["windows"], x, ye, routing["pos"], routing["aff"], g2, routing["cap"])


def _dft_mats(n):
    k = jnp.arange(n, dtype=jnp.int32)
    ang = ((k[:, None] * k[None, :]) % n).astype(F32) * (2.0 * math.pi / n)
    return jnp.cos(ang), jnp.sin(ang)


def _dft_mats_bf16(n, block=64):
    assert n % block == 0
    k = jnp.arange(n // 2, dtype=jnp.int32)
    a = jnp.arange(n // block, dtype=jnp.int32) * block
    b = jnp.arange(block, dtype=jnp.int32)
    ang_a = ((a[:, None] * k[None, :]) % n).astype(F32) * (2.0 * math.pi / n)
    ang_b = ((b[:, None] * k[None, :]) % n).astype(F32) * (2.0 * math.pi / n)
    ca, sa = jnp.cos(ang_a)[:, None, :], jnp.sin(ang_a)[:, None, :]
    cb, sb = jnp.cos(ang_b)[None], jnp.sin(ang_b)[None]
    cos_m = (ca * cb - sa * sb).reshape(n, n // 2)
    nsin_m = (-(sa * cb + ca * sb)).reshape(n, n // 2)
    return cos_m.astype(BF16), nsin_m.astype(BF16)


def _chan_dft():
    c, s = _dft_mats(FT_GROUP_DIM)
    eye = jnp.eye(FT_GROUPS, dtype=F32)
    return jnp.concatenate([jnp.kron(eye, c), jnp.kron(eye, s)], axis=1).astype(BF16)


def _group_mean_mat(group):
    gid = jnp.arange(HEAD_W) // group
    return jnp.where(gid[:, None] == gid[None, :], 1.0 / group, 0.0).astype(BF16)


def _rope_tables(n):
    t = jnp.arange(n)
    row = (t // GRID_W).astype(F32)
    col = (t % GRID_W).astype(F32)
    ax = DF_QK_DIM // 2
    inv = ROPE_BASE ** (-jnp.arange(0, ax, 2, dtype=F32) / ax)
    lane = jnp.arange(LANES)
    freq = inv[lane % (ax // 2)]
    pos = jnp.where(((lane % DF_QK_DIM) < ax)[None, :], row[:, None], col[:, None])
    ang = pos * freq[None, :]
    sign = jnp.where((lane % ax) < ax // 2, -1.0, 1.0)
    return jnp.cos(ang), jnp.sin(ang) * sign[None, :]


def kernel(x, c, ctx, c_ctx, norm1_g, norm2_g, w_ada, b_ada, w_in, na_qn_g, na_kn_g, na_rpb, df_qn_g, df_kn_g,
           df_lambda, df_subln_g, w_ft, w_na_o, w_df_o, w_out, w_router, w_gate, w_up, w_down):
    b, n, d = x.shape
    lc = ctx.shape[1]
    depth = w_ada.shape[0]
    assert b + 1 <= MOD_ROWS and n % GRID_W == 0

    cc = jnp.zeros((MOD_ROWS, d), F32).at[:b].set(c).at[b].set(c_ctx)
    mods = _ada(cc, w_ada, b_ada)

    consts = {"chan_dft": _chan_dft(), "g64": _group_mean_mat(NA_HEAD_DIM), "g32": _group_mean_mat(DF_QK_DIM)}
    rope_lat = _rope_tables(n)
    rope_ctx = (jnp.zeros((lc, LANES), F32), jnp.zeros((lc, LANES), F32))
    dft_lat = _dft_mats_bf16(n)
    dft_ctx = _dft_mats_bf16(lc)
    rows = n // GRID_W

    w_gate_b, w_up_b, w_down_b = w_gate.astype(BF16), w_up.astype(BF16), w_down.astype(BF16)

    xc = ctx
    for i in range(depth):
        last = i == depth - 1
        lam_init = 0.8 - 0.6 * math.exp(-0.3 * i)
        m_lat = mods[i, :b].reshape(b, 6, 1, d)
        m_ctx = jnp.broadcast_to(mods[i, b].reshape(1, 6, 1, d), (b, 6, 1, d))
        sh1, sc1, g1, sh2, sc2, g2 = [m_lat[:, j] for j in range(6)]
        csh1, csc1, cg1, csh2, csc2, cg2 = [m_ctx[:, j] for j in range(6)]

        w_in_b = w_in[i].astype(BF16)
        gains = {
            "naq": jnp.tile(na_qn_g[i], NA_HEADS).reshape(1, HEAD_W),
            "nak": jnp.tile(na_kn_g[i], NA_HEADS).reshape(1, HEAD_W),
            "dfq": jnp.tile(df_qn_g[i], 2 * DF_HEADS).reshape(1, HEAD_W),
            "dfk": jnp.tile(df_kn_g[i], 2 * DF_HEADS).reshape(1, HEAD_W),
        }
        n1g = norm1_g[i].reshape(1, d)
        n2g = norm2_g[i].reshape(1, d)
        subln = jnp.tile(df_subln_g[i], 2).reshape(1, LANES)
        w_ft_b, w_na_b, w_df_b, w_out_b = (w.astype(BF16) for w in (w_ft[i], w_na_o[i], w_df_o[i], w_out[i]))
        wr = jnp.zeros((d, ROUTER_PAD), F32).at[:, :N_EXPERTS].set(w_router[i])
        wr_hi, wr_lo = _split_bf16(wr)

        if last:
            nkc, nvc, dkc, dvc = _inproj(xc, csh1, csc1, n1g, w_in_b, consts, gains, rope_ctx,
                                         rope=False, kv_only=True)
        else:
            fabc, nqc, dqc, nkc, nvc, dkc, dvc, gatec = _inproj(xc, csh1, csc1, n1g, w_in_b, consts, gains,
                                                                 rope_ctx, rope=False, kv_only=False)

        fab, nq, dq, nk, nv, dk, dv, gate = _inproj(x, sh1, sc1, n1g, w_in_b, consts, gains, rope_lat,
                                                     rope=True, kv_only=False)
        f = _fourier(fab, *dft_lat)
        o_na = _na_attention(nq, nk, nv, nkc, nvc, *_na_bias_table(na_rpb[i], rows))
        o_df = _diff_attention(dq, [(dk, dv), (dkc, dvc)], df_lambda[i], subln, lam_init)
        x, h2, logits = _merge(x, f, o_na, o_df, gate, g1, w_ft_b, w_na_b, w_df_b, w_out_b, n2g, sh2, sc2,
                               wr_hi, wr_lo)
        xe, routing = _route_and_dispatch(h2, logits)

        if last:
            (ye,) = _experts([xe], w_gate_b, w_up_b, w_down_b, i)
        else:
            fc = _fourier(fabc, *dft_ctx)
            o_nac = _ctx_na_attention(nqc, nkc, nvc)
            o_dfc = _diff_attention(dqc, [(dkc, dvc)], df_lambda[i], subln, lam_init)
            xc, hc2, logits_c = _merge(xc, fc, o_nac, o_dfc, gatec, cg1, w_ft_b, w_na_b, w_df_b, w_out_b, n2g,
                                       csh2, csc2, wr_hi, wr_lo)
            xec, routing_c = _route_and_dispatch(hc2, logits_c)
            ye, yec = _experts([xe, xec], w_gate_b, w_up_b, w_down_b, i)
            xc = _combine_residual(xc, yec, routing_c, cg2)
        x = _combine_residual(x, ye, routing, g2)
    return x
```

```python
import functools
import math

import jax
import jax.numpy as jnp
from jax import lax
from jax.experimental import pallas as pl
from jax.experimental.pallas import tpu as pltpu

F32 = jnp.float32
BF16 = jnp.bfloat16

GRID_W = 64
FT_GROUPS = 4
FT_GROUP_DIM = 64
FT_WIDTH = FT_GROUPS * FT_GROUP_DIM
NA_HEADS = 6
NA_HEAD_DIM = 64
NA_WIDTH = NA_HEADS * NA_HEAD_DIM
NA_WIN_H = 8
NA_WIN_W = 16
DF_HEADS = 6
DF_QK_DIM = 32
DF_V_DIM = 2 * DF_QK_DIM
DF_QK_WIDTH = DF_HEADS * 2 * DF_QK_DIM
DF_WIDTH = DF_HEADS * DF_V_DIM
N_BRANCHES = 3
N_EXPERTS = 16
EC_CAPACITY_FACTOR = 2
ROPE_BASE = 10000.0
EPS = 1e-6
MASK_VALUE = -1e30

LANES = 128
VMEM_LIMIT_BYTES = 56 * 1024 * 1024

HEAD_W = 384
N_PAIRS = HEAD_W // LANES
MOD_ROWS = 16
ROUTER_PAD = LANES


def _cparams(sem):
    return pltpu.CompilerParams(dimension_semantics=sem, vmem_limit_bytes=VMEM_LIMIT_BYTES)


def _dot(a, b):
    return jnp.dot(a, b, preferred_element_type=F32)


def _dot_nt(a, b):
    return lax.dot_general(a, b, (((1,), (1,)), ((), ())), preferred_element_type=F32)


def _split_bf16(v):
    hi = v.astype(BF16)
    lo = (v - hi.astype(F32)).astype(BF16)
    return hi, lo


def _ada_kernel(c_ref, w_ref, b_ref, o_ref):
    c = c_ref[...]
    a = c * jax.nn.sigmoid(c)
    a_hi, a_lo = _split_bf16(a)
    w_hi, w_lo = _split_bf16(w_ref[0])
    acc = _dot(a_hi, w_hi) + _dot(a_lo, w_hi) + _dot(a_hi, w_lo)
    o_ref[0] = acc + b_ref[0]


def _ada(cc, w_ada, b_ada):
    depth, d, d6 = w_ada.shape
    tn = 512
    return pl.pallas_call(
        _ada_kernel,
        grid=(depth, d6 // tn),
        in_specs=[
            pl.BlockSpec((MOD_ROWS, d), lambda l, j: (0, 0)),
            pl.BlockSpec((1, d, tn), lambda l, j: (l, 0, j)),
            pl.BlockSpec((1, 1, tn), lambda l, j: (l, 0, j)),
        ],
        out_specs=pl.BlockSpec((1, MOD_ROWS, tn), lambda l, j: (l, 0, j)),
        out_shape=jax.ShapeDtypeStruct((depth, MOD_ROWS, d6), F32),
        compiler_params=_cparams(("arbitrary", "arbitrary")),
        name="ada",
    )(cc, w_ada, b_ada.reshape(depth, 1, d6))


OFF_Q = FT_WIDTH
OFF_DQ = OFF_Q + NA_WIDTH
OFF_KV = OFF_DQ + DF_QK_WIDTH
OFF_NV = OFF_KV + NA_WIDTH
OFF_DK = OFF_NV + NA_WIDTH
OFF_DV = OFF_DK + DF_QK_WIDTH
OFF_GATE = OFF_DV + DF_WIDTH


def _group_rms(v, gmat_ref, gain_ref):
    ms = _dot((v * v).astype(BF16), gmat_ref[...])
    return v * lax.rsqrt(ms + EPS) * gain_ref[...]


def _rope_chunk(vj, cos, sin_signed, first_half):
    fwd = pltpu.roll(vj, LANES - 8, 1)
    bwd = pltpu.roll(vj, 8, 1)
    partner = jnp.where(first_half, fwd, bwd)
    return vj * cos + partner * sin_signed


def _inproj_kernel(x_ref, sh_ref, sc_ref, ng_ref, w_ref, cd_ref, g64_ref, g32_ref,
                   naq_g_ref, nak_g_ref, dfq_g_ref, dfk_g_ref, cos_ref, sin_ref,
                   *out_refs, rope, kv_only):
    x = x_ref[0]
    ms = jnp.mean(x * x, axis=-1, keepdims=True)
    y = x * lax.rsqrt(ms + EPS) * ng_ref[...]
    h = (y * (1.0 + sc_ref[0]) + sh_ref[0]).astype(BF16)

    def proj(c0, c1):
        return _dot(h, w_ref[:, c0:c1])

    if rope:
        lane = lax.broadcasted_iota(jnp.int32, (x.shape[0], LANES), 1)
        first_half = (lane & 8) == 0
        cos = cos_ref[...]
        sin_signed = sin_ref[...]

    def df_qk(v, gain_ref, scale, o_ref):
        v = _group_rms(v, g32_ref, gain_ref)
        for j in range(N_PAIRS):
            vj = v[:, j * LANES:(j + 1) * LANES]
            if rope:
                vj = _rope_chunk(vj, cos, sin_signed, first_half)
            o_ref[0, :, j * LANES:(j + 1) * LANES] = (vj * scale).astype(BF16)

    if kv_only:
        nk_ref, nv_ref, dk_ref, dv_ref = out_refs
    else:
        fab_ref, nq_ref, dq_ref, nk_ref, nv_ref, dk_ref, dv_ref, gate_ref = out_refs
        u = proj(0, FT_WIDTH)
        fab_ref[0] = _dot(u.astype(BF16), cd_ref[...]).astype(BF16)
        nq = _group_rms(proj(OFF_Q, OFF_DQ), g64_ref, naq_g_ref)
        nq_ref[0] = (nq * (NA_HEAD_DIM ** -0.5)).astype(BF16)
        df_qk(proj(OFF_DQ, OFF_KV), dfq_g_ref, DF_QK_DIM ** -0.5 * math.log2(math.e), dq_ref)
        d = x.shape[1]
        for j in range(N_BRANCHES):
            z = proj(OFF_GATE + j * d, OFF_GATE + (j + 1) * d)
            gate_ref[0, :, j * d:(j + 1) * d] = jax.nn.sigmoid(z).astype(BF16)

    nk_ref[0] = _group_rms(proj(OFF_KV, OFF_NV), g64_ref, nak_g_ref).astype(BF16)
    nv_ref[0] = proj(OFF_NV, OFF_DK).astype(BF16)
    df_qk(proj(OFF_DK, OFF_DV), dfk_g_ref, 1.0, dk_ref)
    dv = proj(OFF_DV, OFF_GATE).astype(BF16)
    for j in range(N_PAIRS):
        dv_ref[0, :, 2 * j * LANES:(2 * j + 1) * LANES] = dv[:, j * LANES:(j + 1) * LANES]
        dv_ref[0, :, (2 * j + 1) * LANES:(2 * j + 2) * LANES] = jnp.ones((dv.shape[0], LANES), BF16)


def _inproj(x, sh, sc, norm_g, w_in, consts, gains, rope_tabs, *, rope, kv_only):
    b, n, d = x.shape
    tm = min(512, n)
    tok = lambda w: pl.BlockSpec((1, tm, w), lambda bi, i: (bi, i, 0))
    full = lambda a: pl.BlockSpec(a.shape, lambda bi, i: (0,) * a.ndim)
    mod = pl.BlockSpec((1, 1, d), lambda bi, i: (bi, 0, 0))
    cos_t, sin_t = rope_tabs
    tab = pl.BlockSpec((tm, LANES), lambda bi, i: (i, 0))
    slab = jax.ShapeDtypeStruct((b, n, HEAD_W), BF16)
    slab_aug = jax.ShapeDtypeStruct((b, n, 2 * HEAD_W), BF16)
    if kv_only:
        out_shape = [slab] * 3 + [slab_aug]
        out_specs = [tok(HEAD_W)] * 3 + [tok(2 * HEAD_W)]
    else:
        out_shape = [jax.ShapeDtypeStruct((b, n, 2 * FT_WIDTH), BF16)] + [slab] * 5 + [
            slab_aug, jax.ShapeDtypeStruct((b, n, N_BRANCHES * d), BF16)]
        out_specs = [tok(2 * FT_WIDTH)] + [tok(HEAD_W)] * 5 + [tok(2 * HEAD_W), tok(N_BRANCHES * d)]
    args = [x, sh, sc, norm_g, w_in, consts["chan_dft"], consts["g64"], consts["g32"],
            gains["naq"], gains["nak"], gains["dfq"], gains["dfk"], cos_t, sin_t]
    in_specs = [tok(d), mod, mod, full(norm_g), full(w_in), full(consts["chan_dft"]),
                full(consts["g64"]), full(consts["g32"]), full(gains["naq"]), full(gains["nak"]),
                full(gains["dfq"]), full(gains["dfk"]), tab, tab]
    return pl.pallas_call(
        functools.partial(_inproj_kernel, rope=rope, kv_only=kv_only),
        grid=(b, n // tm),
        in_specs=in_specs,
        out_specs=out_specs,
        out_shape=out_shape,
        compiler_params=_cparams(("parallel", "parallel")),
        name="inproj_kv" if kv_only else "inproj",
    )(*args)


def _fourier_kernel(c_ref, s_ref, head_ref, tail_ref, mid_ref, o_ref, acc_ref, *, scale):
    k = pl.program_id(1)

    @pl.when(k == 0)
    def _():
        acc_ref[...] = jnp.zeros_like(acc_ref)

    cm = c_ref[...]
    sm = s_ref[...]
    for bi in range(head_ref.shape[0]):
        head = head_ref[bi].astype(F32)
        tail = tail_ref[bi].astype(F32)
        af = (head[:, :FT_WIDTH] + tail[:, :FT_WIDTH]).astype(BF16)
        bf = (head[:, FT_WIDTH:] - tail[:, FT_WIDTH:]).astype(BF16)
        acc_ref[bi] += _dot(cm, af) + _dot(sm, bf)

    @pl.when(k == pl.num_programs(1) - 1)
    def _():
        row = lax.broadcasted_iota(jnp.int32, (acc_ref.shape[1], 1), 0)
        sign = jnp.where((row & 1) == 0, 1.0, -1.0)
        for bi in range(head_ref.shape[0]):
            o_ref[bi] = ((acc_ref[bi] + sign * mid_ref[bi].astype(F32)) * scale).astype(o_ref.dtype)


def _flip_kernel(x_ref, o_ref):
    t = x_ref.shape[1]
    row = lax.broadcasted_iota(jnp.int32, (t, t), 0)
    col = lax.broadcasted_iota(jnp.int32, (t, t), 1)
    anti = jnp.where(row + col == t - 1, 1.0, 0.0).astype(BF16)
    o_ref[0] = _dot(anti, x_ref[0]).astype(o_ref.dtype)


def _reverse_rows(x):
    b, m, w = x.shape
    t = min(512, m)
    nb = m // t
    return pl.pallas_call(
        _flip_kernel,
        grid=(b, nb),
        in_specs=[pl.BlockSpec((1, t, w), lambda bi, j: (bi, nb - 1 - j, 0))],
        out_specs=pl.BlockSpec((1, t, w), lambda bi, j: (bi, j, 0)),
        out_shape=jax.ShapeDtypeStruct(x.shape, x.dtype),
        compiler_params=_cparams(("parallel", "parallel")),
        name="reverse_rows",
    )(x)


def _fourier(fab, cos_m, nsin_m):
    b, n, _ = fab.shape
    half = n // 2
    tn = min(1024, n)
    tk = min(512, half)
    assert tn % 2 == 0
    scale = 1.0 / math.sqrt(n * FT_GROUP_DIM)
    head = fab[:, :half]
    tail = _reverse_rows(jnp.concatenate([fab[:, half + 1:], jnp.zeros_like(fab[:, :1])], axis=1))
    mid = fab[:, half:half + 1, :FT_WIDTH]
    return pl.pallas_call(
        functools.partial(_fourier_kernel, scale=scale),
        grid=(n // tn, half // tk),
        in_specs=[
            pl.BlockSpec((tn, tk), lambda i, k: (i, k)),
            pl.BlockSpec((tn, tk), lambda i, k: (i, k)),
            pl.BlockSpec((b, tk, 2 * FT_WIDTH), lambda i, k: (0, k, 0)),
            pl.BlockSpec((b, tk, 2 * FT_WIDTH), lambda i, k: (0, k, 0)),
            pl.BlockSpec((b, 1, FT_WIDTH), lambda i, k: (0, 0, 0)),
        ],
        out_specs=pl.BlockSpec((b, tn, FT_WIDTH), lambda i, k: (0, i, 0)),
        out_shape=jax.ShapeDtypeStruct((b, n, FT_WIDTH), BF16),
        scratch_shapes=[pltpu.VMEM((b, tn, FT_WIDTH), F32)],
        compiler_params=_cparams(("parallel", "arbitrary")),
        name="fourier",
    )(cos_m, nsin_m, head, tail, mid)


NA_ROW_GROUP = 4


def _na_kernel(pid_ref, q_ref, k_ref, v_ref, kc_ref, vc_ref, bias_ref, o_ref, *, rows, kh):
    del pid_ref
    g = pl.program_id(1)
    rq = NA_ROW_GROUP * GRID_W
    key_rows = NA_ROW_GROUP + kh - 1
    u = jnp.clip(NA_ROW_GROUP * g - kh // 2, 0, rows - key_rows)
    start = pl.multiple_of(u * GRID_W, GRID_W)
    lane = lax.broadcasted_iota(jnp.int32, (rq, LANES), 1)
    first = lane < NA_HEAD_DIM
    for pr in range(N_PAIRS):
        cols = slice(pr * LANES, (pr + 1) * LANES)
        q2 = q_ref[0, :, cols]
        zero = jnp.zeros_like(q2)
        qcat = jnp.concatenate([jnp.where(first, q2, zero), jnp.where(first, zero, q2)], axis=0)
        kb = k_ref[0, pl.ds(start, key_rows * GRID_W), cols]
        vb = v_ref[0, pl.ds(start, key_rows * GRID_W), cols]
        s1 = _dot_nt(qcat, kb) + bias_ref[0, pr]
        s2 = _dot_nt(qcat, kc_ref[0, :, cols])
        m = jnp.maximum(jnp.max(s1, axis=-1, keepdims=True), jnp.max(s2, axis=-1, keepdims=True))
        p1 = jnp.exp(s1 - m)
        p2 = jnp.exp(s2 - m)
        l = jnp.sum(p1, axis=-1, keepdims=True) + jnp.sum(p2, axis=-1, keepdims=True)
        o = (_dot(p1.astype(BF16), vb) + _dot(p2.astype(BF16), vc_ref[0, :, cols])) / l
        o_ref[0, :, cols] = jnp.where(first, o[:rq], o[rq:]).astype(BF16)


def _na_attention(nq, nk, nv, nkc, nvc, bias, pattern_ids):
    b, n, _ = nq.shape
    lc = nkc.shape[1]
    rows = n // GRID_W
    kh = min(NA_WIN_H, rows)
    rq = NA_ROW_GROUP * GRID_W
    grid_spec = pltpu.PrefetchScalarGridSpec(
        num_scalar_prefetch=1,
        grid=(b, rows // NA_ROW_GROUP),
        in_specs=[
            pl.BlockSpec((1, rq, HEAD_W), lambda bi, g, pid: (bi, g, 0)),
            pl.BlockSpec((1, n, HEAD_W), lambda bi, g, pid: (bi, 0, 0)),
            pl.BlockSpec((1, n, HEAD_W), lambda bi, g, pid: (bi, 0, 0)),
            pl.BlockSpec((1, lc, HEAD_W), lambda bi, g, pid: (bi, 0, 0)),
            pl.BlockSpec((1, lc, HEAD_W), lambda bi, g, pid: (bi, 0, 0)),
            pl.BlockSpec((1,) + bias.shape[1:], lambda bi, g, pid: (pid[g], 0, 0, 0)),
        ],
        out_specs=pl.BlockSpec((1, rq, HEAD_W), lambda bi, g, pid: (bi, g, 0)),
    )
    return pl.pallas_call(
        functools.partial(_na_kernel, rows=rows, kh=kh),
        grid_spec=grid_spec,
        out_shape=jax.ShapeDtypeStruct((b, n, HEAD_W), BF16),
        compiler_params=_cparams(("parallel", "arbitrary")),
        name="na_attention",
    )(pattern_ids, nq, nk, nv, nkc, nvc, bias)


def _na_patterns(rows):
    kh = min(NA_WIN_H, rows)
    key_rows = NA_ROW_GROUP + kh - 1
    assert rows % NA_ROW_GROUP == 0 and rows >= key_rows
    patterns, ids = [], []
    for g in range(rows // NA_ROW_GROUP):
        u = min(max(NA_ROW_GROUP * g - kh // 2, 0), rows - key_rows)
        geo = []
        for r in range(NA_ROW_GROUP * g, NA_ROW_GROUP * (g + 1)):
            rs = min(max(r - kh // 2, 0), rows - kh)
            geo.append((r - u, rs - u))
        geo = tuple(geo)
        if geo not in patterns:
            patterns.append(geo)
        ids.append(patterns.index(geo))
    return patterns, ids


def _na_bias_table(rpb, rows):
    kh = min(NA_WIN_H, rows)
    key_rows = NA_ROW_GROUP + kh - 1
    patterns, ids = _na_patterns(rows)
    geo = jnp.asarray(patterns, dtype=jnp.int32)
    rq_off, win_off = geo[..., 0], geo[..., 1]
    a = jnp.arange(key_rows)
    row_ok = (a >= win_off[..., None]) & (a < win_off[..., None] + kh)
    row_idx = jnp.clip(a - rq_off[..., None] + NA_WIN_H - 1, 0, 2 * NA_WIN_H - 2)
    cols = jnp.arange(GRID_W)
    col_start = jnp.clip(cols - NA_WIN_W // 2, 0, GRID_W - NA_WIN_W)
    kc = jnp.arange(GRID_W)
    col_ok = (kc[None, :] >= col_start[:, None]) & (kc[None, :] < col_start[:, None] + NA_WIN_W)
    col_idx = jnp.clip(kc[None, :] - cols[:, None] + NA_WIN_W - 1, 0, 2 * NA_WIN_W - 2)
    t = rpb[:, row_idx]
    pick = (jnp.arange(2 * NA_WIN_W - 1)[:, None, None] == col_idx[None]).astype(F32)
    t = jnp.einsum('hpraj,jck->hprack', t, pick, precision=lax.Precision.HIGHEST)
    ok = row_ok[None, :, :, :, None, None] & col_ok[None, None, None, None]
    t = jnp.where(ok, t, MASK_VALUE)
    t = t.transpose(1, 0, 2, 4, 3, 5)
    n_pat = len(patterns)
    t = t.reshape(n_pat, N_PAIRS, 2 * NA_ROW_GROUP * GRID_W, key_rows * GRID_W)
    return t.astype(F32), jnp.asarray(ids, dtype=jnp.int32)


def _ctx_na_kernel(q_ref, k_ref, v_ref, o_ref):
    tq = q_ref.shape[1]
    lane = lax.broadcasted_iota(jnp.int32, (tq, LANES), 1)
    for pr in range(N_PAIRS):
        cols = slice(pr * LANES, (pr + 1) * LANES)
        q2 = q_ref[0, :, cols]
        kb = k_ref[0, :, cols]
        vb = v_ref[0, :, cols]
        acc = jnp.zeros((tq, LANES), F32)
        for hh in range(2):
            in_head = (lane >= hh * NA_HEAD_DIM) & (lane < (hh + 1) * NA_HEAD_DIM)
            qm = jnp.where(in_head, q2, jnp.zeros_like(q2))
            s = _dot_nt(qm, kb)
            p = jnp.exp(s - jnp.max(s, axis=-1, keepdims=True))
            l = jnp.sum(p, axis=-1, keepdims=True)
            acc = jnp.where(in_head, _dot(p.astype(BF16), vb) / l, acc)
        o_ref[0, :, cols] = acc.astype(BF16)


def _ctx_na_attention(q, k, v):
    b, n, _ = q.shape
    spec = pl.BlockSpec((1, n, HEAD_W), lambda bi: (bi, 0, 0))
    return pl.pallas_call(
        _ctx_na_kernel,
        grid=(b,),
        in_specs=[spec, spec, spec],
        out_specs=spec,
        out_shape=jax.ShapeDtypeStruct((b, n, HEAD_W), BF16),
        compiler_params=_cparams(("parallel",)),
        name="ctx_na_attention",
    )(q, k, v)


def _diff_kernel(q_ref, lam_ref, g_ref, *refs, lam_init, n_src):
    src_refs, o_ref, k_ref, v_ref = refs[:2 * n_src], refs[2 * n_src], refs[-2], refs[-1]

    @pl.when(pl.program_id(1) == 0)
    def _():
        off = 0
        for j in range(n_src):
            n = src_refs[2 * j].shape[1]
            k_ref[off:off + n] = src_refs[2 * j][0]
            v_ref[off:off + n] = src_refs[2 * j + 1][0]
            off += n

    tq = q_ref.shape[1]
    lp = lam_ref[...]
    lam = (jnp.exp(jnp.sum(lp[0:1] * lp[1:2], axis=-1, keepdims=True))
           - jnp.exp(jnp.sum(lp[2:3] * lp[3:4], axis=-1, keepdims=True)) + lam_init)
    lane = lax.broadcasted_iota(jnp.int32, (tq, LANES), 1)
    for pr in range(N_PAIRS):
        cols = slice(pr * LANES, (pr + 1) * LANES)
        q2 = q_ref[0, :, cols]
        zero = jnp.zeros_like(q2)
        outp = jnp.zeros((tq, LANES), F32)
        for hh in range(2):
            comp = []
            for c in range(2):
                lo = hh * DF_V_DIM + c * DF_QK_DIM
                qm = jnp.where((lane >= lo) & (lane < lo + DF_QK_DIM), q2, zero)
                s = _dot_nt(qm, k_ref[:, cols])
                p = jnp.exp2(s - jnp.max(s, axis=-1, keepdims=True)).astype(BF16)
                res = _dot(p, v_ref[:, 2 * pr * LANES:2 * (pr + 1) * LANES])
                comp.append(res[:, :LANES] / res[:, LANES:LANES + 1])
            in_head = (lane >= hh * DF_V_DIM) & (lane < (hh + 1) * DF_V_DIM)
            oh = jnp.where(in_head, comp[0] - lam * comp[1], 0.0)
            ms = jnp.sum(oh * oh, axis=-1, keepdims=True) * (1.0 / DF_V_DIM)
            outp = outp + oh * lax.rsqrt(ms + EPS) * g_ref[...] * (1.0 - lam_init)
        o_ref[0, :, cols] = outp.astype(BF16)


def _diff_attention(dq, kvs, lam_p, subln_g, lam_init):
    b, nq, _ = dq.shape
    tq = min(256, nq)
    nk = sum(k.shape[1] for k, _ in kvs)
    kv_specs, kv_args = [], []
    for k, v in kvs:
        kv_specs += [pl.BlockSpec((1, k.shape[1], HEAD_W), lambda bi, i: (bi, 0, 0)),
                     pl.BlockSpec((1, k.shape[1], 2 * HEAD_W), lambda bi, i: (bi, 0, 0))]
        kv_args += [k, v]
    return pl.pallas_call(
        functools.partial(_diff_kernel, lam_init=lam_init, n_src=len(kvs)),
        grid=(b, nq // tq),
        in_specs=[
            pl.BlockSpec((1, tq, HEAD_W), lambda bi, i: (bi, i, 0)),
            pl.BlockSpec(lam_p.shape, lambda bi, i: (0, 0)),
            pl.BlockSpec(subln_g.shape, lambda bi, i: (0, 0)),
        ] + kv_specs,
        out_specs=pl.BlockSpec((1, tq, HEAD_W), lambda bi, i: (bi, i, 0)),
        out_shape=jax.ShapeDtypeStruct((b, nq, HEAD_W), BF16),
        scratch_shapes=[pltpu.VMEM((nk, HEAD_W), BF16), pltpu.VMEM((nk, 2 * HEAD_W), BF16)],
        compiler_params=_cparams(("parallel", "arbitrary")),
        name="diff_attention",
    )(dq, lam_p, subln_g, *kv_args)


def _merge_kernel(x_ref, f_ref, na_ref, df_ref, gate_ref, g1_ref, wft_ref, wna_ref, wdf_ref, wout_ref,
                  ng_ref, sh_ref, sc_ref, wrh_ref, wrl_ref, xo_ref, h2_ref, lg_ref):
    d = x_ref.shape[2]
    y_ft = _dot(f_ref[0], wft_ref[...])
    y_na = _dot(na_ref[0], wna_ref[...])
    y_df = _dot(df_ref[0], wdf_ref[...])
    m = (gate_ref[0, :, 0:d].astype(F32) * y_ft + gate_ref[0, :, d:2 * d].astype(F32) * y_na
         + gate_ref[0, :, 2 * d:3 * d].astype(F32) * y_df)
    y = _dot(m.astype(BF16), wout_ref[...])
    xn = x_ref[0] + g1_ref[0] * y
    xo_ref[0] = xn
    ms = jnp.mean(xn * xn, axis=-1, keepdims=True)
    h2 = (xn * lax.rsqrt(ms + EPS) * ng_ref[...]) * (1.0 + sc_ref[0]) + sh_ref[0]
    h_hi, h_lo = _split_bf16(h2)
    h2_ref[0] = h_hi
    lg_ref[0] = _dot(h_hi, wrh_ref[...]) + _dot(h_lo, wrh_ref[...]) + _dot(h_hi, wrl_ref[...])


def _merge(x, f, o_na, o_df, gates, g1, w_ft, w_na_o, w_df_o, w_out, norm_g, sh2, sc2, wr_hi, wr_lo):
    b, n, d = x.shape
    tm = min(512, n)
    tok = lambda w: pl.BlockSpec((1, tm, w), lambda bi, i: (bi, i, 0))
    full = lambda a: pl.BlockSpec(a.shape, lambda bi, i: (0,) * a.ndim)
    mod = pl.BlockSpec((1, 1, d), lambda bi, i: (bi, 0, 0))
    return pl.pallas_call(
        _merge_kernel,
        grid=(b, n // tm),
        in_specs=[tok(d), tok(FT_WIDTH), tok(HEAD_W), tok(HEAD_W), tok(N_BRANCHES * d), mod,
                  full(w_ft), full(w_na_o), full(w_df_o), full(w_out), full(norm_g), mod, mod,
                  full(wr_hi), full(wr_lo)],
        out_specs=[tok(d), tok(d), tok(ROUTER_PAD)],
        out_shape=[jax.ShapeDtypeStruct((b, n, d), F32), jax.ShapeDtypeStruct((b, n, d), BF16),
                   jax.ShapeDtypeStruct((b, n, ROUTER_PAD), F32)],
        compiler_params=_cparams(("parallel", "parallel")),
        name="merge",
    )(x, f, o_na, o_df, gates, g1, w_ft, w_na_o, w_df_o, w_out, norm_g, sh2, sc2, wr_hi, wr_lo)


ROUTE_TILE = LANES
SLOT_ALIGN = 16
ONE_BITS = 0x3F800000


FAST_WINDOW = 48


def _slot_window(cap):
    return min(ROUTE_TILE + SLOT_ALIGN, cap)


def _fast_window(cap):
    return min(FAST_WINDOW, cap)


def _slot_windows(lo, cap):
    b, nt, _ = lo.shape
    filled = jnp.concatenate([lo[:, 1:], jnp.full((b, 1, N_EXPERTS), cap, lo.dtype)], axis=1)
    aligned = (lo // SLOT_ALIGN) * SLOT_ALIGN
    lo_slow = jnp.minimum(aligned, cap - _slot_window(cap))
    lo_fast = jnp.minimum(aligned, cap - _fast_window(cap))
    fast_ok = jnp.all(filled - lo_fast <= _fast_window(cap), axis=-1)
    return {"lo_slow": lo_slow.reshape(-1), "lo_fast": lo_fast.reshape(-1),
            "fast_ok": fast_ok.astype(jnp.int32).reshape(-1)}


def _route_kernel(lg_ref, pos_ref, aff_ref, lo_ref, bits_ref, sel_ref, packed_ref, *, cap):
    n = lg_ref.shape[1]
    nb = n // ROUTE_TILE
    lane = lax.broadcasted_iota(jnp.int32, (n, LANES), 1)
    z = jnp.where(lane < N_EXPERTS, lg_ref[0], MASK_VALUE)
    p = jnp.exp(z - jnp.max(z, axis=-1, keepdims=True))
    aff = p / jnp.sum(p, axis=-1, keepdims=True)
    aff_ref[0] = aff
    bits_ref[...] = pltpu.bitcast(aff, jnp.int32)

    groups = LANES // N_EXPERTS
    rows_p = n // groups
    packed = bits_ref[0:rows_p]
    for j in range(1, groups):
        packed = packed + pltpu.roll(bits_ref[j * rows_p:(j + 1) * rows_p], N_EXPERTS * j, 1)
    packed_ref[...] = packed

    def count(mask):
        part = jnp.broadcast_to(jnp.sum(jnp.where(mask, 1.0, 0.0), axis=0, keepdims=True), (8, LANES))
        tot = part
        for j in range(1, groups):
            tot = tot + pltpu.roll(part, N_EXPERTS * j, 1)
        return tot[0:1]

    def bisect(_, carry):
        lo, hi = carry
        mid = (lo + hi) >> 1
        ge = count(packed_ref[...] >= mid) >= cap
        return jnp.where(ge, mid, lo), jnp.where(ge, hi, mid)

    lo0 = jnp.zeros((1, LANES), jnp.int32)
    hi0 = jnp.full((1, LANES), ONE_BITS + 1, jnp.int32)
    thr, _ = lax.fori_loop(0, 31, bisect, (lo0, hi0))
    need = cap - count(packed_ref[...] > thr)

    row = lax.broadcasted_iota(jnp.int32, (ROUTE_TILE, ROUTE_TILE), 0)
    col = lax.broadcasted_iota(jnp.int32, (ROUTE_TILE, ROUTE_TILE), 1)
    tri = jnp.where(row >= col, 1.0, 0.0).astype(BF16)

    carry = jnp.zeros((1, LANES), F32)
    for blk in range(nb):
        rows = slice(blk * ROUTE_TILE, (blk + 1) * ROUTE_TILE)
        bb = bits_ref[rows]
        eq = jnp.where(bb == thr, 1.0, 0.0)
        incl = _dot(tri, eq.astype(BF16))
        before = incl - eq + carry
        take = jnp.where(before < need, eq, 0.0)
        sel_ref[rows] = jnp.where(bb > thr, 1.0, take)
        carry = carry + incl[ROUTE_TILE - 1:ROUTE_TILE]

    carry = jnp.zeros((1, LANES), F32)
    for blk in range(nb):
        rows = slice(blk * ROUTE_TILE, (blk + 1) * ROUTE_TILE)
        sel = sel_ref[rows]
        incl = _dot(tri, sel.astype(BF16))
        pos_ref[0, rows] = jnp.where(sel > 0.0, carry + incl - sel, -1.0)
        lo_ref[0, blk:blk + 1] = carry.astype(jnp.int32)
        carry = carry + incl[ROUTE_TILE - 1:ROUTE_TILE]


def _route(logits, cap):
    b, n, _ = logits.shape
    nt = n // ROUTE_TILE
    tok = pl.BlockSpec((1, n, LANES), lambda bi: (bi, 0, 0))
    return pl.pallas_call(
        functools.partial(_route_kernel, cap=cap),
        grid=(b,),
        in_specs=[tok],
        out_specs=[tok, tok, pl.BlockSpec((1, nt, LANES), lambda bi: (bi, 0, 0))],
        out_shape=[jax.ShapeDtypeStruct((b, n, LANES), F32), jax.ShapeDtypeStruct((b, n, LANES), F32),
                   jax.ShapeDtypeStruct((b, nt, LANES), jnp.int32)],
        scratch_shapes=[pltpu.VMEM((n, LANES), jnp.int32), pltpu.VMEM((n, LANES), F32),
                        pltpu.VMEM((n // (LANES // N_EXPERTS), LANES), jnp.int32)],
        compiler_params=_cparams(("parallel",)),
        name="route",
    )(logits)


DISPATCH_COLS = 256


def _dispatch_kernel(lo_fast_ref, lo_slow_ref, fast_ref, h_ref, pos_ref, o_ref, *, win_fast, win_slow):
    b = pl.program_id(0)
    step = pl.program_id(1)
    per_step = pos_ref.shape[1]
    nt = pl.num_programs(1) * per_step

    @pl.when(step == 0)
    def _():
        o_ref[...] = jnp.zeros_like(o_ref)

    def run(sub, t, win, lo_ref):
        slot = lax.broadcasted_iota(jnp.int32, (win, ROUTE_TILE), 0).astype(F32)
        pos_t = pos_ref[0, sub]
        los = [pl.multiple_of(lo_ref[(b * nt + t) * N_EXPERTS + e], SLOT_ALIGN) for e in range(N_EXPERTS)]
        onehot = jnp.concatenate(
            [jnp.where(pos_t[e:e + 1, :] - los[e].astype(F32) == slot, 1.0, 0.0).astype(BF16)
             for e in range(N_EXPERTS)], axis=0)
        tok_rows = slice(sub * ROUTE_TILE, (sub + 1) * ROUTE_TILE)
        for c0 in range(0, h_ref.shape[2], DISPATCH_COLS):
            cols = slice(c0, c0 + DISPATCH_COLS)
            res = _dot(onehot, h_ref[0, tok_rows, cols]).astype(BF16)
            for e in range(N_EXPERTS):
                rows = pl.ds(los[e], win)
                o_ref[0, e, rows, cols] = o_ref[0, e, rows, cols] + res[e * win:(e + 1) * win]

    for sub in range(per_step):
        t = step * per_step + sub
        if win_fast == win_slow:
            run(sub, t, win_slow, lo_slow_ref)
        else:
            fast = fast_ref[b * nt + t] == 1
            pl.when(fast)(functools.partial(run, sub, t, win_fast, lo_fast_ref))
            pl.when(jnp.logical_not(fast))(functools.partial(run, sub, t, win_slow, lo_slow_ref))


def _tiles_per_step(nt):
    for per_step in (4, 2):
        if nt % per_step == 0:
            return per_step
    return 1


def _dispatch(windows, h2, pos_t, cap):
    b, n, d = h2.shape
    nt = n // ROUTE_TILE
    per_step = _tiles_per_step(nt)
    grid_spec = pltpu.PrefetchScalarGridSpec(
        num_scalar_prefetch=3,
        grid=(b, nt // per_step),
        in_specs=[
            pl.BlockSpec((1, per_step * ROUTE_TILE, d), lambda bi, t, *_: (bi, t, 0)),
            pl.BlockSpec((1, per_step, N_EXPERTS, ROUTE_TILE), lambda bi, t, *_: (bi, t, 0, 0)),
        ],
        out_specs=pl.BlockSpec((1, N_EXPERTS, cap, d), lambda bi, t, *_: (bi, 0, 0, 0)),
    )
    return pl.pallas_call(
        functools.partial(_dispatch_kernel, win_fast=_fast_window(cap), win_slow=_slot_window(cap)),
        grid_spec=grid_spec,
        out_shape=jax.ShapeDtypeStruct((b, N_EXPERTS, cap, d), BF16),
        compiler_params=_cparams(("arbitrary", "arbitrary")),
        name="dispatch",
    )(windows["lo_fast"], windows["lo_slow"], windows["fast_ok"], h2, pos_t)


def _expert_kernel(*refs, f_chunk, n_sets):
    x_refs, (wg_ref, wu_ref, wd_ref), o_refs = refs[:n_sets], refs[n_sets:n_sets + 3], refs[n_sets + 3:]
    xs = [r[0, 0] for r in x_refs]
    x = xs[0] if n_sets == 1 else jnp.concatenate(xs, axis=0)
    ff = wg_ref.shape[3]
    acc = jnp.zeros(x.shape, F32)
    for f0 in range(0, ff, f_chunk):
        a = _dot(x, wg_ref[0, 0, :, f0:f0 + f_chunk])
        u = _dot(x, wu_ref[0, 0, :, f0:f0 + f_chunk])
        hm = (a * jax.nn.sigmoid(a) * u).astype(BF16)
        acc = acc + _dot(hm, wd_ref[0, 0, f0:f0 + f_chunk, :])
    off = 0
    for xr, o_ref in zip(xs, o_refs):
        o_ref[0, 0] = acc[off:off + xr.shape[0]].astype(BF16)
        off += xr.shape[0]


def _experts(xes, w_gate, w_up, w_down, layer):
    b, e, _, d = xes[0].shape
    ff = w_gate.shape[3]
    toks = [pl.BlockSpec((1, 1, xe.shape[2], d), lambda ei, bi: (bi, ei, 0, 0)) for xe in xes]
    return pl.pallas_call(
        functools.partial(_expert_kernel, f_chunk=min(512, ff), n_sets=len(xes)),
        grid=(e, b),
        in_specs=toks + [
            pl.BlockSpec((1, 1, d, ff), lambda ei, bi: (layer, ei, 0, 0)),
            pl.BlockSpec((1, 1, d, ff), lambda ei, bi: (layer, ei, 0, 0)),
            pl.BlockSpec((1, 1, ff, d), lambda ei, bi: (layer, ei, 0, 0)),
        ],
        out_specs=toks,
        out_shape=[jax.ShapeDtypeStruct(xe.shape, BF16) for xe in xes],
        compiler_params=_cparams(("arbitrary", "arbitrary")),
        name="experts",
    )(*xes, w_gate, w_up, w_down)


def _combine_kernel(lo_fast_ref, lo_slow_ref, fast_ref, x_ref, y_ref, pos_ref, aff_ref, g_ref, spread_ref, slot_ref,
                    o_ref, *, win_fast, win_slow):
    b = pl.program_id(0)
    step = pl.program_id(1)
    tile = ROUTE_TILE
    per_step = x_ref.shape[1] // tile
    nt = pl.num_programs(1) * per_step

    def wide(rows, base):
        slot = lax.broadcasted_iota(jnp.int32, (tile, win_slow), 1).astype(F32)
        pos = pos_ref[0, rows]
        aff = aff_ref[0, rows]
        acc = jnp.zeros((tile, x_ref.shape[2]), F32)
        for e in range(N_EXPERTS):
            lo = pl.multiple_of(lo_slow_ref[base + e], SLOT_ALIGN)
            pick = jnp.where(pos[:, e:e + 1] - lo.astype(F32) == slot, aff[:, e:e + 1], 0.0).astype(BF16)
            acc = acc + _dot(pick, y_ref[0, e, pl.ds(lo, win_slow), :])
        o_ref[0, rows] = x_ref[0, rows] + g_ref[0] * acc

    def narrow(rows, base):
        lane = lax.broadcasted_iota(jnp.int32, (1, LANES), 1)
        lo_vec = jnp.zeros((1, LANES), F32)
        los = []
        for e in range(N_EXPERTS):
            lo = pl.multiple_of(lo_fast_ref[base + e], SLOT_ALIGN)
            los.append(lo)
            lo_vec = jnp.where(lane == e, lo.astype(F32), lo_vec)
        pos = pos_ref[0, rows]
        rel = jnp.where(pos >= 0.0, pos - lo_vec, -1.0).astype(BF16)
        rel_wide = _dot(rel, spread_ref[...])
        gate_wide = _dot(aff_ref[0, rows].astype(BF16), spread_ref[...])
        pick = jnp.where(rel_wide == slot_ref[...], gate_wide, 0.0).astype(BF16)
        y_cat = jnp.concatenate([y_ref[0, e, pl.ds(los[e], win_fast), :] for e in range(N_EXPERTS)], axis=0)
        o_ref[0, rows] = x_ref[0, rows] + g_ref[0] * _dot(pick, y_cat)

    for sub in range(per_step):
        t = step * per_step + sub
        rows = slice(sub * tile, (sub + 1) * tile)
        base = (b * nt + t) * N_EXPERTS
        if win_fast == win_slow:
            wide(rows, base)
        else:
            fast = fast_ref[b * nt + t] == 1
            pl.when(fast)(functools.partial(narrow, rows, base))
            pl.when(jnp.logical_not(fast))(functools.partial(wide, rows, base))


def _combine(windows, x, ye, pos, aff, g2, cap):
    b, n, d = x.shape
    e = ye.shape[1]
    nt = n // ROUTE_TILE
    per_step = _tiles_per_step(nt)
    tok = lambda w: pl.BlockSpec((1, per_step * ROUTE_TILE, w), lambda bi, t, *_: (bi, t, 0))
    win_fast = _fast_window(cap)
    j = jnp.arange(N_EXPERTS * win_fast)
    spread = (jnp.arange(LANES)[:, None] == (j // win_fast)[None, :]).astype(BF16)
    slot = (j % win_fast).astype(F32)[None, :]
    grid_spec = pltpu.PrefetchScalarGridSpec(
        num_scalar_prefetch=3,
        grid=(b, nt // per_step),
        in_specs=[
            tok(d),
            pl.BlockSpec((1, e, cap, d), lambda bi, t, *_: (bi, 0, 0, 0)),
            tok(LANES),
            tok(LANES),
            pl.BlockSpec((1, 1, d), lambda bi, t, *_: (bi, 0, 0)),
            pl.BlockSpec(spread.shape, lambda bi, t, *_: (0, 0)),
            pl.BlockSpec(slot.shape, lambda bi, t, *_: (0, 0)),
        ],
        out_specs=tok(d),
    )
    return pl.pallas_call(
        functools.partial(_combine_kernel, win_fast=win_fast, win_slow=_slot_window(cap)),
        grid_spec=grid_spec,
        out_shape=jax.ShapeDtypeStruct((b, n, d), F32),
        compiler_params=_cparams(("arbitrary", "arbitrary")),
        name="combine",
    )(windows["lo_fast"], windows["lo_slow"], windows["fast_ok"], x, ye, pos, aff, g2, spread, slot)


def _route_and_dispatch(h2, logits):
    b, n, _ = h2.shape
    cap = EC_CAPACITY_FACTOR * n // N_EXPERTS
    nt = n // ROUTE_TILE
    assert n % ROUTE_TILE == 0 and cap % SLOT_ALIGN == 0
    pos, aff, lo = _route(logits, cap)
    windows = _slot_windows(lo[:, :, :N_EXPERTS], cap)
    pos_t = pos[:, :, :N_EXPERTS].reshape(b, nt, ROUTE_TILE, N_EXPERTS).transpose(0, 1, 3, 2)
    xe = _dispatch(windows, h2, pos_t, cap)
    return xe, {"windows": windows, "pos": pos, "aff": aff, "cap": cap}


def _combine_residual(x, ye, routing, g2):
    return _combine(routing["windows"], x, ye, routing["pos"], routing["aff"], g2, routing["cap"])


def _dft_mats(n):
    k = jnp.arange(n, dtype=jnp.int32)
    ang = ((k[:, None] * k[None, :]) % n).astype(F32) * (2.0 * math.pi / n)
    return jnp.cos(ang), jnp.sin(ang)


def _dft_mats_bf16(n, block=64):
    assert n % block == 0
    k = jnp.arange(n // 2, dtype=jnp.int32)
    a = jnp.arange(n // block, dtype=jnp.int32) * block
    b = jnp.arange(block, dtype=jnp.int32)
    ang_a = ((a[:, None] * k[None, :]) % n).astype(F32) * (2.0 * math.pi / n)
    ang_b = ((b[:, None] * k[None, :]) % n).astype(F32) * (2.0 * math.pi / n)
    ca, sa = jnp.cos(ang_a)[:, None, :], jnp.sin(ang_a)[:, None, :]
    cb, sb = jnp.cos(ang_b)[None], jnp.sin(ang_b)[None]
    cos_m = (ca * cb - sa * sb).reshape(n, n // 2)
    nsin_m = (-(sa * cb + ca * sb)).reshape(n, n // 2)
    return cos_m.astype(BF16), nsin_m.astype(BF16)


def _chan_dft():
    c, s = _dft_mats(FT_GROUP_DIM)
    eye = jnp.eye(FT_GROUPS, dtype=F32)
    return jnp.concatenate([jnp.kron(eye, c), jnp.kron(eye, s)], axis=1).astype(BF16)


def _group_mean_mat(group):
    gid = jnp.arange(HEAD_W) // group
    return jnp.where(gid[:, None] == gid[None, :], 1.0 / group, 0.0).astype(BF16)


def _rope_tables(n):
    t = jnp.arange(n)
    row = (t // GRID_W).astype(F32)
    col = (t % GRID_W).astype(F32)
    ax = DF_QK_DIM // 2
    inv = ROPE_BASE ** (-jnp.arange(0, ax, 2, dtype=F32) / ax)
    lane = jnp.arange(LANES)
    freq = inv[lane % (ax // 2)]
    pos = jnp.where(((lane % DF_QK_DIM) < ax)[None, :], row[:, None], col[:, None])
    ang = pos * freq[None, :]
    sign = jnp.where((lane % ax) < ax // 2, -1.0, 1.0)
    return jnp.cos(ang), jnp.sin(ang) * sign[None, :]


def kernel(x, c, ctx, c_ctx, norm1_g, norm2_g, w_ada, b_ada, w_in, na_qn_g, na_kn_g, na_rpb, df_qn_g, df_kn_g,
           df_lambda, df_subln_g, w_ft, w_na_o, w_df_o, w_out, w_router, w_gate, w_up, w_down):
    b, n, d = x.shape
    lc = ctx.shape[1]
    depth = w_ada.shape[0]
    assert b + 1 <= MOD_ROWS and n % GRID_W == 0

    cc = jnp.zeros((MOD_ROWS, d), F32).at[:b].set(c).at[b].set(c_ctx)
    mods = _ada(cc, w_ada, b_ada)

    consts = {"chan_dft": _chan_dft(), "g64": _group_mean_mat(NA_HEAD_DIM), "g32": _group_mean_mat(DF_QK_DIM)}
    rope_lat = _rope_tables(n)
    rope_ctx = (jnp.zeros((lc, LANES), F32), jnp.zeros((lc, LANES), F32))
    dft_lat = _dft_mats_bf16(n)
    dft_ctx = _dft_mats_bf16(lc)
    rows = n // GRID_W

    w_gate_b, w_up_b, w_down_b = w_gate.astype(BF16), w_up.astype(BF16), w_down.astype(BF16)

    xc = ctx
    for i in range(depth):
        last = i == depth - 1
        lam_init = 0.8 - 0.6 * math.exp(-0.3 * i)
        m_lat = mods[i, :b].reshape(b, 6, 1, d)
        m_ctx = jnp.broadcast_to(mods[i, b].reshape(1, 6, 1, d), (b, 6, 1, d))
        sh1, sc1, g1, sh2, sc2, g2 = [m_lat[:, j] for j in range(6)]
        csh1, csc1, cg1, csh2, csc2, cg2 = [m_ctx[:, j] for j in range(6)]

        w_in_b = w_in[i].astype(BF16)
        gains = {
            "naq": jnp.tile(na_qn_g[i], NA_HEADS).reshape(1, HEAD_W),
            "nak": jnp.tile(na_kn_g[i], NA_HEADS).reshape(1, HEAD_W),
            "dfq": jnp.tile(df_qn_g[i], 2 * DF_HEADS).reshape(1, HEAD_W),
            "dfk": jnp.tile(df_kn_g[i], 2 * DF_HEADS).reshape(1, HEAD_W),
        }
        n1g = norm1_g[i].reshape(1, d)
        n2g = norm2_g[i].reshape(1, d)
        subln = jnp.tile(df_subln_g[i], 2).reshape(1, LANES)
        w_ft_b, w_na_b, w_df_b, w_out_b = (w.astype(BF16) for w in (w_ft[i], w_na_o[i], w_df_o[i], w_out[i]))
        wr = jnp.zeros((d, ROUTER_PAD), F32).at[:, :N_EXPERTS].set(w_router[i])
        wr_hi, wr_lo = _split_bf16(wr)

        if last:
            nkc, nvc, dkc, dvc = _inproj(xc, csh1, csc1, n1g, w_in_b, consts, gains, rope_ctx,
                                         rope=False, kv_only=True)
        else:
            fabc, nqc, dqc, nkc, nvc, dkc, dvc, gatec = _inproj(xc, csh1, csc1, n1g, w_in_b, consts, gains,
                                                                 rope_ctx, rope=False, kv_only=False)

        fab, nq, dq, nk, nv, dk, dv, gate = _inproj(x, sh1, sc1, n1g, w_in_b, consts, gains, rope_lat,
                                                     rope=True, kv_only=False)
        f = _fourier(fab, *dft_lat)
        o_na = _na_attention(nq, nk, nv, nkc, nvc, *_na_bias_table(na_rpb[i], rows))
        o_df = _diff_attention(dq, [(dk, dv), (dkc, dvc)], df_lambda[i], subln, lam_init)
        x, h2, logits = _merge(x, f, o_na, o_df, gate, g1, w_ft_b, w_na_b, w_df_b, w_out_b, n2g, sh2, sc2,
                               wr_hi, wr_lo)
        xe, routing = _route_and_dispatch(h2, logits)

        if last:
            (ye,) = _experts([xe], w_gate_b, w_up_b, w_down_b, i)
        else:
            fc = _fourier(fabc, *dft_ctx)
            o_nac = _ctx_na_attention(nqc, nkc, nvc)
            o_dfc = _diff_attention(dqc, [(dkc, dvc)], df_lambda[i], subln, lam_init)
            xc, hc2, logits_c = _merge(xc, fc, o_nac, o_dfc, gatec, cg1, w_ft_b, w_na_b, w_df_b, w_out_b, n2g,
                                       csh2, csc2, wr_hi, wr_lo)
            xec, routing_c = _route_and_dispatch(hc2, logits_c)
            ye, yec = _experts([xe, xec], w_gate_b, w_up_b, w_down_b, i)
            xc = _combine_residual(xc, yec, routing_c, cg2)
        x = _combine_residual(x, ye, routing, g2)
    return x
```

```python
import functools
import math

import jax
import jax.numpy as jnp
from jax import lax
from jax.experimental import pallas as pl
from jax.experimental.pallas import tpu as pltpu

F32 = jnp.float32
BF16 = jnp.bfloat16

GRID_W = 64
FT_GROUPS = 4
FT_GROUP_DIM = 64
FT_WIDTH = FT_GROUPS * FT_GROUP_DIM
NA_HEADS = 6
NA_HEAD_DIM = 64
NA_WIDTH = NA_HEADS * NA_HEAD_DIM
NA_WIN_H = 8
NA_WIN_W = 16
DF_HEADS = 6
DF_QK_DIM = 32
DF_V_DIM = 2 * DF_QK_DIM
DF_QK_WIDTH = DF_HEADS * 2 * DF_QK_DIM
DF_WIDTH = DF_HEADS * DF_V_DIM
N_BRANCHES = 3
N_EXPERTS = 16
EC_CAPACITY_FACTOR = 2
ROPE_BASE = 10000.0
EPS = 1e-6
MASK_VALUE = -1e30

LANES = 128
VMEM_LIMIT_BYTES = 56 * 1024 * 1024

HEAD_W = 384
N_PAIRS = HEAD_W // LANES
MOD_ROWS = 16
ROUTER_PAD = LANES


def _cparams(sem):
    return pltpu.CompilerParams(dimension_semantics=sem, vmem_limit_bytes=VMEM_LIMIT_BYTES)


def _dot(a, b):
    return jnp.dot(a, b, preferred_element_type=F32)


def _dot_nt(a, b):
    return lax.dot_general(a, b, (((1,), (1,)), ((), ())), preferred_element_type=F32)


def _split_bf16(v):
    hi = v.astype(BF16)
    lo = (v - hi.astype(F32)).astype(BF16)
    return hi, lo


def _ada_kernel(c_ref, w_ref, b_ref, o_ref):
    c = c_ref[...]
    a = c * jax.nn.sigmoid(c)
    a_hi, a_lo = _split_bf16(a)
    w_hi, w_lo = _split_bf16(w_ref[0])
    acc = _dot(a_hi, w_hi) + _dot(a_lo, w_hi) + _dot(a_hi, w_lo)
    o_ref[0] = acc + b_ref[0]


def _ada(cc, w_ada, b_ada):
    depth, d, d6 = w_ada.shape
    tn = 512
    return pl.pallas_call(
        _ada_kernel,
        grid=(depth, d6 // tn),
        in_specs=[
            pl.BlockSpec((MOD_ROWS, d), lambda l, j: (0, 0)),
            pl.BlockSpec((1, d, tn), lambda l, j: (l, 0, j)),
            pl.BlockSpec((1, 1, tn), lambda l, j: (l, 0, j)),
        ],
        out_specs=pl.BlockSpec((1, MOD_ROWS, tn), lambda l, j: (l, 0, j)),
        out_shape=jax.ShapeDtypeStruct((depth, MOD_ROWS, d6), F32),
        compiler_params=_cparams(("arbitrary", "arbitrary")),
        name="ada",
    )(cc, w_ada, b_ada.reshape(depth, 1, d6))


OFF_Q = FT_WIDTH
OFF_DQ = OFF_Q + NA_WIDTH
OFF_KV = OFF_DQ + DF_QK_WIDTH
OFF_NV = OFF_KV + NA_WIDTH
OFF_DK = OFF_NV + NA_WIDTH
OFF_DV = OFF_DK + DF_QK_WIDTH
OFF_GATE = OFF_DV + DF_WIDTH


def _group_rms(v, gmat_ref, gain_ref):
    ms = _dot((v * v).astype(BF16), gmat_ref[...])
    return v * lax.rsqrt(ms + EPS) * gain_ref[...]


def _rope_chunk(vj, cos, sin_signed, first_half):
    fwd = pltpu.roll(vj, LANES - 8, 1)
    bwd = pltpu.roll(vj, 8, 1)
    partner = jnp.where(first_half, fwd, bwd)
    return vj * cos + partner * sin_signed


def _inproj_kernel(x_ref, sh_ref, sc_ref, ng_ref, w_ref, cd_ref, g64_ref, g32_ref,
                   naq_g_ref, nak_g_ref, dfq_g_ref, dfk_g_ref, cos_ref, sin_ref,
                   *out_refs, rope, kv_only):
    x = x_ref[0]
    ms = jnp.mean(x * x, axis=-1, keepdims=True)
    y = x * lax.rsqrt(ms + EPS) * ng_ref[...]
    h = (y * (1.0 + sc_ref[0]) + sh_ref[0]).astype(BF16)

    def proj(c0, c1):
        return _dot(h, w_ref[:, c0:c1])

    if rope:
        lane = lax.broadcasted_iota(jnp.int32, (x.shape[0], LANES), 1)
        first_half = (lane & 8) == 0
        cos = cos_ref[...]
        sin_signed = sin_ref[...]

    def df_qk(v, gain_ref, scale, o_ref):
        v = _group_rms(v, g32_ref, gain_ref)
        for j in range(N_PAIRS):
            vj = v[:, j * LANES:(j + 1) * LANES]
            if rope:
                vj = _rope_chunk(vj, cos, sin_signed, first_half)
            o_ref[0, :, j * LANES:(j + 1) * LANES] = (vj * scale).astype(BF16)

    if kv_only:
        nk_ref, nv_ref, dk_ref, dv_ref = out_refs
    else:
        fab_ref, nq_ref, dq_ref, nk_ref, nv_ref, dk_ref, dv_ref, gate_ref = out_refs
        u = proj(0, FT_WIDTH)
        fab_ref[0] = _dot(u.astype(BF16), cd_ref[...]).astype(BF16)
        nq = _group_rms(proj(OFF_Q, OFF_DQ), g64_ref, naq_g_ref)
        nq_ref[0] = (nq * (NA_HEAD_DIM ** -0.5)).astype(BF16)
        df_qk(proj(OFF_DQ, OFF_KV), dfq_g_ref, DF_QK_DIM ** -0.5 * math.log2(math.e), dq_ref)
        d = x.shape[1]
        for j in range(N_BRANCHES):
            z = proj(OFF_GATE + j * d, OFF_GATE + (j + 1) * d)
            gate_ref[0, :, j * d:(j + 1) * d] = jax.nn.sigmoid(z).astype(BF16)

    nk_ref[0] = _group_rms(proj(OFF_KV, OFF_NV), g64_ref, nak_g_ref).astype(BF16)
    nv_ref[0] = proj(OFF_NV, OFF_DK).astype(BF16)
    df_qk(proj(OFF_DK, OFF_DV), dfk_g_ref, 1.0, dk_ref)
    dv = proj(OFF_DV, OFF_GATE).astype(BF16)
    for j in range(N_PAIRS):
        dv_ref[0, :, 2 * j * LANES:(2 * j + 1) * LANES] = dv[:, j * LANES:(j + 1) * LANES]
        dv_ref[0, :, (2 * j + 1) * LANES:(2 * j + 2) * LANES] = jnp.ones((dv.shape[0], LANES), BF16)


def _inproj(x, sh, sc, norm_g, w_in, consts, gains, rope_tabs, *, rope, kv_only):
    b, n, d = x.shape
    tm = min(512, n)
    tok = lambda w: pl.BlockSpec((1, tm, w), lambda bi, i: (bi, i, 0))
    full = lambda a: pl.BlockSpec(a.shape, lambda bi, i: (0,) * a.ndim)
    mod = pl.BlockSpec((1, 1, d), lambda bi, i: (bi, 0, 0))
    cos_t, sin_t = rope_tabs
    tab = pl.BlockSpec((tm, LANES), lambda bi, i: (i, 0))
    slab = jax.ShapeDtypeStruct((b, n, HEAD_W), BF16)
    slab_aug = jax.ShapeDtypeStruct((b, n, 2 * HEAD_W), BF16)
    if kv_only:
        out_shape = [slab] * 3 + [slab_aug]
        out_specs = [tok(HEAD_W)] * 3 + [tok(2 * HEAD_W)]
    else:
        out_shape = [jax.ShapeDtypeStruct((b, n, 2 * FT_WIDTH), BF16)] + [slab] * 5 + [
            slab_aug, jax.ShapeDtypeStruct((b, n, N_BRANCHES * d), BF16)]
        out_specs = [tok(2 * FT_WIDTH)] + [tok(HEAD_W)] * 5 + [tok(2 * HEAD_W), tok(N_BRANCHES * d)]
    args = [x, sh, sc, norm_g, w_in, consts["chan_dft"], consts["g64"], consts["g32"],
            gains["naq"], gains["nak"], gains["dfq"], gains["dfk"], cos_t, sin_t]
    in_specs = [tok(d), mod, mod, full(norm_g), full(w_in), full(consts["chan_dft"]),
                full(consts["g64"]), full(consts["g32"]), full(gains["naq"]), full(gains["nak"]),
                full(gains["dfq"]), full(gains["dfk"]), tab, tab]
    return pl.pallas_call(
        functools.partial(_inproj_kernel, rope=rope, kv_only=kv_only),
        grid=(b, n // tm),
        in_specs=in_specs,
        out_specs=out_specs,
        out_shape=out_shape,
        compiler_params=_cparams(("parallel", "parallel")),
        name="inproj_kv" if kv_only else "inproj",
    )(*args)


def _fourier_kernel(c_ref, s_ref, head_ref, tail_ref, mid_ref, o_ref, acc_ref, *, scale):
    k = pl.program_id(1)

    @pl.when(k == 0)
    def _():
        acc_ref[...] = jnp.zeros_like(acc_ref)

    cm = c_ref[...]
    sm = s_ref[...]
    for bi in range(head_ref.shape[0]):
        head = head_ref[bi].astype(F32)
        tail = tail_ref[bi].astype(F32)
        af = (head[:, :FT_WIDTH] + tail[:, :FT_WIDTH]).astype(BF16)
        bf = (head[:, FT_WIDTH:] - tail[:, FT_WIDTH:]).astype(BF16)
        acc_ref[bi] += _dot(cm, af) + _dot(sm, bf)

    @pl.when(k == pl.num_programs(1) - 1)
    def _():
        row = lax.broadcasted_iota(jnp.int32, (acc_ref.shape[1], 1), 0)
        sign = jnp.where((row & 1) == 0, 1.0, -1.0)
        for bi in range(head_ref.shape[0]):
            o_ref[bi] = ((acc_ref[bi] + sign * mid_ref[bi].astype(F32)) * scale).astype(o_ref.dtype)


def _flip_kernel(x_ref, o_ref):
    t = x_ref.shape[1]
    row = lax.broadcasted_iota(jnp.int32, (t, t), 0)
    col = lax.broadcasted_iota(jnp.int32, (t, t), 1)
    anti = jnp.where(row + col == t - 1, 1.0, 0.0).astype(BF16)
    o_ref[0] = _dot(anti, x_ref[0]).astype(o_ref.dtype)


def _reverse_rows(x):
    b, m, w = x.shape
    t = min(512, m)
    nb = m // t
    return pl.pallas_call(
        _flip_kernel,
        grid=(b, nb),
        in_specs=[pl.BlockSpec((1, t, w), lambda bi, j: (bi, nb - 1 - j, 0))],
        out_specs=pl.BlockSpec((1, t, w), lambda bi, j: (bi, j, 0)),
        out_shape=jax.ShapeDtypeStruct(x.shape, x.dtype),
        compiler_params=_cparams(("parallel", "parallel")),
        name="reverse_rows",
    )(x)


def _fourier(fab, cos_m, nsin_m):
    b, n, _ = fab.shape
    half = n // 2
    tn = min(1024, n)
    tk = min(512, half)
    assert tn % 2 == 0
    scale = 1.0 / math.sqrt(n * FT_GROUP_DIM)
    head = fab[:, :half]
    tail = _reverse_rows(jnp.concatenate([fab[:, half + 1:], jnp.zeros_like(fab[:, :1])], axis=1))
    mid = fab[:, half:half + 1, :FT_WIDTH]
    return pl.pallas_call(
        functools.partial(_fourier_kernel, scale=scale),
        grid=(n // tn, half // tk),
        in_specs=[
            pl.BlockSpec((tn, tk), lambda i, k: (i, k)),
            pl.BlockSpec((tn, tk), lambda i, k: (i, k)),
            pl.BlockSpec((b, tk, 2 * FT_WIDTH), lambda i, k: (0, k, 0)),
            pl.BlockSpec((b, tk, 2 * FT_WIDTH), lambda i, k: (0, k, 0)),
            pl.BlockSpec((b, 1, FT_WIDTH), lambda i, k: (0, 0, 0)),
        ],
        out_specs=pl.BlockSpec((b, tn, FT_WIDTH), lambda i, k: (0, i, 0)),
        out_shape=jax.ShapeDtypeStruct((b, n, FT_WIDTH), BF16),
        scratch_shapes=[pltpu.VMEM((b, tn, FT_WIDTH), F32)],
        compiler_params=_cparams(("parallel", "arbitrary")),
        name="fourier",
    )(cos_m, nsin_m, head, tail, mid)


NA_ROW_GROUP = 4


def _na_kernel(pid_ref, q_ref, k_ref, v_ref, kc_ref, vc_ref, bias_ref, o_ref, *, rows, kh):
    del pid_ref
    g = pl.program_id(1)
    rq = NA_ROW_GROUP * GRID_W
    key_rows = NA_ROW_GROUP + kh - 1
    u = jnp.clip(NA_ROW_GROUP * g - kh // 2, 0, rows - key_rows)
    start = pl.multiple_of(u * GRID_W, GRID_W)
    lane = lax.broadcasted_iota(jnp.int32, (rq, LANES), 1)
    first = lane < NA_HEAD_DIM
    for pr in range(N_PAIRS):
        cols = slice(pr * LANES, (pr + 1) * LANES)
        q2 = q_ref[0, :, cols]
        zero = jnp.zeros_like(q2)
        qcat = jnp.concatenate([jnp.where(first, q2, zero), jnp.where(first, zero, q2)], axis=0)
        kb = k_ref[0, pl.ds(start, key_rows * GRID_W), cols]
        vb = v_ref[0, pl.ds(start, key_rows * GRID_W), cols]
        s1 = _dot_nt(qcat, kb) + bias_ref[0, pr]
        s2 = _dot_nt(qcat, kc_ref[0, :, cols])
        m = jnp.maximum(jnp.max(s1, axis=-1, keepdims=True), jnp.max(s2, axis=-1, keepdims=True))
        p1 = jnp.exp(s1 - m)
        p2 = jnp.exp(s2 - m)
        l = jnp.sum(p1, axis=-1, keepdims=True) + jnp.sum(p2, axis=-1, keepdims=True)
        o = (_dot(p1.astype(BF16), vb) + _dot(p2.astype(BF16), vc_ref[0, :, cols])) / l
        o_ref[0, :, cols] = jnp.where(first, o[:rq], o[rq:]).astype(BF16)


def _na_attention(nq, nk, nv, nkc, nvc, bias, pattern_ids):
    b, n, _ = nq.shape
    lc = nkc.shape[1]
    rows = n // GRID_W
    kh = min(NA_WIN_H, rows)
    rq = NA_ROW_GROUP * GRID_W
    grid_spec = pltpu.PrefetchScalarGridSpec(
        num_scalar_prefetch=1,
        grid=(b, rows // NA_ROW_GROUP),
        in_specs=[
            pl.BlockSpec((1, rq, HEAD_W), lambda bi, g, pid: (bi, g, 0)),
            pl.BlockSpec((1, n, HEAD_W), lambda bi, g, pid: (bi, 0, 0)),
            pl.BlockSpec((1, n, HEAD_W), lambda bi, g, pid: (bi, 0, 0)),
            pl.BlockSpec((1, lc, HEAD_W), lambda bi, g, pid: (bi, 0, 0)),
            pl.BlockSpec((1, lc, HEAD_W), lambda bi, g, pid: (bi, 0, 0)),
            pl.BlockSpec((1,) + bias.shape[1:], lambda bi, g, pid: (pid[g], 0, 0, 0)),
        ],
        out_specs=pl.BlockSpec((1, rq, HEAD_W), lambda bi, g, pid: (bi, g, 0)),
    )
    return pl.pallas_call(
        functools.partial(_na_kernel, rows=rows, kh=kh),
        grid_spec=grid_spec,
        out_shape=jax.ShapeDtypeStruct((b, n, HEAD_W), BF16),
        compiler_params=_cparams(("parallel", "arbitrary")),
        name="na_attention",
    )(pattern_ids, nq, nk, nv, nkc, nvc, bias)


def _na_patterns(rows):
    kh = min(NA_WIN_H, rows)
    key_rows = NA_ROW_GROUP + kh - 1
    assert rows % NA_ROW_GROUP == 0 and rows >= key_rows
    patterns, ids = [], []
    for g in range(rows // NA_ROW_GROUP):
        u = min(max(NA_ROW_GROUP * g - kh // 2, 0), rows - key_rows)
        geo = []
        for r in range(NA_ROW_GROUP * g, NA_ROW_GROUP * (g + 1)):
            rs = min(max(r - kh // 2, 0), rows - kh)
            geo.append((r - u, rs - u))
        geo = tuple(geo)
        if geo not in patterns:
            patterns.append(geo)
        ids.append(patterns.index(geo))
    return patterns, ids


def _na_bias_table(rpb, rows):
    kh = min(NA_WIN_H, rows)
    key_rows = NA_ROW_GROUP + kh - 1
    patterns, ids = _na_patterns(rows)
    geo = jnp.asarray(patterns, dtype=jnp.int32)
    rq_off, win_off = geo[..., 0], geo[..., 1]
    a = jnp.arange(key_rows)
    row_ok = (a >= win_off[..., None]) & (a < win_off[..., None] + kh)
    row_idx = jnp.clip(a - rq_off[..., None] + NA_WIN_H - 1, 0, 2 * NA_WIN_H - 2)
    cols = jnp.arange(GRID_W)
    col_start = jnp.clip(cols - NA_WIN_W // 2, 0, GRID_W - NA_WIN_W)
    kc = jnp.arange(GRID_W)
    col_ok = (kc[None, :] >= col_start[:, None]) & (kc[None, :] < col_start[:, None] + NA_WIN_W)
    col_idx = jnp.clip(kc[None, :] - cols[:, None] + NA_WIN_W - 1, 0, 2 * NA_WIN_W - 2)
    t = rpb[:, row_idx]
    pick = (jnp.arange(2 * NA_WIN_W - 1)[:, None, None] == col_idx[None]).astype(F32)
    t = jnp.einsum('hpraj,jck->hprack', t, pick, precision=lax.Precision.HIGHEST)
    ok = row_ok[None, :, :, :, None, None] & col_ok[None, None, None, None]
    t = jnp.where(ok, t, MASK_VALUE)
    t = t.transpose(1, 0, 2, 4, 3, 5)
    n_pat = len(patterns)
    t = t.reshape(n_pat, N_PAIRS, 2 * NA_ROW_GROUP * GRID_W, key_rows * GRID_W)
    return t.astype(F32), jnp.asarray(ids, dtype=jnp.int32)


def _ctx_na_kernel(q_ref, k_ref, v_ref, o_ref):
    tq = q_ref.shape[1]
    lane = lax.broadcasted_iota(jnp.int32, (tq, LANES), 1)
    for pr in range(N_PAIRS):
        cols = slice(pr * LANES, (pr + 1) * LANES)
        q2 = q_ref[0, :, cols]
        kb = k_ref[0, :, cols]
        vb = v_ref[0, :, cols]
        acc = jnp.zeros((tq, LANES), F32)
        for hh in range(2):
            in_head = (lane >= hh * NA_HEAD_DIM) & (lane < (hh + 1) * NA_HEAD_DIM)
            qm = jnp.where(in_head, q2, jnp.zeros_like(q2))
            s = _dot_nt(qm, kb)
            p = jnp.exp(s - jnp.max(s, axis=-1, keepdims=True))
            l = jnp.sum(p, axis=-1, keepdims=True)
            acc = jnp.where(in_head, _dot(p.astype(BF16), vb) / l, acc)
        o_ref[0, :, cols] = acc.astype(BF16)


def _ctx_na_attention(q, k, v):
    b, n, _ = q.shape
    spec = pl.BlockSpec((1, n, HEAD_W), lambda bi: (bi, 0, 0))
    return pl.pallas_call(
        _ctx_na_kernel,
        grid=(b,),
        in_specs=[spec, spec, spec],
        out_specs=spec,
        out_shape=jax.ShapeDtypeStruct((b, n, HEAD_W), BF16),
        compiler_params=_cparams(("parallel",)),
        name="ctx_na_attention",
    )(q, k, v)


def _diff_kernel(q_ref, lam_ref, g_ref, *refs, lam_init, n_src):
    src_refs, o_ref, k_ref, v_ref = refs[:2 * n_src], refs[2 * n_src], refs[-2], refs[-1]

    @pl.when(pl.program_id(1) == 0)
    def _():
        off = 0
        for j in range(n_src):
            n = src_refs[2 * j].shape[1]
            k_ref[off:off + n] = src_refs[2 * j][0]
            v_ref[off:off + n] = src_refs[2 * j + 1][0]
            off += n

    tq = q_ref.shape[1]
    lp = lam_ref[...]
    lam = (jnp.exp(jnp.sum(lp[0:1] * lp[1:2], axis=-1, keepdims=True))
           - jnp.exp(jnp.sum(lp[2:3] * lp[3:4], axis=-1, keepdims=True)) + lam_init)
    lane = lax.broadcasted_iota(jnp.int32, (tq, LANES), 1)
    for pr in range(N_PAIRS):
        cols = slice(pr * LANES, (pr + 1) * LANES)
        q2 = q_ref[0, :, cols]
        zero = jnp.zeros_like(q2)
        outp = jnp.zeros((tq, LANES), F32)
        for hh in range(2):
            comp = []
            for c in range(2):
                lo = hh * DF_V_DIM + c * DF_QK_DIM
                qm = jnp.where((lane >= lo) & (lane < lo + DF_QK_DIM), q2, zero)
                s = _dot_nt(qm, k_ref[:, cols])
                p = jnp.exp2(s - jnp.max(s, axis=-1, keepdims=True)).astype(BF16)
                res = _dot(p, v_ref[:, 2 * pr * LANES:2 * (pr + 1) * LANES])
                comp.append(res[:, :LANES] / res[:, LANES:LANES + 1])
            in_head = (lane >= hh * DF_V_DIM) & (lane < (hh + 1) * DF_V_DIM)
            oh = jnp.where(in_head, comp[0] - lam * comp[1], 0.0)
            ms = jnp.sum(oh * oh, axis=-1, keepdims=True) * (1.0 / DF_V_DIM)
            outp = outp + oh * lax.rsqrt(ms + EPS) * g_ref[...] * (1.0 - lam_init)
        o_ref[0, :, cols] = outp.astype(BF16)


def _diff_attention(dq, kvs, lam_p, subln_g, lam_init):
    b, nq, _ = dq.shape
    tq = min(512, nq)
    nk = sum(k.shape[1] for k, _ in kvs)
    kv_specs, kv_args = [], []
    for k, v in kvs:
        kv_specs += [pl.BlockSpec((1, k.shape[1], HEAD_W), lambda bi, i: (bi, 0, 0)),
                     pl.BlockSpec((1, k.shape[1], 2 * HEAD_W), lambda bi, i: (bi, 0, 0))]
        kv_args += [k, v]
    return pl.pallas_call(
        functools.partial(_diff_kernel, lam_init=lam_init, n_src=len(kvs)),
        grid=(b, nq // tq),
        in_specs=[
            pl.BlockSpec((1, tq, HEAD_W), lambda bi, i: (bi, i, 0)),
            pl.BlockSpec(lam_p.shape, lambda bi, i: (0, 0)),
            pl.BlockSpec(subln_g.shape, lambda bi, i: (0, 0)),
        ] + kv_specs,
        out_specs=pl.BlockSpec((1, tq, HEAD_W), lambda bi, i: (bi, i, 0)),
        out_shape=jax.ShapeDtypeStruct((b, nq, HEAD_W), BF16),
        scratch_shapes=[pltpu.VMEM((nk, HEAD_W), BF16), pltpu.VMEM((nk, 2 * HEAD_W), BF16)],
        compiler_params=_cparams(("parallel", "arbitrary")),
        name="diff_attention",
    )(dq, lam_p, subln_g, *kv_args)


def _merge_kernel(x_ref, f_ref, na_ref, df_ref, gate_ref, g1_ref, wft_ref, wna_ref, wdf_ref, wout_ref,
                  ng_ref, sh_ref, sc_ref, wrh_ref, wrl_ref, xo_ref, h2_ref, lg_ref):
    d = x_ref.shape[2]
    y_ft = _dot(f_ref[0], wft_ref[...])
    y_na = _dot(na_ref[0], wna_ref[...])
    y_df = _dot(df_ref[0], wdf_ref[...])
    m = (gate_ref[0, :, 0:d].astype(F32) * y_ft + gate_ref[0, :, d:2 * d].astype(F32) * y_na
         + gate_ref[0, :, 2 * d:3 * d].astype(F32) * y_df)
    y = _dot(m.astype(BF16), wout_ref[...])
    xn = x_ref[0] + g1_ref[0] * y
    xo_ref[0] = xn
    ms = jnp.mean(xn * xn, axis=-1, keepdims=True)
    h2 = (xn * lax.rsqrt(ms + EPS) * ng_ref[...]) * (1.0 + sc_ref[0]) + sh_ref[0]
    h_hi, h_lo = _split_bf16(h2)
    h2_ref[0] = h_hi
    lg_ref[0] = _dot(h_hi, wrh_ref[...]) + _dot(h_lo, wrh_ref[...]) + _dot(h_hi, wrl_ref[...])


def _merge(x, f, o_na, o_df, gates, g1, w_ft, w_na_o, w_df_o, w_out, norm_g, sh2, sc2, wr_hi, wr_lo):
    b, n, d = x.shape
    tm = min(512, n)
    tok = lambda w: pl.BlockSpec((1, tm, w), lambda bi, i: (bi, i, 0))
    full = lambda a: pl.BlockSpec(a.shape, lambda bi, i: (0,) * a.ndim)
    mod = pl.BlockSpec((1, 1, d), lambda bi, i: (bi, 0, 0))
    return pl.pallas_call(
        _merge_kernel,
        grid=(b, n // tm),
        in_specs=[tok(d), tok(FT_WIDTH), tok(HEAD_W), tok(HEAD_W), tok(N_BRANCHES * d), mod,
                  full(w_ft), full(w_na_o), full(w_df_o), full(w_out), full(norm_g), mod, mod,
                  full(wr_hi), full(wr_lo)],
        out_specs=[tok(d), tok(d), tok(ROUTER_PAD)],
        out_shape=[jax.ShapeDtypeStruct((b, n, d), F32), jax.ShapeDtypeStruct((b, n, d), BF16),
                   jax.ShapeDtypeStruct((b, n, ROUTER_PAD), F32)],
        compiler_params=_cparams(("parallel", "parallel")),
        name="merge",
    )(x, f, o_na, o_df, gates, g1, w_ft, w_na_o, w_df_o, w_out, norm_g, sh2, sc2, wr_hi, wr_lo)


ROUTE_TILE = LANES
SLOT_ALIGN = 16
ONE_BITS = 0x3F800000


FAST_WINDOW = 48


def _slot_window(cap):
    return min(ROUTE_TILE + SLOT_ALIGN, cap)


def _fast_window(cap):
    return min(FAST_WINDOW, cap)


def _slot_windows(lo, cap):
    b, nt, _ = lo.shape
    filled = jnp.concatenate([lo[:, 1:], jnp.full((b, 1, N_EXPERTS), cap, lo.dtype)], axis=1)
    aligned = (lo // SLOT_ALIGN) * SLOT_ALIGN
    lo_slow = jnp.minimum(aligned, cap - _slot_window(cap))
    lo_fast = jnp.minimum(aligned, cap - _fast_window(cap))
    fast_ok = jnp.all(filled - lo_fast <= _fast_window(cap), axis=-1)
    return {"lo_slow": lo_slow.reshape(-1), "lo_fast": lo_fast.reshape(-1),
            "fast_ok": fast_ok.astype(jnp.int32).reshape(-1)}


def _route_kernel(lg_ref, pos_ref, aff_ref, lo_ref, bits_ref, sel_ref, packed_ref, *, cap):
    n = lg_ref.shape[1]
    nb = n // ROUTE_TILE
    lane = lax.broadcasted_iota(jnp.int32, (n, LANES), 1)
    z = jnp.where(lane < N_EXPERTS, lg_ref[0], MASK_VALUE)
    p = jnp.exp(z - jnp.max(z, axis=-1, keepdims=True))
    aff = p / jnp.sum(p, axis=-1, keepdims=True)
    aff_ref[0] = aff
    bits_ref[...] = pltpu.bitcast(aff, jnp.int32)

    groups = LANES // N_EXPERTS
    rows_p = n // groups
    packed = bits_ref[0:rows_p]
    for j in range(1, groups):
        packed = packed + pltpu.roll(bits_ref[j * rows_p:(j + 1) * rows_p], N_EXPERTS * j, 1)
    packed_ref[...] = packed

    def count(mask):
        part = jnp.broadcast_to(jnp.sum(jnp.where(mask, 1.0, 0.0), axis=0, keepdims=True), (8, LANES))
        tot = part
        for j in range(1, groups):
            tot = tot + pltpu.roll(part, N_EXPERTS * j, 1)
        return tot[0:1]

    def bisect(_, carry):
        lo, hi = carry
        mid = (lo + hi) >> 1
        ge = count(packed_ref[...] >= mid) >= cap
        return jnp.where(ge, mid, lo), jnp.where(ge, hi, mid)

    lo0 = jnp.zeros((1, LANES), jnp.int32)
    hi0 = jnp.full((1, LANES), ONE_BITS + 1, jnp.int32)
    thr, _ = lax.fori_loop(0, 31, bisect, (lo0, hi0))
    need = cap - count(packed_ref[...] > thr)

    row = lax.broadcasted_iota(jnp.int32, (ROUTE_TILE, ROUTE_TILE), 0)
    col = lax.broadcasted_iota(jnp.int32, (ROUTE_TILE, ROUTE_TILE), 1)
    tri = jnp.where(row >= col, 1.0, 0.0).astype(BF16)

    carry = jnp.zeros((1, LANES), F32)
    for blk in range(nb):
        rows = slice(blk * ROUTE_TILE, (blk + 1) * ROUTE_TILE)
        bb = bits_ref[rows]
        eq = jnp.where(bb == thr, 1.0, 0.0)
        incl = _dot(tri, eq.astype(BF16))
        before = incl - eq + carry
        take = jnp.where(before < need, eq, 0.0)
        sel_ref[rows] = jnp.where(bb > thr, 1.0, take)
        carry = carry + incl[ROUTE_TILE - 1:ROUTE_TILE]

    carry = jnp.zeros((1, LANES), F32)
    for blk in range(nb):
        rows = slice(blk * ROUTE_TILE, (blk + 1) * ROUTE_TILE)
        sel = sel_ref[rows]
        incl = _dot(tri, sel.astype(BF16))
        pos_ref[0, rows] = jnp.where(sel > 0.0, carry + incl - sel, -1.0)
        lo_ref[0, blk:blk + 1] = carry.astype(jnp.int32)
        carry = carry + incl[ROUTE_TILE - 1:ROUTE_TILE]


def _route(logits, cap):
    b, n, _ = logits.shape
    nt = n // ROUTE_TILE
    tok = pl.BlockSpec((1, n, LANES), lambda bi: (bi, 0, 0))
    return pl.pallas_call(
        functools.partial(_route_kernel, cap=cap),
        grid=(b,),
        in_specs=[tok],
        out_specs=[tok, tok, pl.BlockSpec((1, nt, LANES), lambda bi: (bi, 0, 0))],
        out_shape=[jax.ShapeDtypeStruct((b, n, LANES), F32), jax.ShapeDtypeStruct((b, n, LANES), F32),
                   jax.ShapeDtypeStruct((b, nt, LANES), jnp.int32)],
        scratch_shapes=[pltpu.VMEM((n, LANES), jnp.int32), pltpu.VMEM((n, LANES), F32),
                        pltpu.VMEM((n // (LANES // N_EXPERTS), LANES), jnp.int32)],
        compiler_params=_cparams(("parallel",)),
        name="route",
    )(logits)


DISPATCH_COLS = 256


def _dispatch_kernel(lo_fast_ref, lo_slow_ref, fast_ref, h_ref, pos_ref, o_ref, *, win_fast, win_slow):
    b = pl.program_id(0)
    step = pl.program_id(1)
    per_step = pos_ref.shape[1]
    nt = pl.num_programs(1) * per_step

    @pl.when(step == 0)
    def _():
        o_ref[...] = jnp.zeros_like(o_ref)

    def run(sub, t, win, lo_ref):
        slot = lax.broadcasted_iota(jnp.int32, (win, ROUTE_TILE), 0).astype(F32)
        pos_t = pos_ref[0, sub]
        los = [pl.multiple_of(lo_ref[(b * nt + t) * N_EXPERTS + e], SLOT_ALIGN) for e in range(N_EXPERTS)]
        onehot = jnp.concatenate(
            [jnp.where(pos_t[e:e + 1, :] - los[e].astype(F32) == slot, 1.0, 0.0).astype(BF16)
             for e in range(N_EXPERTS)], axis=0)
        tok_rows = slice(sub * ROUTE_TILE, (sub + 1) * ROUTE_TILE)
        for c0 in range(0, h_ref.shape[2], DISPATCH_COLS):
            cols = slice(c0, c0 + DISPATCH_COLS)
            res = _dot(onehot, h_ref[0, tok_rows, cols]).astype(BF16)
            for e in range(N_EXPERTS):
                rows = pl.ds(los[e], win)
                o_ref[0, e, rows, cols] = o_ref[0, e, rows, cols] + res[e * win:(e + 1) * win]

    for sub in range(per_step):
        t = step * per_step + sub
        if win_fast == win_slow:
            run(sub, t, win_slow, lo_slow_ref)
        else:
            fast = fast_ref[b * nt + t] == 1
            pl.when(fast)(functools.partial(run, sub, t, win_fast, lo_fast_ref))
            pl.when(jnp.logical_not(fast))(functools.partial(run, sub, t, win_slow, lo_slow_ref))


def _tiles_per_step(nt):
    for per_step in (4, 2):
        if nt % per_step == 0:
            return per_step
    return 1


def _dispatch(windows, h2, pos_t, cap):
    b, n, d = h2.shape
    nt = n // ROUTE_TILE
    per_step = _tiles_per_step(nt)
    grid_spec = pltpu.PrefetchScalarGridSpec(
        num_scalar_prefetch=3,
        grid=(b, nt // per_step),
        in_specs=[
            pl.BlockSpec((1, per_step * ROUTE_TILE, d), lambda bi, t, *_: (bi, t, 0)),
            pl.BlockSpec((1, per_step, N_EXPERTS, ROUTE_TILE), lambda bi, t, *_: (bi, t, 0, 0)),
        ],
        out_specs=pl.BlockSpec((1, N_EXPERTS, cap, d), lambda bi, t, *_: (bi, 0, 0, 0)),
    )
    return pl.pallas_call(
        functools.partial(_dispatch_kernel, win_fast=_fast_window(cap), win_slow=_slot_window(cap)),
        grid_spec=grid_spec,
        out_shape=jax.ShapeDtypeStruct((b, N_EXPERTS, cap, d), BF16),
        compiler_params=_cparams(("arbitrary", "arbitrary")),
        name="dispatch",
    )(windows["lo_fast"], windows["lo_slow"], windows["fast_ok"], h2, pos_t)


def _expert_kernel(*refs, f_chunk, n_sets):
    x_refs, (wg_ref, wu_ref, wd_ref), o_refs = refs[:n_sets], refs[n_sets:n_sets + 3], refs[n_sets + 3:]
    xs = [r[0, 0] for r in x_refs]
    x = xs[0] if n_sets == 1 else jnp.concatenate(xs, axis=0)
    ff = wg_ref.shape[3]
    acc = jnp.zeros(x.shape, F32)
    for f0 in range(0, ff, f_chunk):
        a = _dot(x, wg_ref[0, 0, :, f0:f0 + f_chunk])
        u = _dot(x, wu_ref[0, 0, :, f0:f0 + f_chunk])
        hm = (a * jax.nn.sigmoid(a) * u).astype(BF16)
        acc = acc + _dot(hm, wd_ref[0, 0, f0:f0 + f_chunk, :])
    off = 0
    for xr, o_ref in zip(xs, o_refs):
        o_ref[0, 0] = acc[off:off + xr.shape[0]].astype(BF16)
        off += xr.shape[0]


def _experts(xes, w_gate, w_up, w_down, layer):
    b, e, _, d = xes[0].shape
    ff = w_gate.shape[3]
    toks = [pl.BlockSpec((1, 1, xe.shape[2], d), lambda ei, bi: (bi, ei, 0, 0)) for xe in xes]
    return pl.pallas_call(
        functools.partial(_expert_kernel, f_chunk=min(512, ff), n_sets=len(xes)),
        grid=(e, b),
        in_specs=toks + [
            pl.BlockSpec((1, 1, d, ff), lambda ei, bi: (layer, ei, 0, 0)),
            pl.BlockSpec((1, 1, d, ff), lambda ei, bi: (layer, ei, 0, 0)),
            pl.BlockSpec((1, 1, ff, d), lambda ei, bi: (layer, ei, 0, 0)),
        ],
        out_specs=toks,
        out_shape=[jax.ShapeDtypeStruct(xe.shape, BF16) for xe in xes],
        compiler_params=_cparams(("arbitrary", "arbitrary")),
        name="experts",
    )(*xes, w_gate, w_up, w_down)


def _combine_kernel(lo_fast_ref, lo_slow_ref, fast_ref, x_ref, y_ref, pos_ref, aff_ref, g_ref, spread_ref, slot_ref,
                    o_ref, *, win_fast, win_slow):
    b = pl.program_id(0)
    step = pl.program_id(1)
    tile = ROUTE_TILE
    per_step = x_ref.shape[1] // tile
    nt = pl.num_programs(1) * per_step

    def wide(rows, base):
        slot = lax.broadcasted_iota(jnp.int32, (tile, win_slow), 1).astype(F32)
        pos = pos_ref[0, rows]
        aff = aff_ref[0, rows]
        acc = jnp.zeros((tile, x_ref.shape[2]), F32)
        for e in range(N_EXPERTS):
            lo = pl.multiple_of(lo_slow_ref[base + e], SLOT_ALIGN)
            pick = jnp.where(pos[:, e:e + 1] - lo.astype(F32) == slot, aff[:, e:e + 1], 0.0).astype(BF16)
            acc = acc + _dot(pick, y_ref[0, e, pl.ds(lo, win_slow), :])
        o_ref[0, rows] = x_ref[0, rows] + g_ref[0] * acc

    def narrow(rows, base):
        lane = lax.broadcasted_iota(jnp.int32, (1, LANES), 1)
        lo_vec = jnp.zeros((1, LANES), F32)
        los = []
        for e in range(N_EXPERTS):
            lo = pl.multiple_of(lo_fast_ref[base + e], SLOT_ALIGN)
            los.append(lo)
            lo_vec = jnp.where(lane == e, lo.astype(F32), lo_vec)
        pos = pos_ref[0, rows]
        rel = jnp.where(pos >= 0.0, pos - lo_vec, -1.0).astype(BF16)
        rel_wide = _dot(rel, spread_ref[...])
        gate_wide = _dot(aff_ref[0, rows].astype(BF16), spread_ref[...])
        pick = jnp.where(rel_wide == slot_ref[...], gate_wide, 0.0).astype(BF16)
        y_cat = jnp.concatenate([y_ref[0, e, pl.ds(los[e], win_fast), :] for e in range(N_EXPERTS)], axis=0)
        o_ref[0, rows] = x_ref[0, rows] + g_ref[0] * _dot(pick, y_cat)

    for sub in range(per_step):
        t = step * per_step + sub
        rows = slice(sub * tile, (sub + 1) * tile)
        base = (b * nt + t) * N_EXPERTS
        if win_fast == win_slow:
            wide(rows, base)
        else:
            fast = fast_ref[b * nt + t] == 1
            pl.when(fast)(functools.partial(narrow, rows, base))
            pl.when(jnp.logical_not(fast))(functools.partial(wide, rows, base))


def _combine(windows, x, ye, pos, aff, g2, cap):
    b, n, d = x.shape
    e = ye.shape[1]
    nt = n // ROUTE_TILE
    per_step = _tiles_per_step(nt)
    tok = lambda w: pl.BlockSpec((1, per_step * ROUTE_TILE, w), lambda bi, t, *_: (bi, t, 0))
    win_fast = _fast_window(cap)
    j = jnp.arange(N_EXPERTS * win_fast)
    spread = (jnp.arange(LANES)[:, None] == (j // win_fast)[None, :]).astype(BF16)
    slot = (j % win_fast).astype(F32)[None, :]
    grid_spec = pltpu.PrefetchScalarGridSpec(
        num_scalar_prefetch=3,
        grid=(b, nt // per_step),
        in_specs=[
            tok(d),
            pl.BlockSpec((1, e, cap, d), lambda bi, t, *_: (bi, 0, 0, 0)),
            tok(LANES),
            tok(LANES),
            pl.BlockSpec((1, 1, d), lambda bi, t, *_: (bi, 0, 0)),
            pl.BlockSpec(spread.shape, lambda bi, t, *_: (0, 0)),
            pl.BlockSpec(slot.shape, lambda bi, t, *_: (0, 0)),
        ],
        out_specs=tok(d),
    )
    return pl.pallas_call(
        functools.partial(_combine_kernel, win_fast=win_fast, win_slow=_slot_window(cap)),
        grid_spec=grid_spec,
        out_shape=jax.ShapeDtypeStruct((b, n, d), F32),
        compiler_params=_cparams(("arbitrary", "arbitrary")),
        name="combine",
    )(windows["lo_fast"], windows["lo_slow"], windows["fast_ok"], x, ye, pos, aff, g2, spread, slot)


def _route_and_dispatch(h2, logits):
    b, n, _ = h2.shape
    cap = EC_CAPACITY_FACTOR * n // N_EXPERTS
    nt = n // ROUTE_TILE
    assert n % ROUTE_TILE == 0 and cap % SLOT_ALIGN == 0
    pos, aff, lo = _route(logits, cap)
    windows = _slot_windows(lo[:, :, :N_EXPERTS], cap)
    pos_t = pos[:, :, :N_EXPERTS].reshape(b, nt, ROUTE_TILE, N_EXPERTS).transpose(0, 1, 3, 2)
    xe = _dispatch(windows, h2, pos_t, cap)
    return xe, {"windows": windows, "pos": pos, "aff": aff, "cap": cap}


def _combine_residual(x, ye, routing, g2):
    return _combine(routing["windows"], x, ye, routing["pos"], routing["aff"], g2, routing["cap"])


def _dft_mats(n):
    k = jnp.arange(n, dtype=jnp.int32)
    ang = ((k[:, None] * k[None, :]) % n).astype(F32) * (2.0 * math.pi / n)
    return jnp.cos(ang), jnp.sin(ang)


def _dft_mats_bf16(n, block=64):
    assert n % block == 0
    k = jnp.arange(n // 2, dtype=jnp.int32)
    a = jnp.arange(n // block, dtype=jnp.int32) * block
    b = jnp.arange(block, dtype=jnp.int32)
    ang_a = ((a[:, None] * k[None, :]) % n).astype(F32) * (2.0 * math.pi / n)
    ang_b = ((b[:, None] * k[None, :]) % n).astype(F32) * (2.0 * math.pi / n)
    ca, sa = jnp.cos(ang_a)[:, None, :], jnp.sin(ang_a)[:, None, :]
    cb, sb = jnp.cos(ang_b)[None], jnp.sin(ang_b)[None]
    cos_m = (ca * cb - sa * sb).reshape(n, n // 2)
    nsin_m = (-(sa * cb + ca * sb)).reshape(n, n // 2)
    return cos_m.astype(BF16), nsin_m.astype(BF16)


def _chan_dft():
    c, s = _dft_mats(FT_GROUP_DIM)
    eye = jnp.eye(FT_GROUPS, dtype=F32)
    return jnp.concatenate([jnp.kron(eye, c), jnp.kron(eye, s)], axis=1).astype(BF16)


def _group_mean_mat(group):
    gid = jnp.arange(HEAD_W) // group
    return jnp.where(gid[:, None] == gid[None, :], 1.0 / group, 0.0).astype(BF16)


def _rope_tables(n):
    t = jnp.arange(n)
    row = (t // GRID_W).astype(F32)
    col = (t % GRID_W).astype(F32)
    ax = DF_QK_DIM // 2
    inv = ROPE_BASE ** (-jnp.arange(0, ax, 2, dtype=F32) / ax)
    lane = jnp.arange(LANES)
    freq = inv[lane % (ax // 2)]
    pos = jnp.where(((lane % DF_QK_DIM) < ax)[None, :], row[:, None], col[:, None])
    ang = pos * freq[None, :]
    sign = jnp.where((lane % ax) < ax // 2, -1.0, 1.0)
    return jnp.cos(ang), jnp.sin(ang) * sign[None, :]


def kernel(x, c, ctx, c_ctx, norm1_g, norm2_g, w_ada, b_ada, w_in, na_qn_g, na_kn_g, na_rpb, df_qn_g, df_kn_g,
           df_lambda, df_subln_g, w_ft, w_na_o, w_df_o, w_out, w_router, w_gate, w_up, w_down):
    b, n, d = x.shape
    lc = ctx.shape[1]
    depth = w_ada.shape[0]
    assert b + 1 <= MOD_ROWS and n % GRID_W == 0

    cc = jnp.zeros((MOD_ROWS, d), F32).at[:b].set(c).at[b].set(c_ctx)
    mods = _ada(cc, w_ada, b_ada)

    consts = {"chan_dft": _chan_dft(), "g64": _group_mean_mat(NA_HEAD_DIM), "g32": _group_mean_mat(DF_QK_DIM)}
    rope_lat = _rope_tables(n)
    rope_ctx = (jnp.zeros((lc, LANES), F32), jnp.zeros((lc, LANES), F32))
    dft_lat = _dft_mats_bf16(n)
    dft_ctx = _dft_mats_bf16(lc)
    rows = n // GRID_W

    w_gate_b, w_up_b, w_down_b = w_gate.astype(BF16), w_up.astype(BF16), w_down.astype(BF16)

    xc = ctx
    for i in range(depth):
        last = i == depth - 1
        lam_init = 0.8 - 0.6 * math.exp(-0.3 * i)
        m_lat = mods[i, :b].reshape(b, 6, 1, d)
        m_ctx = jnp.broadcast_to(mods[i, b].reshape(1, 6, 1, d), (b, 6, 1, d))
        sh1, sc1, g1, sh2, sc2, g2 = [m_lat[:, j] for j in range(6)]
        csh1, csc1, cg1, csh2, csc2, cg2 = [m_ctx[:, j] for j in range(6)]

        w_in_b = w_in[i].astype(BF16)
        gains = {
            "naq": jnp.tile(na_qn_g[i], NA_HEADS).reshape(1, HEAD_W),
            "nak": jnp.tile(na_kn_g[i], NA_HEADS).reshape(1, HEAD_W),
            "dfq": jnp.tile(df_qn_g[i], 2 * DF_HEADS).reshape(1, HEAD_W),
            "dfk": jnp.tile(df_kn_g[i], 2 * DF_HEADS).reshape(1, HEAD_W),
        }
        n1g = norm1_g[i].reshape(1, d)
        n2g = norm2_g[i].reshape(1, d)
        subln = jnp.tile(df_subln_g[i], 2).reshape(1, LANES)
        w_ft_b, w_na_b, w_df_b, w_out_b = (w.astype(BF16) for w in (w_ft[i], w_na_o[i], w_df_o[i], w_out[i]))
        wr = jnp.zeros((d, ROUTER_PAD), F32).at[:, :N_EXPERTS].set(w_router[i])
        wr_hi, wr_lo = _split_bf16(wr)

        if last:
            nkc, nvc, dkc, dvc = _inproj(xc, csh1, csc1, n1g, w_in_b, consts, gains, rope_ctx,
                                         rope=False, kv_only=True)
        else:
            fabc, nqc, dqc, nkc, nvc, dkc, dvc, gatec = _inproj(xc, csh1, csc1, n1g, w_in_b, consts, gains,
                                                                 rope_ctx, rope=False, kv_only=False)

        fab, nq, dq, nk, nv, dk, dv, gate = _inproj(x, sh1, sc1, n1g, w_in_b, consts, gains, rope_lat,
                                                     rope=True, kv_only=False)
        f = _fourier(fab, *dft_lat)
        o_na = _na_attention(nq, nk, nv, nkc, nvc, *_na_bias_table(na_rpb[i], rows))
        o_df = _diff_attention(dq, [(dk, dv), (dkc, dvc)], df_lambda[i], subln, lam_init)
        x, h2, logits = _merge(x, f, o_na, o_df, gate, g1, w_ft_b, w_na_b, w_df_b, w_out_b, n2g, sh2, sc2,
                               wr_hi, wr_lo)
        xe, routing = _route_and_dispatch(h2, logits)

        if last:
            (ye,) = _experts([xe], w_gate_b, w_up_b, w_down_b, i)
        else:
            fc = _fourier(fabc, *dft_ctx)
            o_nac = _ctx_na_attention(nqc, nkc, nvc)
            o_dfc = _diff_attention(dqc, [(dkc, dvc)], df_lambda[i], subln, lam_init)
            xc, hc2, logits_c = _merge(xc, fc, o_nac, o_dfc, gatec, cg1, w_ft_b, w_na_b, w_df_b, w_out_b, n2g,
                                       csh2, csc2, wr_hi, wr_lo)
            xec, routing_c = _route_and_dispatch(hc2, logits_c)
            ye, yec = _experts([xe, xec], w_gate_b, w_up_b, w_down_b, i)
            xc = _combine_residual(xc, yec, routing_c, cg2)
        x = _combine_residual(x, ye, routing, g2)
    return x
```
